```python
import math
import jax, jax.numpy as jnp
from jax import lax
import numpy as np

D_MODEL = 1024
BATCH = 8
SEQ = 2048
DEPTH = 2

GRID_W = 64
CTX_LEN = 256
N_HEADS_A = 4
DH_A = 64
DV_A = 2 * DH_A
WIDTH_A = N_HEADS_A * DV_A
N_HEADS_B = 4
DK_B = 64
DV_B = 64
WIDTH_BK = N_HEADS_B * DK_B
WIDTH_B = N_HEADS_B * DV_B
FN_GROUPS = 4
FN_GROUP_DIM = 64
WIDTH_C = FN_GROUPS * FN_GROUP_DIM
MIX_WIDTH = WIDTH_A + WIDTH_B + WIDTH_C
PROJ_SIZES = (WIDTH_A, WIDTH_A, WIDTH_A, WIDTH_BK, WIDTH_BK, WIDTH_BK, WIDTH_B, WIDTH_B, WIDTH_C)
PROJ_WIDTH = 3 * WIDTH_A + 3 * WIDTH_BK + 2 * WIDTH_B + WIDTH_C
D_FF = 2816
CONV_W = 3
Q_BLOCK = 128
CHUNK = 64
ROPE_BASE = 10000.0
EPS = 1e-6

kernel_name = 'hybrid_diffattn_hgrn2_fnet_convffn_dit'


def rms_norm(x, w):
    xf = x.astype(jnp.float32)
    y = xf * lax.rsqrt(jnp.mean(xf * xf, axis=-1, keepdims=True) + EPS)
    return (y * w.astype(jnp.float32)).astype(x.dtype)


def modulate(h, shift, scale):
    return h * (1 + scale) + shift


def flip(a):
    return jnp.flip(a, axis=1)


def axial_rope(rows):
    pos_r = jnp.repeat(jnp.arange(rows), GRID_W).astype(jnp.float32)
    pos_c = jnp.tile(jnp.arange(GRID_W), rows).astype(jnp.float32)
    half = DH_A // 2
    inv_freq = ROPE_BASE ** (-jnp.arange(0, half, 2, dtype=jnp.float32) / half)
    ang_r = pos_r[:, None] * inv_freq
    ang_c = pos_c[:, None] * inv_freq
    ang = jnp.concatenate([ang_r, ang_r, ang_c, ang_c], axis=-1)
    return jnp.cos(ang), jnp.sin(ang)


def _rotate_half(p):
    h = p.shape[-1] // 2
    return jnp.concatenate([-p[..., h:], p[..., :h]], axis=-1)


def apply_axial_rope(t, cos, sin):
    t_row, t_col = jnp.split(t, 2, axis=-1)
    rot = jnp.concatenate([_rotate_half(t_row), _rotate_half(t_col)], axis=-1)
    cos = cos[None, :, None, None, :]
    sin = sin[None, :, None, None, :]
    return (t * cos + rot * sin).astype(t.dtype)


def diff_attend(q, k, v, lam):
    s = jnp.einsum('bqhcd,bkhcd->bhcqk', q, k).astype(jnp.float32) * (DH_A ** -0.5)
    p = jax.nn.softmax(s, axis=-1)
    a = p[:, :, 0] - lam * p[:, :, 1]
    return jnp.einsum('bhqk,bkhd->bqhd', a, v.astype(jnp.float32))


def diff_attend_blocks(q, k, v, lam):
    b, l = q.shape[:2]
    nb = l // Q_BLOCK
    qb = q.reshape(b, nb, Q_BLOCK, N_HEADS_A, 2, DH_A).transpose(1, 0, 2, 3, 4, 5)
    o = lax.map(lambda blk: diff_attend(blk, k, v, lam), qb)
    return o.transpose(1, 0, 2, 3, 4).reshape(b, l, N_HEADS_A, DV_A)


def diff_head_norm(o, w, lam_init):
    b, l = o.shape[:2]
    return (rms_norm(o, w) * (1.0 - lam_init)).reshape(b, l, WIDTH_A)


def log_forget(z, lb):
    zf = z.astype(jnp.float32)
    return jnp.logaddexp(jnp.log(lb), jnp.log1p(-lb) + jax.nn.log_sigmoid(zf))


def hgrn_scan(q, k, v, logf, s0):
    b, l, h, _ = q.shape
    dv = v.shape[-1]
    nc = l // CHUNK

    def chunks(a):
        return a.reshape(b, nc, CHUNK, h, a.shape[-1]).transpose(1, 0, 3, 2, 4)

    lower = jnp.tril(jnp.ones((CHUNK, CHUNK), dtype=bool))[:, :, None]

    def step(state, blk):
        qc, kc, vc, gc = blk
        cum = jnp.cumsum(gc, axis=2)
        rel = jnp.where(lower, cum[:, :, :, None, :] - cum[:, :, None, :, :], -jnp.inf)
        scores = jnp.einsum('bhtd,bhsd,bhtsd->bhts', qc, kc, jnp.exp(rel))
        o = (jnp.einsum('bhts,bhsv->bhtv', scores, vc)
             + jnp.einsum('bhtd,bhdv->bhtv', qc * jnp.exp(cum), state))
        cum_end = cum[:, :, -1:, :]
        state = (jnp.exp(cum_end)[:, :, 0, :, None] * state
                 + jnp.einsum('bhsd,bhsv->bhdv', kc * jnp.exp(cum_end - cum), vc))
        return state, o

    state, o = lax.scan(step, s0, (chunks(q), chunks(k), chunks(v), chunks(logf)))
    return o.transpose(1, 0, 3, 2, 4).reshape(b, l, h, dv), state


def hgrn_final_state(k, v, logf):
    tail = flip(jnp.cumsum(flip(logf), axis=1)) - logf
    return jnp.einsum('blhk,blhv->bhkv', k * jnp.exp(tail), v)


def hgrn_bidir(m, s_fwd, s_bwd):
    o_f, st_f = hgrn_scan(m['qh'], m['kf'], m['vh'], m['gf'], s_fwd)
    o_b, st_b = hgrn_scan(flip(m['qh']), flip(m['kb']), flip(m['vh']), flip(m['gb']), s_bwd)
    return o_f + flip(o_b), st_f, st_b


def hgrn_out(o, og, w):
    b, l = o.shape[:2]
    return (rms_norm(o, w) * jax.nn.silu(og.astype(jnp.float32))).reshape(b, l, WIDTH_B)


def fourier_mix(u, w):
    b, l, _ = u.shape
    ug = u.astype(jnp.float32).reshape(b, l, FN_GROUPS, FN_GROUP_DIM)
    y = jnp.fft.fft2(ug, axes=(1, 3), norm='ortho').real.reshape(b, l, WIDTH_C)
    return y.astype(u.dtype) @ w


def conv_ffn(h, w_up, conv_w, conv_b, w_down):
    u = h @ w_up
    up = jnp.pad(u, ((0, 0), (1, 1), (0, 0)))
    u = up[:, :-2] * conv_w[0] + up[:, 1:-1] * conv_w[1] + up[:, 2:] * conv_w[2] + conv_b
    gate, val = jnp.split(u, 2, axis=-1)
    return (jax.nn.silu(gate) * val) @ w_down


def unpack(p, lb):
    b, l, _ = p.shape
    idx = np.cumsum(PROJ_SIZES)[:-1].tolist()
    qa, ka, va, qh, zf, zb, ih, gh, uf = jnp.split(p, idx, axis=-1)
    gf = log_forget(zf, lb[0]).reshape(b, l, N_HEADS_B, DK_B)
    gb = log_forget(zb, lb[1]).reshape(b, l, N_HEADS_B, DK_B)
    return {
        'qa': qa.reshape(b, l, N_HEADS_A, 2, DH_A),
        'ka': ka.reshape(b, l, N_HEADS_A, 2, DH_A),
        'va': va.reshape(b, l, N_HEADS_A, DV_A),
        'qh': qh.reshape(b, l, N_HEADS_B, DK_B).astype(jnp.float32),
        'gf': gf, 'gb': gb,
        'kf': -jnp.expm1(gf), 'kb': -jnp.expm1(gb),
        'vh': ih.reshape(b, l, N_HEADS_B, DV_B).astype(jnp.float32),
        'og': gh.reshape(b, l, N_HEADS_B, DV_B),
        'uf': uf,
    }


def merge(att, rec, four, w_out, dt):
    return jnp.concatenate([att.astype(dt), rec.astype(dt), four.astype(dt)], axis=-1) @ w_out


def setup_inputs(seed: int = 0) -> dict:
    key = jax.random.key(seed)
    ks = jax.random.split(key, 20)
    f32 = jnp.float32

    def nrm(k, shape, s):
        return s * jax.random.normal(k, shape, f32)

    return {
        'x': nrm(ks[0], (BATCH, SEQ, D_MODEL), 1.0),
        'c': nrm(ks[1], (BATCH, D_MODEL), 1.0),
        'ctx': nrm(ks[2], (BATCH, CTX_LEN, D_MODEL), 1.0),
        'c_ctx': nrm(ks[3], (D_MODEL,), 1.0),
        'w_ada': nrm(ks[4], (DEPTH, D_MODEL, 6 * D_MODEL), 0.5 * D_MODEL ** -0.5),
        'b_ada': nrm(ks[5], (DEPTH, 6 * D_MODEL), 0.02),
        'norm1_w': 1.0 + nrm(ks[6], (DEPTH, D_MODEL), 0.05),
        'norm2_w': 1.0 + nrm(ks[7], (DEPTH, D_MODEL), 0.05),
        'w_in': nrm(ks[8], (DEPTH, D_MODEL, PROJ_WIDTH), D_MODEL ** -0.5),
        'lam_qk': nrm(ks[9], (DEPTH, 4, DH_A), 0.1),
        'subln_w': 1.0 + nrm(ks[10], (DEPTH, DV_A), 0.05),
        'lb_param': nrm(ks[11], (DEPTH, 2, WIDTH_BK), 0.5),
        'hgrn_norm_w': 1.0 + nrm(ks[12], (DEPTH, DV_B), 0.05),
        'w_fnet': nrm(ks[13], (DEPTH, WIDTH_C, WIDTH_C), WIDTH_C ** -0.5),
        'w_out': nrm(ks[14], (DEPTH, MIX_WIDTH, D_MODEL), MIX_WIDTH ** -0.5),
        'w_up': nrm(ks[15], (DEPTH, D_MODEL, 2 * D_FF), D_MODEL ** -0.5),
        'conv_w': jnp.array([0.0, 1.0, 0.0], f32)[None, :, None] + nrm(ks[16], (DEPTH, CONV_W, 2 * D_FF), 0.2),
        'conv_b': nrm(ks[17], (DEPTH, 2 * D_FF), 0.02),
        'w_down': nrm(ks[18], (DEPTH, D_FF, D_MODEL), D_FF ** -0.5),
        'final_norm_w': 1.0 + nrm(ks[19], (D_MODEL,), 0.05),
    }


def reference(x, c, ctx, c_ctx, w_ada, b_ada, norm1_w, norm2_w, w_in, lam_qk, subln_w,
              lb_param, hgrn_norm_w, w_fnet, w_out, w_up, conv_w, conv_b, w_down, final_norm_w):
    b, l, _ = x.shape
    rows = l // GRID_W
    cos, sin = axial_rope(rows)
    lb_all = jnp.cumsum(jax.nn.softmax(lb_param.astype(jnp.float32), axis=0), axis=0)
    lb_all = lb_all - lb_all[0:1]
    c_act = jax.nn.silu(c)
    cc_act = jax.nn.silu(c_ctx)
    xc = ctx
    for li in range(DEPTH):
        last = li == DEPTH - 1
        dt = x.dtype
        mod = c_act @ w_ada[li] + b_ada[li]
        sh1, sc1, g1, sh2, sc2, g2 = jnp.split(mod[:, None, :], 6, axis=-1)
        mod_c = cc_act @ w_ada[li] + b_ada[li]
        sh1c, sc1c, g1c, sh2c, sc2c, g2c = jnp.split(mod_c[None, None, :], 6, axis=-1)
        lam_init = 0.8 - 0.6 * math.exp(-0.3 * li)
        lq1, lk1, lq2, lk2 = lam_qk[li].astype(jnp.float32)
        lam = jnp.exp(jnp.sum(lq1 * lk1)) - jnp.exp(jnp.sum(lq2 * lk2)) + lam_init

        m = unpack(modulate(rms_norm(x, norm1_w[li]), sh1, sc1) @ w_in[li], lb_all[li])
        mc = unpack(modulate(rms_norm(xc, norm1_w[li]), sh1c, sc1c) @ w_in[li], lb_all[li])

        k_all = jnp.concatenate([mc['ka'], apply_axial_rope(m['ka'], cos, sin)], axis=1)
        v_all = jnp.concatenate([mc['va'], m['va']], axis=1)
        att = diff_head_norm(diff_attend_blocks(apply_axial_rope(m['qa'], cos, sin), k_all, v_all, lam),
                             subln_w[li], lam_init)

        if last:
            s_f = hgrn_final_state(mc['kf'], mc['vh'], mc['gf'])
            s_b = hgrn_final_state(flip(mc['kb']), flip(mc['vh']), flip(mc['gb']))
        else:
            zeros = jnp.zeros((xc.shape[0], N_HEADS_B, DK_B, DV_B), jnp.float32)
            o_c, s_f, s_b = hgrn_bidir(mc, zeros, zeros)
        o_l, _, _ = hgrn_bidir(m, s_f, s_b)
        rec = hgrn_out(o_l, m['og'], hgrn_norm_w[li])

        four = fourier_mix(m['uf'], w_fnet[li])

        x = x + g1 * merge(att, rec, four, w_out[li], dt)
        x = x + g2 * conv_ffn(modulate(rms_norm(x, norm2_w[li]), sh2, sc2),
                              w_up[li], conv_w[li], conv_b[li], w_down[li])

        if not last:
            att_c = diff_head_norm(diff_attend(mc['qa'], mc['ka'], mc['va'], lam), subln_w[li], lam_init)
            rec_c = hgrn_out(o_c, mc['og'], hgrn_norm_w[li])
            four_c = fourier_mix(mc['uf'], w_fnet[li])
            xc = xc + g1c * merge(att_c, rec_c, four_c, w_out[li], dt)
            xc = xc + g2c * conv_ffn(modulate(rms_norm(xc, norm2_w[li]), sh2c, sc2c),
                                     w_up[li], conv_w[li], conv_b[li], w_down[li])
    return rms_norm(x, final_norm_w)
```

```python
import functools
import math

import numpy as np
import jax
import jax.numpy as jnp
from jax import lax
from jax.experimental import pallas as pl
from jax.experimental.pallas import tpu as pltpu

F32 = jnp.float32
BF16 = jnp.bfloat16

D_MODEL = 1024
GRID_W = 64
N_HEADS_A = 4
DH_A = 64
DV_A = 2 * DH_A
WIDTH_A = N_HEADS_A * DV_A
N_HEADS_B = 4
DK_B = 64
DV_B = 64
WIDTH_B = N_HEADS_B * DV_B
FN_GROUPS = 4
FN_GROUP_DIM = 64
WIDTH_C = FN_GROUPS * FN_GROUP_DIM
PROJ_WIDTH = 3 * WIDTH_A + 5 * WIDTH_B + WIDTH_C
D_FF = 2816
ROPE_BASE = 10000.0
EPS = 1e-6

V7X_LANES = 128
V7X_BF16_SUBLANES = 16
V7X_VMEM_BYTES = 64 * 1024 * 1024
VMEM_LIMIT = V7X_VMEM_BYTES - 8 * 1024 * 1024

HG_CHUNK = 64
HG_LEVELS = 6
HG_WIDTHS = tuple(HG_CHUNK >> (l + 1) for l in range(HG_LEVELS))
FFN_COLS = 256
HALO = V7X_BF16_SUBLANES


def _params(sem):
    return pltpu.CompilerParams(dimension_semantics=sem, vmem_limit_bytes=VMEM_LIMIT)


def _const_spec(shape, index_map):
    return pl.BlockSpec(shape, index_map, pipeline_mode=pl.Buffered(1))


def _dot(a, b):
    return jnp.dot(a, b, preferred_element_type=F32)


def _dot_nt(a, b):
    return lax.dot_general(a, b, (((1,), (1,)), ((), ())), preferred_element_type=F32)


def _dot_tn(a, b):
    return lax.dot_general(a, b, (((0,), (0,)), ((), ())), preferred_element_type=F32)


def _split3(x):
    h = x.astype(BF16)
    r = x - h.astype(F32)
    m = r.astype(BF16)
    l = (r - m.astype(F32)).astype(BF16)
    return h, m, l


def _rms_scale(x):
    return lax.rsqrt(jnp.mean(x * x, axis=-1, keepdims=True) + EPS)


def _mod_kernel(c_ref, w_ref, b_ref, o_ref):
    c = c_ref[...]
    a = (c * jax.nn.sigmoid(c)).astype(BF16)
    o_ref[0] = _dot(a, w_ref[0].astype(BF16)) + b_ref[0]


def _modulation(cstack, w_ada, b_ada):
    depth, _, n = w_ada.shape
    tn = 1536
    rows = cstack.shape[0]
    return pl.pallas_call(
        _mod_kernel,
        grid=(depth, n // tn),
        in_specs=[
            pl.BlockSpec((rows, D_MODEL), lambda l, j: (0, 0)),
            pl.BlockSpec((1, D_MODEL, tn), lambda l, j: (l, 0, j)),
            pl.BlockSpec((1, 1, tn), lambda l, j: (l, 0, j)),
        ],
        out_specs=pl.BlockSpec((1, rows, tn), lambda l, j: (l, 0, j)),
        out_shape=jax.ShapeDtypeStruct((depth, rows, n), F32),
        compiler_params=_params(("arbitrary", "arbitrary")),
        name="modulation",
    )(cstack, w_ada, b_ada.reshape(depth, 1, n))


def _log_forget(z, log_lb, log1m_lb):
    ls = jnp.minimum(z, 0.0) - jnp.log1p(jnp.exp(-jnp.abs(z)))
    b = log1m_lb + ls
    return jnp.maximum(log_lb, b) + jnp.log1p(jnp.exp(-jnp.abs(log_lb - b)))


def _inproj_kernel(rope, x_ref, mod_ref, nw_ref, w_ref, lbc_ref, *rest):
    if rope:
        cos_ref, sa_ref, sb_ref, qa_ref, ka_ref, va_ref, hg_ref, uf_ref = rest
    else:
        qa_ref, ka_ref, va_ref, hg_ref, uf_ref = rest
    x = x_ref[...]
    m = mod_ref[...]
    y = x * _rms_scale(x) * nw_ref[...]
    h = (y * (1.0 + m[1:2]) + m[0:1]).astype(BF16)

    def proj(c0, n):
        return _dot(h, w_ref[:, c0:c0 + n])

    for base, ref, scale in ((0, qa_ref, DH_A ** -0.5), (WIDTH_A, ka_ref, None)):
        t = proj(base, WIDTH_A)
        for hd in range(N_HEADS_A):
            th = t[:, hd * DV_A:(hd + 1) * DV_A]
            if rope:
                th = (th * cos_ref[...]
                      + pltpu.roll(th, V7X_LANES - 16, 1) * sa_ref[...]
                      + pltpu.roll(th, 16, 1) * sb_ref[...])
            if scale is not None:
                th = th * scale
            ref[:, hd * DV_A:(hd + 1) * DV_A] = th.astype(BF16)
    va_ref[...] = proj(2 * WIDTH_A, WIDTH_A).astype(BF16)
    c0 = 3 * WIDTH_A
    hg_ref[:, 0:WIDTH_B] = proj(c0, WIDTH_B)
    lbc = lbc_ref[...]
    for dirn in range(2):
        z = proj(c0 + WIDTH_B * (1 + dirn), WIDTH_B)
        hg_ref[:, WIDTH_B * (1 + dirn):WIDTH_B * (2 + dirn)] = _log_forget(
            z, lbc[2 * dirn:2 * dirn + 1], lbc[2 * dirn + 1:2 * dirn + 2])
    hg_ref[:, 3 * WIDTH_B:5 * WIDTH_B] = proj(c0 + 3 * WIDTH_B, 2 * WIDTH_B)
    uf_ref[...] = proj(c0 + 5 * WIDTH_B, WIDTH_C).astype(BF16)


def _inproj(x2d, mod4, nw, w_bf, lbc, rope_tabs, li, seg_len, mod_row, tm):
    rows = x2d.shape[0]
    tpb = seg_len // tm
    rope = rope_tabs is not None
    mod_map = (lambda i: (li, i // tpb, 0, 0)) if mod_row is None else (lambda i: (li, mod_row, 0, 0))
    in_specs = [
        pl.BlockSpec((tm, D_MODEL), lambda i: (i, 0)),
        pl.BlockSpec((None, None, 6, D_MODEL), mod_map),
        pl.BlockSpec((None, 1, D_MODEL), lambda i: (li, 0, 0)),
        _const_spec((None, D_MODEL, PROJ_WIDTH), lambda i: (li, 0, 0)),
        pl.BlockSpec((None, 4, WIDTH_B), lambda i: (li, 0, 0)),
    ]
    args = [x2d, mod4, nw, w_bf, lbc]
    if rope:
        in_specs += [pl.BlockSpec((tm, DV_A), lambda i: (i % tpb, 0))] * 3
        args += list(rope_tabs)
    out_shape = (
        jax.ShapeDtypeStruct((rows, WIDTH_A), BF16),
        jax.ShapeDtypeStruct((rows, WIDTH_A), BF16),
        jax.ShapeDtypeStruct((rows, WIDTH_A), BF16),
        jax.ShapeDtypeStruct((rows, 5 * WIDTH_B), F32),
        jax.ShapeDtypeStruct((rows, WIDTH_C), BF16),
    )
    out_specs = (
        pl.BlockSpec((tm, WIDTH_A), lambda i: (i, 0)),
        pl.BlockSpec((tm, WIDTH_A), lambda i: (i, 0)),
        pl.BlockSpec((tm, WIDTH_A), lambda i: (i, 0)),
        pl.BlockSpec((tm, 5 * WIDTH_B), lambda i: (i, 0)),
        pl.BlockSpec((tm, WIDTH_C), lambda i: (i, 0)),
    )
    return pl.pallas_call(
        functools.partial(_inproj_kernel, rope),
        grid=(rows // tm,),
        in_specs=in_specs,
        out_specs=out_specs,
        out_shape=out_shape,
        compiler_params=_params(("parallel",)),
        name="inproj_rope" if rope else "inproj",
    )(*args)


def _attn_kernel(nseg, tq, out_scale, lam_ref, q_ref, *rest):
    k_refs = rest[:nseg]
    v_refs = rest[nseg:2 * nseg]
    w_ref, o_ref = rest[2 * nseg:]
    q = q_ref[...].astype(F32)
    lane = lax.broadcasted_iota(jnp.int32, q.shape, 1)
    qs = jnp.concatenate([jnp.where(lane < DH_A, q, 0.0), jnp.where(lane >= DH_A, q, 0.0)],
                         axis=0).astype(BF16)
    scores = [_dot_nt(qs, k_ref[...]) for k_ref in k_refs]
    mx = jnp.max(scores[0], axis=-1, keepdims=True)
    for s in scores[1:]:
        mx = jnp.maximum(mx, jnp.max(s, axis=-1, keepdims=True))
    acc = None
    for s, v_ref in zip(scores, v_refs):
        p = jnp.exp(s - mx).astype(BF16)
        v = v_ref[...]
        vl = lax.broadcasted_iota(jnp.int32, v.shape, 1)
        ones_col = jnp.where(vl == 0, 1.0, 0.0).astype(BF16)
        part = _dot(p, jnp.concatenate([v, ones_col], axis=1))
        acc = part if acc is None else acc + part
    r = acc[:, :DV_A] / acc[:, DV_A:DV_A + 1]
    o = r[:tq] - lam_ref[0, 0] * r[tq:]
    o_ref[...] = (o * _rms_scale(o) * (w_ref[...] * out_scale)).astype(BF16)


def _attention(lam, q, ks, vs, subln_w, li, out_scale, tq):
    b, lq, _ = q.shape
    nseg = len(ks)
    in_specs = [
        pl.BlockSpec(memory_space=pltpu.SMEM),
        pl.BlockSpec((None, tq, DV_A), lambda bi, h, i: (bi, i, h)),
    ]
    for arr in list(ks) + list(vs):
        in_specs.append(pl.BlockSpec((None, arr.shape[1], DV_A), lambda bi, h, i: (bi, 0, h)))
    in_specs.append(pl.BlockSpec((None, 1, DV_A), lambda bi, h, i: (li, 0, 0)))
    return pl.pallas_call(
        functools.partial(_attn_kernel, nseg, tq, out_scale),
        grid=(b, N_HEADS_A, lq // tq),
        in_specs=in_specs,
        out_specs=pl.BlockSpec((None, tq, DV_A), lambda bi, h, i: (bi, i, h)),
        out_shape=jax.ShapeDtypeStruct((b, lq, WIDTH_A), BF16),
        compiler_params=_params(("parallel", "parallel", "arbitrary")),
        name="diff_attention_%dseg" % nseg,
    )(lam, q, *ks, *vs, subln_w)


def _hgrn_constants():
    c = HG_CHUNK
    t = np.arange(c)[:, None]
    r = np.arange(c)[None, :]
    blocks = []
    amask = []
    lomask = []
    for w in HG_WIDTHS:
        mid = (t // (2 * w)) * 2 * w + w
        hi = (t % (2 * w)) >= w
        eq = np.zeros((c, 2 * c), np.float32)
        eq[:, :c] = hi & (r >= mid) & (r <= t)
        eq[:, c:] = (~hi) & (r >= t) & (r <= mid - 1)
        ek = np.zeros((c, 2 * c), np.float32)
        ek[:, :c] = (~hi) & (r > t) & (r <= mid - 1)
        ek[:, c:] = hi & (r >= mid) & (r < t)
        blocks += [eq, ek]
        same_pair = (t // (2 * w)) == (r // (2 * w))
        amask.append(same_pair & (hi != ((r % (2 * w)) >= w)))
        lomask.append(np.broadcast_to(~hi, (c, c)))
    qf = np.zeros((c, 2 * c), np.float32); qf[:, :c] = r <= t
    kf = np.zeros((c, 2 * c), np.float32); kf[:, :c] = r > t
    qb = np.zeros((c, 2 * c), np.float32); qb[:, c:] = r >= t
    blocks += [qf, kf, qb]
    pmain = np.concatenate(blocks, axis=0)
    ppre = np.zeros((2 * c, c), np.float32)
    ppre[:c] = r < t
    ppre[c:c + 8] = 1.0
    amask.append(t == r)
    tile = lambda a: np.tile(np.asarray(a, np.float32), (1, N_HEADS_B))
    amask = np.stack([tile(a) for a in amask])
    lomask = np.stack([tile(a) for a in lomask])
    hd = np.arange(WIDTH_B) // DK_B
    bd = (hd[:, None] == hd[None, :]).astype(np.float32)
    return (jnp.asarray(pmain, BF16), jnp.asarray(ppre, BF16), jnp.asarray(amask, F32),
            jnp.asarray(lomask, F32), jnp.asarray(bd, F32))


def _hgrn_kernel(nc, q_ref, gf_ref, gb_ref, v_ref, og_ref, pmain_ref, ppre_ref, amask_ref, lomask_ref,
                 bd_ref, hw_ref, sf0_ref, sb0_ref, rec_ref, sf_ref, sb_ref, sbn_ref):
    c = HG_CHUNK
    bd = bd_ref[...]
    bd_bf = bd.astype(BF16)
    lane_head = lax.broadcasted_iota(jnp.int32, (c, WIDTH_B), 1) // DK_B

    def stack_heads(a):
        return jnp.concatenate([jnp.where(lane_head == h, a, 0.0).astype(BF16) for h in range(N_HEADS_B)],
                               axis=0)

    def rows(ref, n):
        return ref[pl.ds(pl.multiple_of(n * c, c), c), :]

    sb_ref[...] = sb0_ref[...]

    def pre_body(i, carry):
        n = nc - 1 - i
        gb = rows(gb_ref, n)
        sums = sum(_dot(ppre_ref[...], part) for part in _split3(gb))
        e = jnp.exp(sums)
        kb = 1.0 - jnp.exp(gb)
        kt = (kb * e[0:c]).astype(BF16)
        st = sb_ref[...]
        sbn_ref[n] = st.astype(BF16)
        upd = _dot_tn(rows(v_ref, n).astype(BF16), kt)
        sb_ref[...] = st * e[c:c + 1] + upd * bd
        return carry

    lax.fori_loop(0, nc, pre_body, 0)

    sf_ref[...] = sf0_ref[...]

    def body(n, carry):
        q = rows(q_ref, n)
        gf = rows(gf_ref, n)
        gb = rows(gb_ref, n)
        v = rows(v_ref, n)
        v_bf = v.astype(BF16)
        g = jnp.concatenate([gf, gb], axis=0)
        sums = sum(_dot(pmain_ref[...], part) for part in _split3(g))
        e = jnp.exp(sums)
        kf = 1.0 - jnp.exp(gf)
        kb = 1.0 - jnp.exp(gb)
        a_all = jnp.zeros((c, N_HEADS_B * c), F32)
        for lvl in range(HG_LEVELS):
            eq = e[(2 * lvl) * c:(2 * lvl + 1) * c]
            ek = e[(2 * lvl + 1) * c:(2 * lvl + 2) * c]
            klev = jnp.where(lomask_ref[lvl] > 0.5, kf, kb)
            qt = (q * eq).astype(BF16)
            a_all = a_all + _dot_nt(qt, stack_heads(klev * ek)) * amask_ref[lvl]
        diag = _dot((q * (kf + kb)).astype(BF16), bd_bf)
        a_all = a_all + diag * amask_ref[HG_LEVELS]
        o = _dot(a_all.astype(BF16), stack_heads(v))
        base = 2 * HG_LEVELS * c
        e_qf = e[base:base + c]
        e_kf = e[base + c:base + 2 * c]
        e_qb = e[base + 2 * c:base + 3 * c]
        st = sf_ref[...]
        o = o + _dot_nt((q * e_qf).astype(BF16), st.astype(BF16))
        o = o + _dot_nt((q * e_qb).astype(BF16), sbn_ref[n])
        upd = _dot_tn(v_bf, (kf * e_kf).astype(BF16))
        sf_ref[...] = st * e_qf[c - 1:c] + upd * bd
        sq_h, sq_m, _ = _split3(o * o)
        ms = (_dot(sq_h, bd_bf) + _dot(sq_m, bd_bf)) * (1.0 / DV_B)
        og = rows(og_ref, n)
        rec = o * lax.rsqrt(ms + EPS) * hw_ref[...] * (og * jax.nn.sigmoid(og))
        rec_ref[pl.ds(pl.multiple_of(n * c, c), c), :] = rec.astype(BF16)
        return carry

    lax.fori_loop(0, nc, body, 0)


def _hgrn(hg, hconst, hw, sf0, sb0, li):
    b, l, _ = hg.shape
    nc = l // HG_CHUNK
    pmain, ppre, amask, lomask, bd = hconst
    col = lambda j: pl.BlockSpec((None, l, WIDTH_B), lambda bi: (bi, 0, j))
    full = lambda a: _const_spec(a.shape, lambda bi: (0,) * a.ndim)
    st_spec = pl.BlockSpec((None, WIDTH_B, WIDTH_B), lambda bi: (bi, 0, 0))
    return pl.pallas_call(
        functools.partial(_hgrn_kernel, nc),
        grid=(b,),
        in_specs=[col(0), col(1), col(2), col(3), col(4), full(pmain), full(ppre), full(amask),
                  full(lomask), full(bd), pl.BlockSpec((None, 1, WIDTH_B), lambda bi: (li, 0, 0)),
                  st_spec, st_spec],
        out_specs=(pl.BlockSpec((None, l, WIDTH_B), lambda bi: (bi, 0, 0)), st_spec, st_spec),
        out_shape=(jax.ShapeDtypeStruct((b, l, WIDTH_B), BF16),
                   jax.ShapeDtypeStruct((b, WIDTH_B, WIDTH_B), F32),
                   jax.ShapeDtypeStruct((b, WIDTH_B, WIDTH_B), F32)),
        scratch_shapes=[pltpu.VMEM((nc, WIDTH_B, WIDTH_B), BF16)],
        compiler_params=_params(("parallel",)),
        name="hgrn_l%d" % l,
    )(hg, hg, hg, hg, hg, pmain, ppre, amask, lomask, bd, hw, sf0, sb0)


def _dft_constants(l):
    k = np.arange(FN_GROUP_DIM)
    ang = 2.0 * np.pi * ((k[:, None] * k[None, :]) % FN_GROUP_DIM) / FN_GROUP_DIM
    eye = np.eye(FN_GROUPS)
    cs = np.concatenate([np.kron(eye, np.cos(ang)), np.kron(eye, np.sin(ang))], axis=1)
    n = np.arange(l)
    angl = 2.0 * np.pi * ((n[:, None] * n[None, :]) % l) / l
    return tuple(jnp.asarray(a, F32).astype(BF16) for a in (cs, np.cos(angl), np.sin(angl)))


def _fnet_kernel(scale, u_ref, cs_ref, cl_ref, sl_ref, wf_ref, o_ref):
    t = _dot(u_ref[...], cs_ref[...])
    uc = t[:, :WIDTH_C].astype(BF16)
    us = t[:, WIDTH_C:].astype(BF16)
    y = (_dot(cl_ref[...], uc) - _dot(sl_ref[...], us)) * scale
    o_ref[...] = _dot(y.astype(BF16), wf_ref[...]).astype(BF16)


def _fnet(uf, dft, wf_bf, li):
    b, l, _ = uf.shape
    cs, cl, sl = dft
    scale = 1.0 / math.sqrt(l * FN_GROUP_DIM)
    return pl.pallas_call(
        functools.partial(_fnet_kernel, scale),
        grid=(b,),
        in_specs=[
            pl.BlockSpec((None, l, WIDTH_C), lambda bi: (bi, 0, 0)),
            _const_spec(cs.shape, lambda bi: (0, 0)),
            _const_spec(cl.shape, lambda bi: (0, 0)),
            _const_spec(sl.shape, lambda bi: (0, 0)),
            _const_spec((None, WIDTH_C, WIDTH_C), lambda bi: (li, 0, 0)),
        ],
        out_specs=pl.BlockSpec((None, l, WIDTH_C), lambda bi: (bi, 0, 0)),
        out_shape=jax.ShapeDtypeStruct((b, l, WIDTH_C), BF16),
        compiler_params=_params(("parallel",)),
        name="fnet_l%d" % l,
    )(uf, cs, cl, sl, wf_bf)


def _outproj_kernel(att_ref, rec_ref, four_ref, x_ref, mod_ref, nw_ref, wo_ref, x1_ref, h2_ref):
    mix = jnp.concatenate([att_ref[...], rec_ref[...], four_ref[...]], axis=1)
    m = mod_ref[...]
    x1 = x_ref[...] + m[2:3] * _dot(mix, wo_ref[...])
    x1_ref[...] = x1
    y = x1 * _rms_scale(x1) * nw_ref[...]
    h2_ref[...] = (y * (1.0 + m[4:5]) + m[3:4]).astype(BF16)


def _outproj(att, rec, four, x2d, mod4, nw2, wo_bf, li, seg_len, mod_row, tm):
    rows = x2d.shape[0]
    tpb = seg_len // tm
    mod_map = (lambda i: (li, i // tpb, 0, 0)) if mod_row is None else (lambda i: (li, mod_row, 0, 0))
    row_spec = lambda w: pl.BlockSpec((tm, w), lambda i: (i, 0))
    return pl.pallas_call(
        _outproj_kernel,
        grid=(rows // tm,),
        in_specs=[row_spec(WIDTH_A), row_spec(WIDTH_B), row_spec(WIDTH_C), row_spec(D_MODEL),
                  pl.BlockSpec((None, None, 6, D_MODEL), mod_map),
                  pl.BlockSpec((None, 1, D_MODEL), lambda i: (li, 0, 0)),
                  _const_spec((None, D_MODEL, D_MODEL), lambda i: (li, 0, 0))],
        out_specs=(row_spec(D_MODEL), row_spec(D_MODEL)),
        out_shape=(jax.ShapeDtypeStruct((rows, D_MODEL), F32), jax.ShapeDtypeStruct((rows, D_MODEL), BF16)),
        compiler_params=_params(("parallel",)),
        name="outproj",
    )(att, rec, four, x2d, mod4, nw2, wo_bf)


def _ffn_kernel(tm, tpb, final, h_ref, hp_ref, hn_ref, x1_ref, mod_ref, wup_ref, cw_ref, cb_ref, wdn_ref,
                fw_ref, o_ref, a_ref):
    i = pl.program_id(0)
    t = i % tpb
    prev = (hp_ref[...].astype(F32) * jnp.where(t > 0, 1.0, 0.0)).astype(BF16)
    nxt = (hn_ref[...].astype(F32) * jnp.where(t < tpb - 1, 1.0, 0.0)).astype(BF16)
    hext = jnp.concatenate([prev, h_ref[...], nxt], axis=0)
    ext = tm + 2 * HALO

    def conv(u, c0):
        w = cw_ref[:, c0:c0 + FFN_COLS]
        up = pltpu.roll(u, 1, 0)
        dn = pltpu.roll(u, ext - 1, 0)
        r = up * w[0:1] + u * w[1:2] + dn * w[2:3] + cb_ref[:, c0:c0 + FFN_COLS]
        return r[HALO:HALO + tm]

    for j in range(D_FF // FFN_COLS):
        c0 = j * FFN_COLS
        gate = conv(_dot(hext, wup_ref[:, c0:c0 + FFN_COLS]), c0)
        val = conv(_dot(hext, wup_ref[:, D_FF + c0:D_FF + c0 + FFN_COLS]), D_FF + c0)
        a_ref[:, c0:c0 + FFN_COLS] = (gate * jax.nn.sigmoid(gate) * val).astype(BF16)
    m = mod_ref[...]
    x2 = x1_ref[...] + m[5:6] * _dot(a_ref[...], wdn_ref[...])
    if final:
        x2 = x2 * _rms_scale(x2) * fw_ref[...]
    o_ref[...] = x2


def _ffn(h2, x1, mod4, wup_bf, conv_w, conv_b, wdn_bf, final_w, li, seg_len, mod_row, tm, final):
    rows = x1.shape[0]
    tpb = seg_len // tm
    hb = tm // HALO
    nhb = rows // HALO
    mod_map = (lambda i: (li, i // tpb, 0, 0)) if mod_row is None else (lambda i: (li, mod_row, 0, 0))
    return pl.pallas_call(
        functools.partial(_ffn_kernel, tm, tpb, final),
        grid=(rows // tm,),
        in_specs=[
            pl.BlockSpec((tm, D_MODEL), lambda i: (i, 0)),
            pl.BlockSpec((HALO, D_MODEL), lambda i: (jnp.maximum(i * hb - 1, 0), 0)),
            pl.BlockSpec((HALO, D_MODEL), lambda i: (jnp.minimum((i + 1) * hb, nhb - 1), 0)),
            pl.BlockSpec((tm, D_MODEL), lambda i: (i, 0)),
            pl.BlockSpec((None, None, 6, D_MODEL), mod_map),
            _const_spec((None, D_MODEL, 2 * D_FF), lambda i: (li, 0, 0)),
            pl.BlockSpec((None, 3, 2 * D_FF), lambda i: (li, 0, 0)),
            pl.BlockSpec((None, 1, 2 * D_FF), lambda i: (li, 0, 0)),
            _const_spec((None, D_FF, D_MODEL), lambda i: (li, 0, 0)),
            pl.BlockSpec((1, D_MODEL), lambda i: (0, 0)),
        ],
        out_specs=pl.BlockSpec((tm, D_MODEL), lambda i: (i, 0)),
        out_shape=jax.ShapeDtypeStruct((rows, D_MODEL), F32),
        scratch_shapes=[pltpu.VMEM((tm, D_FF), BF16)],
        compiler_params=_params(("parallel",)),
        name="convffn_final" if final else "convffn",
    )(h2, h2, h2, x1, mod4, wup_bf, conv_w, conv_b, wdn_bf, final_w)


def _rope_tables(l):
    rows = l // GRID_W
    pos_r = np.repeat(np.arange(rows), GRID_W).astype(np.float32)
    pos_c = np.tile(np.arange(GRID_W), rows).astype(np.float32)
    half = DH_A // 2
    inv_freq = (ROPE_BASE ** (-np.arange(0, half, 2, dtype=np.float32) / half)).astype(np.float32)
    ang = np.concatenate([pos_r[:, None] * inv_freq] * 2 + [pos_c[:, None] * inv_freq] * 2, axis=-1)
    cos = np.cos(ang.astype(np.float32)).astype(np.float32)
    sin = np.sin(ang.astype(np.float32)).astype(np.float32)
    cos = np.tile(cos, (1, 2))
    sin = np.tile(sin, (1, 2))
    first = (np.arange(DV_A) % 32) < 16
    sin_a = np.where(first, -sin, 0.0).astype(np.float32)
    sin_b = np.where(first, 0.0, sin).astype(np.float32)
    return jnp.asarray(cos), jnp.asarray(sin_a), jnp.asarray(sin_b)


def kernel(x, c, ctx, c_ctx, w_ada, b_ada, norm1_w, norm2_w, w_in, lam_qk, subln_w, lb_param, hgrn_norm_w,
           w_fnet, w_out, w_up, conv_w, conv_b, w_down, final_norm_w):
    b, l, d = x.shape
    lc = ctx.shape[1]
    depth = w_ada.shape[0]
    assert d == D_MODEL and l % GRID_W == 0 and b <= 8

    lb_all = jnp.cumsum(jax.nn.softmax(lb_param.astype(F32), axis=0), axis=0)
    lb_all = lb_all - lb_all[0:1]
    lbc = jnp.stack([jnp.log(lb_all[:, 0]), jnp.log1p(-lb_all[:, 0]),
                     jnp.log(lb_all[:, 1]), jnp.log1p(-lb_all[:, 1])], axis=1)
    lq = lam_qk.astype(F32)
    lam_dyn = jnp.exp(jnp.sum(lq[:, 0] * lq[:, 1], axis=-1)) - jnp.exp(jnp.sum(lq[:, 2] * lq[:, 3], axis=-1))
    rope_tabs = _rope_tables(l)
    hconst = _hgrn_constants()
    dft_l = _dft_constants(l)
    dft_c = _dft_constants(lc)
    w_in_bf = w_in.astype(BF16)
    w_out_bf = w_out.astype(BF16)
    w_up_bf = w_up.astype(BF16)
    w_dn_bf = w_down.astype(BF16)
    w_fn_bf = w_fnet.astype(BF16)
    nw1 = norm1_w.reshape(depth, 1, d)
    nw2 = norm2_w.reshape(depth, 1, d)
    sub_w = subln_w.reshape(depth, 1, DV_A)
    hw = jnp.tile(hgrn_norm_w, (1, N_HEADS_B)).reshape(depth, 1, WIDTH_B)
    cb = conv_b.reshape(depth, 1, 2 * D_FF)
    fw = final_norm_w.reshape(1, d)

    cstack = jnp.concatenate([c, c_ctx[None, :], jnp.zeros((16 - b - 1, d), F32)], axis=0)
    mod4 = _modulation(cstack, w_ada, b_ada).reshape(depth, 16, 6, d)
    ctx_row = b

    zeros_state = jnp.zeros((b, WIDTH_B, WIDTH_B), F32)
    xl = x.reshape(b * l, d)
    xc = ctx.reshape(b * lc, d)
    tm_l = 512
    tm_c = 256
    for li in range(depth):
        last = li == depth - 1
        lam_init = 0.8 - 0.6 * math.exp(-0.3 * li)
        lam = (lam_dyn[li] + lam_init).reshape(1, 1)

        qa, ka, va, hg, uf = _inproj(xl, mod4, nw1, w_in_bf, lbc, rope_tabs, li, l, None, tm_l)
        qc, kc, vc, hgc, ufc = _inproj(xc, mod4, nw1, w_in_bf, lbc, None, li, lc, ctx_row, tm_c)
        r3 = lambda a, n: a.reshape(b, n, a.shape[-1])

        att = _attention(lam, r3(qa, l), [r3(kc, lc), r3(ka, l)], [r3(vc, lc), r3(va, l)], sub_w, li,
                         1.0 - lam_init, 256)
        rec_c, sf, sb = _hgrn(r3(hgc, lc), hconst, hw, zeros_state, zeros_state, li)
        rec, _, _ = _hgrn(r3(hg, l), hconst, hw, sf, sb, li)
        four = _fnet(r3(uf, l), dft_l, w_fn_bf, li)

        x1, h2 = _outproj(att.reshape(b * l, -1), rec.reshape(b * l, -1), four.reshape(b * l, -1), xl, mod4,
                          nw2, w_out_bf, li, l, None, tm_l)
        xl = _ffn(h2, x1, mod4, w_up_bf, conv_w, cb, w_dn_bf, fw, li, l, None, tm_l, last)

        if not last:
            att_c = _attention(lam, r3(qc, lc), [r3(kc, lc)], [r3(vc, lc)], sub_w, li, 1.0 - lam_init, 256)
            four_c = _fnet(r3(ufc, lc), dft_c, w_fn_bf, li)
            x1c, h2c = _outproj(att_c.reshape(b * lc, -1), rec_c.reshape(b * lc, -1),
                                four_c.reshape(b * lc, -1), xc, mod4, nw2, w_out_bf, li, lc, ctx_row, tm_c)
            xc = _ffn(h2c, x1c, mod4, w_up_bf, conv_w, cb, w_dn_bf, fw, li, lc, ctx_row, tm_c, False)
    return xl.reshape(b, l, d)
```

```python
import functools
import math

import numpy as np
import jax
import jax.numpy as jnp
from jax import lax
from jax.experimental import pallas as pl
from jax.experimental.pallas import tpu as pltpu

F32 = jnp.float32
BF16 = jnp.bfloat16

D_MODEL = 1024
GRID_W = 64
N_HEADS_A = 4
DH_A = 64
DV_A = 2 * DH_A
WIDTH_A = N_HEADS_A * DV_A
N_HEADS_B = 4
DK_B = 64
DV_B = 64
WIDTH_B = N_HEADS_B * DV_B
FN_GROUPS = 4
FN_GROUP_DIM = 64
WIDTH_C = FN_GROUPS * FN_GROUP_DIM
PROJ_WIDTH = 3 * WIDTH_A + 5 * WIDTH_B + WIDTH_C
D_FF = 2816
ROPE_BASE = 10000.0
EPS = 1e-6

V7X_LANES = 128
V7X_BF16_SUBLANES = 16
V7X_VMEM_BYTES = 64 * 1024 * 1024
VMEM_LIMIT = V7X_VMEM_BYTES - 8 * 1024 * 1024

HG_CHUNK = 64
HG_LEVELS = 6
HG_WIDTHS = tuple(HG_CHUNK >> (l + 1) for l in range(HG_LEVELS))
HG_UNROLL = 2
FFN_COLS = 256
HALO = V7X_BF16_SUBLANES


def _params(sem):
    return pltpu.CompilerParams(dimension_semantics=sem, vmem_limit_bytes=VMEM_LIMIT)


def _const_spec(shape, index_map):
    return pl.BlockSpec(shape, index_map, pipeline_mode=pl.Buffered(1))


def _dot(a, b):
    return jnp.dot(a, b, preferred_element_type=F32)


def _dot_nt(a, b):
    return lax.dot_general(a, b, (((1,), (1,)), ((), ())), preferred_element_type=F32)


def _dot_tn(a, b):
    return lax.dot_general(a, b, (((0,), (0,)), ((), ())), preferred_element_type=F32)


def _split2(x):
    h = x.astype(BF16)
    return h, (x - h.astype(F32)).astype(BF16)


def _rms_scale(x):
    return lax.rsqrt(jnp.mean(x * x, axis=-1, keepdims=True) + EPS)


def _mod_kernel(c_ref, w_ref, b_ref, o_ref):
    c = c_ref[...]
    a = (c * jax.nn.sigmoid(c)).astype(BF16)
    o_ref[0] = _dot(a, w_ref[0].astype(BF16)) + b_ref[0]


def _modulation(cstack, w_ada, b_ada):
    depth, _, n = w_ada.shape
    tn = 1536
    rows = cstack.shape[0]
    return pl.pallas_call(
        _mod_kernel,
        grid=(depth, n // tn),
        in_specs=[
            pl.BlockSpec((rows, D_MODEL), lambda l, j: (0, 0)),
            pl.BlockSpec((1, D_MODEL, tn), lambda l, j: (l, 0, j)),
            pl.BlockSpec((1, 1, tn), lambda l, j: (l, 0, j)),
        ],
        out_specs=pl.BlockSpec((1, rows, tn), lambda l, j: (l, 0, j)),
        out_shape=jax.ShapeDtypeStruct((depth, rows, n), F32),
        compiler_params=_params(("arbitrary", "arbitrary")),
        name="modulation",
    )(cstack, w_ada, b_ada.reshape(depth, 1, n))


def _log_forget(z, log_lb, log1m_lb):
    ls = jnp.minimum(z, 0.0) - jnp.log1p(jnp.exp(-jnp.abs(z)))
    b = log1m_lb + ls
    return jnp.maximum(log_lb, b) + jnp.log1p(jnp.exp(-jnp.abs(log_lb - b)))


def _inproj_kernel(rope, x_ref, mod_ref, nw_ref, w_ref, lbc_ref, *rest):
    if rope:
        cos_ref, sa_ref, sb_ref, qa_ref, ka_ref, va_ref, hg_ref, uf_ref = rest
    else:
        qa_ref, ka_ref, va_ref, hg_ref, uf_ref = rest
    x = x_ref[...]
    m = mod_ref[...]
    y = x * _rms_scale(x) * nw_ref[...]
    h = (y * (1.0 + m[1:2]) + m[0:1]).astype(BF16)

    def proj(c0, n):
        return _dot(h, w_ref[:, c0:c0 + n])

    for base, ref, scale in ((0, qa_ref, DH_A ** -0.5), (WIDTH_A, ka_ref, None)):
        t = proj(base, WIDTH_A)
        for hd in range(N_HEADS_A):
            th = t[:, hd * DV_A:(hd + 1) * DV_A]
            if rope:
                th = (th * cos_ref[...]
                      + pltpu.roll(th, V7X_LANES - 16, 1) * sa_ref[...]
                      + pltpu.roll(th, 16, 1) * sb_ref[...])
            if scale is not None:
                th = th * scale
            ref[:, hd * DV_A:(hd + 1) * DV_A] = th.astype(BF16)
    va_ref[...] = proj(2 * WIDTH_A, WIDTH_A).astype(BF16)
    c0 = 3 * WIDTH_A
    hg_ref[:, 0:WIDTH_B] = proj(c0, WIDTH_B)
    lbc = lbc_ref[...]
    for dirn in range(2):
        z = proj(c0 + WIDTH_B * (1 + dirn), WIDTH_B)
        hg_ref[:, WIDTH_B * (1 + dirn):WIDTH_B * (2 + dirn)] = _log_forget(
            z, lbc[2 * dirn:2 * dirn + 1], lbc[2 * dirn + 1:2 * dirn + 2])
    hg_ref[:, 3 * WIDTH_B:5 * WIDTH_B] = proj(c0 + 3 * WIDTH_B, 2 * WIDTH_B)
    uf_ref[...] = proj(c0 + 5 * WIDTH_B, WIDTH_C).astype(BF16)


def _inproj(x2d, mod4, nw, w_bf, lbc, rope_tabs, li, seg_len, mod_row, tm):
    rows = x2d.shape[0]
    tpb = seg_len // tm
    rope = rope_tabs is not None
    mod_map = (lambda i: (li, i // tpb, 0, 0)) if mod_row is None else (lambda i: (li, mod_row, 0, 0))
    in_specs = [
        pl.BlockSpec((tm, D_MODEL), lambda i: (i, 0)),
        pl.BlockSpec((None, None, 6, D_MODEL), mod_map),
        pl.BlockSpec((None, 1, D_MODEL), lambda i: (li, 0, 0)),
        _const_spec((None, D_MODEL, PROJ_WIDTH), lambda i: (li, 0, 0)),
        pl.BlockSpec((None, 4, WIDTH_B), lambda i: (li, 0, 0)),
    ]
    args = [x2d, mod4, nw, w_bf, lbc]
    if rope:
        in_specs += [pl.BlockSpec((tm, DV_A), lambda i: (i % tpb, 0))] * 3
        args += list(rope_tabs)
    out_shape = (
        jax.ShapeDtypeStruct((rows, WIDTH_A), BF16),
        jax.ShapeDtypeStruct((rows, WIDTH_A), BF16),
        jax.ShapeDtypeStruct((rows, WIDTH_A), BF16),
        jax.ShapeDtypeStruct((rows, 5 * WIDTH_B), F32),
        jax.ShapeDtypeStruct((rows, WIDTH_C), BF16),
    )
    out_specs = (
        pl.BlockSpec((tm, WIDTH_A), lambda i: (i, 0)),
        pl.BlockSpec((tm, WIDTH_A), lambda i: (i, 0)),
        pl.BlockSpec((tm, WIDTH_A), lambda i: (i, 0)),
        pl.BlockSpec((tm, 5 * WIDTH_B), lambda i: (i, 0)),
        pl.BlockSpec((tm, WIDTH_C), lambda i: (i, 0)),
    )
    return pl.pallas_call(
        functools.partial(_inproj_kernel, rope),
        grid=(rows // tm,),
        in_specs=in_specs,
        out_specs=out_specs,
        out_shape=out_shape,
        compiler_params=_params(("parallel",)),
        name="inproj_rope" if rope else "inproj",
    )(*args)


def _attn_kernel(nseg, tq, sub, out_scale, lam_ref, q_ref, *rest):
    k_refs = rest[:nseg]
    v_refs = rest[nseg:2 * nseg]
    w_ref, o_ref = rest[2 * nseg:]
    vexts = []
    for v_ref in v_refs:
        v = v_ref[...]
        vl = lax.broadcasted_iota(jnp.int32, v.shape, 1)
        vexts.append(jnp.concatenate([v, jnp.where(vl == 0, 1.0, 0.0).astype(BF16)], axis=1))
    lam = lam_ref[0, 0]
    wn = w_ref[...] * out_scale
    lane = lax.broadcasted_iota(jnp.int32, (sub, DV_A), 1)
    for r0 in range(0, tq, sub):
        q = q_ref[r0:r0 + sub, :].astype(F32)
        qs = jnp.concatenate([jnp.where(lane < DH_A, q, 0.0), jnp.where(lane >= DH_A, q, 0.0)],
                             axis=0).astype(BF16)
        scores = [_dot_nt(qs, k_ref[...]) for k_ref in k_refs]
        mx = jnp.max(scores[0], axis=-1, keepdims=True)
        for s in scores[1:]:
            mx = jnp.maximum(mx, jnp.max(s, axis=-1, keepdims=True))
        acc = None
        for s, vext in zip(scores, vexts):
            part = _dot(jnp.exp(s - mx).astype(BF16), vext)
            acc = part if acc is None else acc + part
        r = acc[:, :DV_A] / acc[:, DV_A:DV_A + 1]
        o = r[:sub] - lam * r[sub:]
        o_ref[r0:r0 + sub, :] = (o * _rms_scale(o) * wn).astype(BF16)


def _attention(lam, q, ks, vs, subln_w, li, out_scale, tq, sub):
    b, lq, _ = q.shape
    nseg = len(ks)
    in_specs = [
        pl.BlockSpec(memory_space=pltpu.SMEM),
        pl.BlockSpec((None, tq, DV_A), lambda bi, h, i: (bi, i, h)),
    ]
    for arr in list(ks) + list(vs):
        in_specs.append(pl.BlockSpec((None, arr.shape[1], DV_A), lambda bi, h, i: (bi, 0, h)))
    in_specs.append(pl.BlockSpec((None, 1, DV_A), lambda bi, h, i: (li, 0, 0)))
    return pl.pallas_call(
        functools.partial(_attn_kernel, nseg, tq, sub, out_scale),
        grid=(b, N_HEADS_A, lq // tq),
        in_specs=in_specs,
        out_specs=pl.BlockSpec((None, tq, DV_A), lambda bi, h, i: (bi, i, h)),
        out_shape=jax.ShapeDtypeStruct((b, lq, WIDTH_A), BF16),
        compiler_params=_params(("parallel", "parallel", "arbitrary")),
        name="diff_attention_%dseg" % nseg,
    )(lam, q, *ks, *vs, subln_w)


def _hgrn_constants():
    c = HG_CHUNK
    t = np.arange(c)[:, None]
    r = np.arange(c)[None, :]
    blocks = []
    amask = []
    lomask = []
    for w in HG_WIDTHS:
        mid = (t // (2 * w)) * 2 * w + w
        hi = (t % (2 * w)) >= w
        eq = np.zeros((c, 2 * c), np.float32)
        eq[:, :c] = hi & (r >= mid) & (r <= t)
        eq[:, c:] = (~hi) & (r >= t) & (r <= mid - 1)
        ek = np.zeros((c, 2 * c), np.float32)
        ek[:, :c] = (~hi) & (r > t) & (r <= mid - 1)
        ek[:, c:] = hi & (r >= mid) & (r < t)
        blocks += [eq, ek]
        same_pair = (t // (2 * w)) == (r // (2 * w))
        amask.append(same_pair & (hi != ((r % (2 * w)) >= w)))
        lomask.append(np.broadcast_to(~hi, (c, c)))
    qf = np.zeros((c, 2 * c), np.float32); qf[:, :c] = r <= t
    qb = np.zeros((c, 2 * c), np.float32); qb[:, c:] = r >= t
    blocks += [qf, qb]
    pmain = np.concatenate(blocks, axis=0)
    ppre = np.zeros((2 * c + 16, 2 * c), np.float32)
    ppre[:c, :c] = r > t
    ppre[c:2 * c, c:] = r < t
    ppre[2 * c:2 * c + 8, :c] = 1.0
    ppre[2 * c + 8:, c:] = 1.0
    tile = lambda a: np.tile(np.asarray(a, np.float32), (1, N_HEADS_B))
    amask = np.stack([tile(a) for a in amask])
    lomask = np.stack([tile(a) for a in lomask])
    hd = np.arange(WIDTH_B) // DK_B
    bd = (hd[:, None] == hd[None, :]).astype(np.float32)
    return (jnp.asarray(pmain, BF16), jnp.asarray(ppre, BF16), jnp.asarray(amask, F32),
            jnp.asarray(lomask, F32), jnp.asarray(bd, F32))


def _hgrn_kernel(nc, q_ref, gf_ref, gb_ref, v_ref, og_ref, pmain_ref, ppre_ref, amask_ref, lomask_ref,
                 bd_ref, hw_ref, sf0_ref, sb0_ref, rec_ref, sf_ref, sb_ref, sfp_ref, sbn_ref):
    c = HG_CHUNK
    bd = bd_ref[...]
    bd_bf = bd.astype(BF16)
    lane_head = lax.broadcasted_iota(jnp.int32, (c, WIDTH_B), 1) // DK_B

    def stack_heads(a):
        return jnp.concatenate([jnp.where(lane_head == h, a, 0.0).astype(BF16) for h in range(N_HEADS_B)],
                               axis=0)

    def rows(ref, n):
        return ref[pl.ds(pl.multiple_of(n * c, c), c), :]

    sf_ref[...] = sf0_ref[...]
    sb_ref[...] = sb0_ref[...]

    def pre_body(i, carry):
        nf = i
        nb = nc - 1 - i
        gf = rows(gf_ref, nf)
        gb = rows(gb_ref, nb)
        g = jnp.concatenate([gf, gb], axis=0)
        e = jnp.exp(sum(_dot(ppre_ref[...], part) for part in _split2(g)))
        ktf = ((1.0 - jnp.exp(gf)) * e[0:c]).astype(BF16)
        ktb = ((1.0 - jnp.exp(gb)) * e[c:2 * c]).astype(BF16)
        stf = sf_ref[...]
        sfp_ref[nf] = stf.astype(BF16)
        sf_ref[...] = stf * e[2 * c:2 * c + 1] + _dot_tn(rows(v_ref, nf).astype(BF16), ktf) * bd
        stb = sb_ref[...]
        sbn_ref[nb] = stb.astype(BF16)
        sb_ref[...] = stb * e[2 * c + 8:2 * c + 9] + _dot_tn(rows(v_ref, nb).astype(BF16), ktb) * bd
        return carry

    lax.fori_loop(0, nc, pre_body, 0)

    def chunk(n):
        q = rows(q_ref, n)
        gf = rows(gf_ref, n)
        gb = rows(gb_ref, n)
        v = rows(v_ref, n)
        g = jnp.concatenate([gf, gb], axis=0)
        e = jnp.exp(sum(_dot(pmain_ref[...], part) for part in _split2(g)))
        kf = 1.0 - jnp.exp(gf)
        kb = 1.0 - jnp.exp(gb)
        a_all = jnp.zeros((c, N_HEADS_B * c), F32)
        for lvl in range(HG_LEVELS):
            eq = e[(2 * lvl) * c:(2 * lvl + 1) * c]
            ek = e[(2 * lvl + 1) * c:(2 * lvl + 2) * c]
            klev = jnp.where(lomask_ref[lvl] > 0.5, kf, kb)
            qt = (q * eq).astype(BF16)
            a_all = a_all + _dot_nt(qt, stack_heads(klev * ek)) * amask_ref[lvl]
        o = _dot(a_all.astype(BF16), stack_heads(v))
        o = o + _dot((q * (kf + kb)).astype(BF16), bd_bf) * v
        base = 2 * HG_LEVELS * c
        o = o + _dot_nt((q * e[base:base + c]).astype(BF16), sfp_ref[n])
        o = o + _dot_nt((q * e[base + c:base + 2 * c]).astype(BF16), sbn_ref[n])
        sq_h, sq_l = _split2(o * o)
        ms = _dot(jnp.concatenate([sq_h, sq_l], axis=0), bd_bf)
        ms = (ms[:c] + ms[c:]) * (1.0 / DV_B)
        og = rows(og_ref, n)
        rec = o * lax.rsqrt(ms + EPS) * hw_ref[...] * (og * jax.nn.sigmoid(og))
        rec_ref[pl.ds(pl.multiple_of(n * c, c), c), :] = rec.astype(BF16)

    def body(i, carry):
        for u in range(HG_UNROLL):
            chunk(i * HG_UNROLL + u)
        return carry

    lax.fori_loop(0, nc // HG_UNROLL, body, 0)


def _hgrn(hg, hconst, hw, sf0, sb0, li):
    b, l, _ = hg.shape
    nc = l // HG_CHUNK
    pmain, ppre, amask, lomask, bd = hconst
    col = lambda j: pl.BlockSpec((None, l, WIDTH_B), lambda bi: (bi, 0, j))
    full = lambda a: _const_spec(a.shape, lambda bi: (0,) * a.ndim)
    st_spec = pl.BlockSpec((None, WIDTH_B, WIDTH_B), lambda bi: (bi, 0, 0))
    return pl.pallas_call(
        functools.partial(_hgrn_kernel, nc),
        grid=(b,),
        in_specs=[col(0), col(1), col(2), col(3), col(4), full(pmain), full(ppre), full(amask),
                  full(lomask), full(bd), pl.BlockSpec((None, 1, WIDTH_B), lambda bi: (li, 0, 0)),
                  st_spec, st_spec],
        out_specs=(pl.BlockSpec((None, l, WIDTH_B), lambda bi: (bi, 0, 0)), st_spec, st_spec),
        out_shape=(jax.ShapeDtypeStruct((b, l, WIDTH_B), BF16),
                   jax.ShapeDtypeStruct((b, WIDTH_B, WIDTH_B), F32),
                   jax.ShapeDtypeStruct((b, WIDTH_B, WIDTH_B), F32)),
        scratch_shapes=[pltpu.VMEM((nc, WIDTH_B, WIDTH_B), BF16), pltpu.VMEM((nc, WIDTH_B, WIDTH_B), BF16)],
        compiler_params=_params(("parallel",)),
        name="hgrn_l%d" % l,
    )(hg, hg, hg, hg, hg, pmain, ppre, amask, lomask, bd, hw, sf0, sb0)


def _dft_constants(l):
    k = np.arange(FN_GROUP_DIM)
    ang = 2.0 * np.pi * ((k[:, None] * k[None, :]) % FN_GROUP_DIM) / FN_GROUP_DIM
    eye = np.eye(FN_GROUPS)
    cs = np.concatenate([np.kron(eye, np.cos(ang)), np.kron(eye, np.sin(ang))], axis=1)
    n = np.arange(l)
    angl = 2.0 * np.pi * ((n[:, None] * n[None, :]) % l) / l
    return tuple(jnp.asarray(a, F32).astype(BF16) for a in (cs, np.cos(angl), np.sin(angl)))


def _fnet_kernel(scale, u_ref, cs_ref, cl_ref, sl_ref, wf_ref, o_ref):
    t = _dot(u_ref[...], cs_ref[...])
    uc = t[:, :WIDTH_C].astype(BF16)
    us = t[:, WIDTH_C:].astype(BF16)
    y = (_dot(cl_ref[...], uc) - _dot(sl_ref[...], us)) * scale
    o_ref[...] = _dot(y.astype(BF16), wf_ref[...]).astype(BF16)


def _fnet(uf, dft, wf_bf, li):
    b, l, _ = uf.shape
    cs, cl, sl = dft
    scale = 1.0 / math.sqrt(l * FN_GROUP_DIM)
    return pl.pallas_call(
        functools.partial(_fnet_kernel, scale),
        grid=(b,),
        in_specs=[
            pl.BlockSpec((None, l, WIDTH_C), lambda bi: (bi, 0, 0)),
            _const_spec(cs.shape, lambda bi: (0, 0)),
            _const_spec(cl.shape, lambda bi: (0, 0)),
            _const_spec(sl.shape, lambda bi: (0, 0)),
            _const_spec((None, WIDTH_C, WIDTH_C), lambda bi: (li, 0, 0)),
        ],
        out_specs=pl.BlockSpec((None, l, WIDTH_C), lambda bi: (bi, 0, 0)),
        out_shape=jax.ShapeDtypeStruct((b, l, WIDTH_C), BF16),
        compiler_params=_params(("parallel",)),
        name="fnet_l%d" % l,
    )(uf, cs, cl, sl, wf_bf)


def _outproj_kernel(att_ref, rec_ref, four_ref, x_ref, mod_ref, nw_ref, wo_ref, x1_ref, h2_ref):
    mix = jnp.concatenate([att_ref[...], rec_ref[...], four_ref[...]], axis=1)
    m = mod_ref[...]
    x1 = x_ref[...] + m[2:3] * _dot(mix, wo_ref[...])
    x1_ref[...] = x1
    y = x1 * _rms_scale(x1) * nw_ref[...]
    h2_ref[...] = (y * (1.0 + m[4:5]) + m[3:4]).astype(BF16)


def _outproj(att, rec, four, x2d, mod4, nw2, wo_bf, li, seg_len, mod_row, tm):
    rows = x2d.shape[0]
    tpb = seg_len // tm
    mod_map = (lambda i: (li, i // tpb, 0, 0)) if mod_row is None else (lambda i: (li, mod_row, 0, 0))
    row_spec = lambda w: pl.BlockSpec((tm, w), lambda i: (i, 0))
    return pl.pallas_call(
        _outproj_kernel,
        grid=(rows // tm,),
        in_specs=[row_spec(WIDTH_A), row_spec(WIDTH_B), row_spec(WIDTH_C), row_spec(D_MODEL),
                  pl.BlockSpec((None, None, 6, D_MODEL), mod_map),
                  pl.BlockSpec((None, 1, D_MODEL), lambda i: (li, 0, 0)),
                  _const_spec((None, D_MODEL, D_MODEL), lambda i: (li, 0, 0))],
        out_specs=(row_spec(D_MODEL), row_spec(D_MODEL)),
        out_shape=(jax.ShapeDtypeStruct((rows, D_MODEL), F32), jax.ShapeDtypeStruct((rows, D_MODEL), BF16)),
        compiler_params=_params(("parallel",)),
        name="outproj",
    )(att, rec, four, x2d, mod4, nw2, wo_bf)


def _ffn_kernel(tm, tpb, final, h_ref, hp_ref, hn_ref, x1_ref, mod_ref, wup_ref, cw_ref, cb_ref, wdn_ref,
                fw_ref, o_ref, a_ref):
    i = pl.program_id(0)
    t = i % tpb
    prev = (hp_ref[...].astype(F32) * jnp.where(t > 0, 1.0, 0.0)).astype(BF16)
    nxt = (hn_ref[...].astype(F32) * jnp.where(t < tpb - 1, 1.0, 0.0)).astype(BF16)
    hext = jnp.concatenate([prev, h_ref[...], nxt], axis=0)
    ext = tm + 2 * HALO

    def conv(u, c0):
        w = cw_ref[:, c0:c0 + FFN_COLS]
        up = pltpu.roll(u, 1, 0)
        dn = pltpu.roll(u, ext - 1, 0)
        r = up * w[0:1] + u * w[1:2] + dn * w[2:3] + cb_ref[:, c0:c0 + FFN_COLS]
        return r[HALO:HALO + tm]

    for j in range(D_FF // FFN_COLS):
        c0 = j * FFN_COLS
        gate = conv(_dot(hext, wup_ref[:, c0:c0 + FFN_COLS]), c0)
        val = conv(_dot(hext, wup_ref[:, D_FF + c0:D_FF + c0 + FFN_COLS]), D_FF + c0)
        a_ref[:, c0:c0 + FFN_COLS] = (gate * jax.nn.sigmoid(gate) * val).astype(BF16)
    m = mod_ref[...]
    x2 = x1_ref[...] + m[5:6] * _dot(a_ref[...], wdn_ref[...])
    if final:
        x2 = x2 * _rms_scale(x2) * fw_ref[...]
    o_ref[...] = x2


def _ffn(h2, x1, mod4, wup_bf, conv_w, conv_b, wdn_bf, final_w, li, seg_len, mod_row, tm, final):
    rows = x1.shape[0]
    tpb = seg_len // tm
    hb = tm // HALO
    nhb = rows // HALO
    mod_map = (lambda i: (li, i // tpb, 0, 0)) if mod_row is None else (lambda i: (li, mod_row, 0, 0))
    return pl.pallas_call(
        functools.partial(_ffn_kernel, tm, tpb, final),
        grid=(rows // tm,),
        in_specs=[
            pl.BlockSpec((tm, D_MODEL), lambda i: (i, 0)),
            pl.BlockSpec((HALO, D_MODEL), lambda i: (jnp.maximum(i * hb - 1, 0), 0)),
            pl.BlockSpec((HALO, D_MODEL), lambda i: (jnp.minimum((i + 1) * hb, nhb - 1), 0)),
            pl.BlockSpec((tm, D_MODEL), lambda i: (i, 0)),
            pl.BlockSpec((None, None, 6, D_MODEL), mod_map),
            _const_spec((None, D_MODEL, 2 * D_FF), lambda i: (li, 0, 0)),
            pl.BlockSpec((None, 3, 2 * D_FF), lambda i: (li, 0, 0)),
            pl.BlockSpec((None, 1, 2 * D_FF), lambda i: (li, 0, 0)),
            _const_spec((None, D_FF, D_MODEL), lambda i: (li, 0, 0)),
            pl.BlockSpec((1, D_MODEL), lambda i: (0, 0)),
        ],
        out_specs=pl.BlockSpec((tm, D_MODEL), lambda i: (i, 0)),
        out_shape=jax.ShapeDtypeStruct((rows, D_MODEL), F32),
        scratch_shapes=[pltpu.VMEM((tm, D_FF), BF16)],
        compiler_params=_params(("parallel",)),
        name="convffn_final" if final else "convffn",
    )(h2, h2, h2, x1, mod4, wup_bf, conv_w, conv_b, wdn_bf, final_w)


def _rope_tables(l):
    rows = l // GRID_W
    pos_r = np.repeat(np.arange(rows), GRID_W).astype(np.float32)
    pos_c = np.tile(np.arange(GRID_W), rows).astype(np.float32)
    half = DH_A // 2
    inv_freq = (ROPE_BASE ** (-np.arange(0, half, 2, dtype=np.float32) / half)).astype(np.float32)
    ang = np.concatenate([pos_r[:, None] * inv_freq] * 2 + [pos_c[:, None] * inv_freq] * 2, axis=-1)
    cos = np.cos(ang.astype(np.float32)).astype(np.float32)
    sin = np.sin(ang.astype(np.float32)).astype(np.float32)
    cos = np.tile(cos, (1, 2))
    sin = np.tile(sin, (1, 2))
    first = (np.arange(DV_A) % 32) < 16
    sin_a = np.where(first, -sin, 0.0).astype(np.float32)
    sin_b = np.where(first, 0.0, sin).astype(np.float32)
    return jnp.asarray(cos), jnp.asarray(sin_a), jnp.asarray(sin_b)


def kernel(x, c, ctx, c_ctx, w_ada, b_ada, norm1_w, norm2_w, w_in, lam_qk, subln_w, lb_param, hgrn_norm_w,
           w_fnet, w_out, w_up, conv_w, conv_b, w_down, final_norm_w):
    b, l, d = x.shape
    lc = ctx.shape[1]
    depth = w_ada.shape[0]
    assert d == D_MODEL and l % GRID_W == 0 and b <= 8

    lb_all = jnp.cumsum(jax.nn.softmax(lb_param.astype(F32), axis=0), axis=0)
    lb_all = lb_all - lb_all[0:1]
    lbc = jnp.stack([jnp.log(lb_all[:, 0]), jnp.log1p(-lb_all[:, 0]),
                     jnp.log(lb_all[:, 1]), jnp.log1p(-lb_all[:, 1])], axis=1)
    lq = lam_qk.astype(F32)
    lam_dyn = jnp.exp(jnp.sum(lq[:, 0] * lq[:, 1], axis=-1)) - jnp.exp(jnp.sum(lq[:, 2] * lq[:, 3], axis=-1))
    rope_tabs = _rope_tables(l)
    hconst = _hgrn_constants()
    dft_l = _dft_constants(l)
    dft_c = _dft_constants(lc)
    w_in_bf = w_in.astype(BF16)
    w_out_bf = w_out.astype(BF16)
    w_up_bf = w_up.astype(BF16)
    w_dn_bf = w_down.astype(BF16)
    w_fn_bf = w_fnet.astype(BF16)
    nw1 = norm1_w.reshape(depth, 1, d)
    nw2 = norm2_w.reshape(depth, 1, d)
    sub_w = subln_w.reshape(depth, 1, DV_A)
    hw = jnp.tile(hgrn_norm_w, (1, N_HEADS_B)).reshape(depth, 1, WIDTH_B)
    cb = conv_b.reshape(depth, 1, 2 * D_FF)
    fw = final_norm_w.reshape(1, d)

    cstack = jnp.concatenate([c, c_ctx[None, :], jnp.zeros((16 - b - 1, d), F32)], axis=0)
    mod4 = _modulation(cstack, w_ada, b_ada).reshape(depth, 16, 6, d)
    ctx_row = b

    zeros_state = jnp.zeros((b, WIDTH_B, WIDTH_B), F32)
    xl = x.reshape(b * l, d)
    xc = ctx.reshape(b * lc, d)
    tm_l = 512
    tm_c = 256
    for li in range(depth):
        last = li == depth - 1
        lam_init = 0.8 - 0.6 * math.exp(-0.3 * li)
        lam = (lam_dyn[li] + lam_init).reshape(1, 1)

        qa, ka, va, hg, uf = _inproj(xl, mod4, nw1, w_in_bf, lbc, rope_tabs, li, l, None, tm_l)
        qc, kc, vc, hgc, ufc = _inproj(xc, mod4, nw1, w_in_bf, lbc, None, li, lc, ctx_row, tm_c)
        r3 = lambda a, n: a.reshape(b, n, a.shape[-1])

        att = _attention(lam, r3(qa, l), [r3(kc, lc), r3(ka, l)], [r3(vc, lc), r3(va, l)], sub_w, li,
                         1.0 - lam_init, 512, 128)
        rec_c, sf, sb = _hgrn(r3(hgc, lc), hconst, hw, zeros_state, zeros_state, li)
        rec, _, _ = _hgrn(r3(hg, l), hconst, hw, sf, sb, li)
        four = _fnet(r3(uf, l), dft_l, w_fn_bf, li)

        x1, h2 = _outproj(att.reshape(b * l, -1), rec.reshape(b * l, -1), four.reshape(b * l, -1), xl, mod4,
                          nw2, w_out_bf, li, l, None, tm_l)
        xl = _ffn(h2, x1, mod4, w_up_bf, conv_w, cb, w_dn_bf, fw, li, l, None, tm_l, last)

        if not last:
            att_c = _attention(lam, r3(qc, lc), [r3(kc, lc)], [r3(vc, lc)], sub_w, li, 1.0 - lam_init, 256,
                               128)
            four_c = _fnet(r3(ufc, lc), dft_c, w_fn_bf, li)
            x1c, h2c = _outproj(att_c.reshape(b * lc, -1), rec_c.reshape(b * lc, -1),
                                four_c.reshape(b * lc, -1), xc, mod4, nw2, w_out_bf, li, lc, ctx_row, tm_c)
            xc = _ffn(h2c, x1c, mod4, w_up_bf, conv_w, cb, w_dn_bf, fw, li, lc, ctx_row, tm_c, False)
    return xl.reshape(b, l, d)
```

```python
import functools
import math

import numpy as np
import jax
import jax.numpy as jnp
from jax import lax
from jax.experimental import pallas as pl
from jax.experimental.pallas import tpu as pltpu

F32 = jnp.float32
BF16 = jnp.bfloat16

D_MODEL = 1024
GRID_W = 64
N_HEADS_A = 4
DH_A = 64
DV_A = 2 * DH_A
WIDTH_A = N_HEADS_A * DV_A
N_HEADS_B = 4
DK_B = 64
DV_B = 64
WIDTH_B = N_HEADS_B * DV_B
FN_GROUPS = 4
FN_GROUP_DIM = 64
WIDTH_C = FN_GROUPS * FN_GROUP_DIM
PROJ_WIDTH = 3 * WIDTH_A + 5 * WIDTH_B + WIDTH_C
D_FF = 2816
ROPE_BASE = 10000.0
EPS = 1e-6

V7X_LANES = 128
V7X_BF16_SUBLANES = 16
V7X_VMEM_BYTES = 64 * 1024 * 1024
VMEM_LIMIT = V7X_VMEM_BYTES - 8 * 1024 * 1024

HG_CHUNK = 64
HG_LEVELS = 6
HG_WIDTHS = tuple(HG_CHUNK >> (l + 1) for l in range(HG_LEVELS))
HG_PRE_UNROLL = 2
HG_UNROLL = 4
FFN_COLS = 256
HALO = V7X_BF16_SUBLANES


def _params(sem):
    return pltpu.CompilerParams(dimension_semantics=sem, vmem_limit_bytes=VMEM_LIMIT)


def _const_spec(shape, index_map):
    return pl.BlockSpec(shape, index_map, pipeline_mode=pl.Buffered(1))


def _dot(a, b):
    return jnp.dot(a, b, preferred_element_type=F32)


def _dot_nt(a, b):
    return lax.dot_general(a, b, (((1,), (1,)), ((), ())), preferred_element_type=F32)


def _dot_tn(a, b):
    return lax.dot_general(a, b, (((0,), (0,)), ((), ())), preferred_element_type=F32)


def _split3(x):
    h = x.astype(BF16)
    r = x - h.astype(F32)
    m = r.astype(BF16)
    return h, m, (r - m.astype(F32)).astype(BF16)


def _split2(x):
    h = x.astype(BF16)
    return h, (x - h.astype(F32)).astype(BF16)


def _rms_scale(x):
    return lax.rsqrt(jnp.mean(x * x, axis=-1, keepdims=True) + EPS)


def _mod_kernel(c_ref, w_ref, b_ref, o_ref):
    c = c_ref[...]
    a = (c * jax.nn.sigmoid(c)).astype(BF16)
    o_ref[0] = _dot(a, w_ref[0].astype(BF16)) + b_ref[0]


def _modulation(cstack, w_ada, b_ada):
    depth, _, n = w_ada.shape
    tn = 1536
    rows = cstack.shape[0]
    return pl.pallas_call(
        _mod_kernel,
        grid=(depth, n // tn),
        in_specs=[
            pl.BlockSpec((rows, D_MODEL), lambda l, j: (0, 0)),
            pl.BlockSpec((1, D_MODEL, tn), lambda l, j: (l, 0, j)),
            pl.BlockSpec((1, 1, tn), lambda l, j: (l, 0, j)),
        ],
        out_specs=pl.BlockSpec((1, rows, tn), lambda l, j: (l, 0, j)),
        out_shape=jax.ShapeDtypeStruct((depth, rows, n), F32),
        compiler_params=_params(("arbitrary", "arbitrary")),
        name="modulation",
    )(cstack, w_ada, b_ada.reshape(depth, 1, n))


def _log_forget(z, log_lb, log1m_lb):
    ls = jnp.minimum(z, 0.0) - jnp.log1p(jnp.exp(-jnp.abs(z)))
    b = log1m_lb + ls
    return jnp.maximum(log_lb, b) + jnp.log1p(jnp.exp(-jnp.abs(log_lb - b)))


def _inproj_kernel(rope, x_ref, mod_ref, nw_ref, w_ref, lbc_ref, *rest):
    if rope:
        cos_ref, sa_ref, sb_ref, qa_ref, ka_ref, va_ref, hg_ref, uf_ref = rest
    else:
        qa_ref, ka_ref, va_ref, hg_ref, uf_ref = rest
    x = x_ref[...]
    m = mod_ref[...]
    y = x * _rms_scale(x) * nw_ref[...]
    h = (y * (1.0 + m[1:2]) + m[0:1]).astype(BF16)

    def proj(c0, n):
        return _dot(h, w_ref[:, c0:c0 + n])

    for base, ref, scale in ((0, qa_ref, DH_A ** -0.5), (WIDTH_A, ka_ref, None)):
        t = proj(base, WIDTH_A)
        for hd in range(N_HEADS_A):
            th = t[:, hd * DV_A:(hd + 1) * DV_A]
            if rope:
                th = (th * cos_ref[...]
                      + pltpu.roll(th, V7X_LANES - 16, 1) * sa_ref[...]
                      + pltpu.roll(th, 16, 1) * sb_ref[...])
            if scale is not None:
                th = th * scale
            ref[:, hd * DV_A:(hd + 1) * DV_A] = th.astype(BF16)
    va_ref[...] = proj(2 * WIDTH_A, WIDTH_A).astype(BF16)
    c0 = 3 * WIDTH_A
    hg_ref[:, 0:WIDTH_B] = proj(c0, WIDTH_B)
    lbc = lbc_ref[...]
    for dirn in range(2):
        z = proj(c0 + WIDTH_B * (1 + dirn), WIDTH_B)
        hg_ref[:, WIDTH_B * (1 + dirn):WIDTH_B * (2 + dirn)] = _log_forget(
            z, lbc[2 * dirn:2 * dirn + 1], lbc[2 * dirn + 1:2 * dirn + 2])
    hg_ref[:, 3 * WIDTH_B:5 * WIDTH_B] = proj(c0 + 3 * WIDTH_B, 2 * WIDTH_B)
    uf_ref[...] = proj(c0 + 5 * WIDTH_B, WIDTH_C).astype(BF16)


def _inproj(x2d, mod4, nw, w_bf, lbc, rope_tabs, li, seg_len, mod_row, tm):
    rows = x2d.shape[0]
    tpb = seg_len // tm
    rope = rope_tabs is not None
    mod_map = (lambda i: (li, i // tpb, 0, 0)) if mod_row is None else (lambda i: (li, mod_row, 0, 0))
    in_specs = [
        pl.BlockSpec((tm, D_MODEL), lambda i: (i, 0)),
        pl.BlockSpec((None, None, 6, D_MODEL), mod_map),
        pl.BlockSpec((None, 1, D_MODEL), lambda i: (li, 0, 0)),
        _const_spec((None, D_MODEL, PROJ_WIDTH), lambda i: (li, 0, 0)),
        pl.BlockSpec((None, 4, WIDTH_B), lambda i: (li, 0, 0)),
    ]
    args = [x2d, mod4, nw, w_bf, lbc]
    if rope:
        in_specs += [pl.BlockSpec((tm, DV_A), lambda i: (i % tpb, 0))] * 3
        args += list(rope_tabs)
    out_shape = (
        jax.ShapeDtypeStruct((rows, WIDTH_A), BF16),
        jax.ShapeDtypeStruct((rows, WIDTH_A), BF16),
        jax.ShapeDtypeStruct((rows, WIDTH_A), BF16),
        jax.ShapeDtypeStruct((rows, 5 * WIDTH_B), F32),
        jax.ShapeDtypeStruct((rows, WIDTH_C), BF16),
    )
    out_specs = (
        pl.BlockSpec((tm, WIDTH_A), lambda i: (i, 0)),
        pl.BlockSpec((tm, WIDTH_A), lambda i: (i, 0)),
        pl.BlockSpec((tm, WIDTH_A), lambda i: (i, 0)),
        pl.BlockSpec((tm, 5 * WIDTH_B), lambda i: (i, 0)),
        pl.BlockSpec((tm, WIDTH_C), lambda i: (i, 0)),
    )
    return pl.pallas_call(
        functools.partial(_inproj_kernel, rope),
        grid=(rows // tm,),
        in_specs=in_specs,
        out_specs=out_specs,
        out_shape=out_shape,
        compiler_params=_params(("parallel",)),
        name="inproj_rope" if rope else "inproj",
    )(*args)


def _attn_kernel(nseg, tq, sub, out_scale, lam_ref, q_ref, *rest):
    k_refs = rest[:nseg]
    v_refs = rest[nseg:2 * nseg]
    w_ref, o_ref = rest[2 * nseg:]
    vexts = []
    for v_ref in v_refs:
        v = v_ref[...]
        vl = lax.broadcasted_iota(jnp.int32, v.shape, 1)
        vexts.append(jnp.concatenate([v, jnp.where(vl == 0, 1.0, 0.0).astype(BF16)], axis=1))
    lam = lam_ref[0, 0]
    wn = w_ref[...] * out_scale
    lane = lax.broadcasted_iota(jnp.int32, (sub, DV_A), 1)
    for r0 in range(0, tq, sub):
        q = q_ref[r0:r0 + sub, :].astype(F32)
        qs = jnp.concatenate([jnp.where(lane < DH_A, q, 0.0), jnp.where(lane >= DH_A, q, 0.0)],
                             axis=0).astype(BF16)
        scores = [_dot_nt(qs, k_ref[...]) for k_ref in k_refs]
        mx = jnp.max(scores[0], axis=-1, keepdims=True)
        for s in scores[1:]:
            mx = jnp.maximum(mx, jnp.max(s, axis=-1, keepdims=True))
        acc = None
        for s, vext in zip(scores, vexts):
            part = _dot(jnp.exp(s - mx).astype(BF16), vext)
            acc = part if acc is None else acc + part
        r = acc[:, :DV_A] / acc[:, DV_A:DV_A + 1]
        o = r[:sub] - lam * r[sub:]
        o_ref[r0:r0 + sub, :] = (o * _rms_scale(o) * wn).astype(BF16)


def _attention(lam, q, ks, vs, subln_w, li, out_scale, tq, sub):
    b, lq, _ = q.shape
    nseg = len(ks)
    in_specs = [
        pl.BlockSpec(memory_space=pltpu.SMEM),
        pl.BlockSpec((None, tq, DV_A), lambda bi, h, i: (bi, i, h)),
    ]
    for arr in list(ks) + list(vs):
        in_specs.append(pl.BlockSpec((None, arr.shape[1], DV_A), lambda bi, h, i: (bi, 0, h)))
    in_specs.append(pl.BlockSpec((None, 1, DV_A), lambda bi, h, i: (li, 0, 0)))
    return pl.pallas_call(
        functools.partial(_attn_kernel, nseg, tq, sub, out_scale),
        grid=(b, N_HEADS_A, lq // tq),
        in_specs=in_specs,
        out_specs=pl.BlockSpec((None, tq, DV_A), lambda bi, h, i: (bi, i, h)),
        out_shape=jax.ShapeDtypeStruct((b, lq, WIDTH_A), BF16),
        compiler_params=_params(("parallel", "parallel", "arbitrary")),
        name="diff_attention_%dseg" % nseg,
    )(lam, q, *ks, *vs, subln_w)


def _hgrn_constants():
    c = HG_CHUNK
    t = np.arange(c)[:, None]
    r = np.arange(c)[None, :]
    amask = []
    lomask = []
    for w in HG_WIDTHS:
        hi = (t % (2 * w)) >= w
        same_pair = (t // (2 * w)) == (r // (2 * w))
        amask.append(same_pair & (hi != ((r % (2 * w)) >= w)))
        lomask.append(np.broadcast_to(~hi, (c, c)))
    psum = np.zeros((2 * c, 2 * c), np.float32)
    psum[:c, :c] = r <= t
    psum[c:, c:] = r >= t
    tile = lambda a: np.tile(np.asarray(a, np.float32), (1, N_HEADS_B))
    amask = np.stack([tile(a) for a in amask])
    lomask = np.stack([tile(a) for a in lomask])
    hd = np.arange(WIDTH_B) // DK_B
    bd = (hd[:, None] == hd[None, :]).astype(np.float32)
    return (jnp.asarray(psum, BF16), jnp.asarray(amask, F32), jnp.asarray(lomask, F32), jnp.asarray(bd, F32))


def _hgrn_kernel(nc, q_ref, gf_ref, gb_ref, v_ref, og_ref, psum_ref, amask_ref, lomask_ref,
                 bd_ref, hw_ref, sf0_ref, sb0_ref, rec_ref, sf_ref, sb_ref, sfp_ref, sbn_ref, cf_ref, cbs_ref):
    c = HG_CHUNK
    bd = bd_ref[...]
    bd_bf = bd.astype(BF16)
    lane_head = lax.broadcasted_iota(jnp.int32, (c, WIDTH_B), 1) // DK_B

    def stack_heads(a):
        return jnp.concatenate([jnp.where(lane_head == h, a, 0.0).astype(BF16) for h in range(N_HEADS_B)],
                               axis=0)

    def rows(ref, n):
        return ref[pl.ds(pl.multiple_of(n * c, c), c), :]

    sf_ref[...] = sf0_ref[...]
    sb_ref[...] = sb0_ref[...]

    def state_update(i):
        nf = i
        nb = nc - 1 - i
        gf = rows(gf_ref, nf)
        gb = rows(gb_ref, nb)
        g = jnp.concatenate([gf, gb], axis=0)
        sums = sum(_dot(psum_ref[...], part) for part in _split3(g))
        cf = sums[:c]
        cbs = sums[c:]
        cf_ref[nf] = cf
        cbs_ref[nb] = cbs
        tot_f = cf[c - 1:c]
        tot_b = cbs[0:1]
        ktf = ((1.0 - jnp.exp(gf)) * jnp.exp(tot_f - cf)).astype(BF16)
        ktb = ((1.0 - jnp.exp(gb)) * jnp.exp(tot_b - cbs)).astype(BF16)
        upd_f = _dot_tn(rows(v_ref, nf).astype(BF16), ktf) * bd
        upd_b = _dot_tn(rows(v_ref, nb).astype(BF16), ktb) * bd
        return nf, nb, jnp.exp(tot_f), upd_f, jnp.exp(tot_b), upd_b

    def pre_body(j, carry):
        steps = [state_update(j * HG_PRE_UNROLL + u) for u in range(HG_PRE_UNROLL)]
        for nf, nb, dec_f, upd_f, dec_b, upd_b in steps:
            stf = sf_ref[...]
            sfp_ref[nf] = stf.astype(BF16)
            sf_ref[...] = stf * dec_f + upd_f
            stb = sb_ref[...]
            sbn_ref[nb] = stb.astype(BF16)
            sb_ref[...] = stb * dec_b + upd_b
        return carry

    lax.fori_loop(0, nc // HG_PRE_UNROLL, pre_body, 0)

    def chunk(n):
        q = rows(q_ref, n)
        gf = rows(gf_ref, n)
        gb = rows(gb_ref, n)
        v = rows(v_ref, n)
        cf = cf_ref[n]
        cbs = cbs_ref[n]
        kf = 1.0 - jnp.exp(gf)
        kb = 1.0 - jnp.exp(gb)
        a_all = jnp.zeros((c, N_HEADS_B * c), F32)
        for lvl, w in enumerate(HG_WIDTHS):
            lo = lomask_ref[lvl] > 0.5
            if 2 * w >= 8:
                anchor = lambda a, r: jnp.broadcast_to(
                    a.reshape(c // (2 * w), 2 * w, WIDTH_B)[:, r:r + 1, :],
                    (c // (2 * w), 2 * w, WIDTH_B)).reshape(c, WIDTH_B)
                d_f = cf - anchor(cf, w - 1)
                d_b = cbs - anchor(cbs, w)
                eq = jnp.exp(jnp.where(lo, d_b, d_f))
                ek = jnp.exp(-jnp.where(lo, d_f, d_b))
            elif w == 2:
                r4 = lax.broadcasted_iota(jnp.int32, (c, WIDTH_B), 0) & 3
                gf_up, gf_dn = pltpu.roll(gf, c - 1, 0), pltpu.roll(gf, 1, 0)
                gb_up, gb_dn = pltpu.roll(gb, c - 1, 0), pltpu.roll(gb, 1, 0)
                eq = jnp.exp(jnp.where(r4 == 0, gb + gb_up, jnp.where(r4 == 1, gb,
                             jnp.where(r4 == 2, gf, gf + gf_dn))))
                ek = jnp.exp(jnp.where(r4 == 0, gf_up, jnp.where(r4 == 3, gb_dn, 0.0)))
            else:
                eq = jnp.exp(jnp.where(lo, gb, gf))
                ek = None
            klev = jnp.where(lo, kf, kb)
            qt = (q * eq).astype(BF16)
            kt = klev if ek is None else klev * ek
            a_all = a_all + _dot_nt(qt, stack_heads(kt)) * amask_ref[lvl]
        o = _dot(a_all.astype(BF16), stack_heads(v))
        o = o + _dot((q * (kf + kb)).astype(BF16), bd_bf) * v
        o = o + _dot_nt((q * jnp.exp(cf)).astype(BF16), sfp_ref[n])
        o = o + _dot_nt((q * jnp.exp(cbs)).astype(BF16), sbn_ref[n])
        sq_h, sq_l = _split2(o * o)
        ms = _dot(jnp.concatenate([sq_h, sq_l], axis=0), bd_bf)
        ms = (ms[:c] + ms[c:]) * (1.0 / DV_B)
        og = rows(og_ref, n)
        rec = o * lax.rsqrt(ms + EPS) * hw_ref[...] * (og * jax.nn.sigmoid(og))
        rec_ref[pl.ds(pl.multiple_of(n * c, c), c), :] = rec.astype(BF16)

    def body(i, carry):
        for u in range(HG_UNROLL):
            chunk(i * HG_UNROLL + u)
        return carry

    lax.fori_loop(0, nc // HG_UNROLL, body, 0)


def _hgrn(hg, hconst, hw, sf0, sb0, li):
    b, l, _ = hg.shape
    nc = l // HG_CHUNK
    psum, amask, lomask, bd = hconst
    col = lambda j: pl.BlockSpec((None, l, WIDTH_B), lambda bi: (bi, 0, j))
    full = lambda a: _const_spec(a.shape, lambda bi: (0,) * a.ndim)
    st_spec = pl.BlockSpec((None, WIDTH_B, WIDTH_B), lambda bi: (bi, 0, 0))
    state_scratch = pltpu.VMEM((nc, WIDTH_B, WIDTH_B), BF16)
    sums_scratch = pltpu.VMEM((nc, HG_CHUNK, WIDTH_B), F32)
    return pl.pallas_call(
        functools.partial(_hgrn_kernel, nc),
        grid=(b,),
        in_specs=[col(0), col(1), col(2), col(3), col(4), full(psum), full(amask),
                  full(lomask), full(bd), pl.BlockSpec((None, 1, WIDTH_B), lambda bi: (li, 0, 0)),
                  st_spec, st_spec],
        out_specs=(pl.BlockSpec((None, l, WIDTH_B), lambda bi: (bi, 0, 0)), st_spec, st_spec),
        out_shape=(jax.ShapeDtypeStruct((b, l, WIDTH_B), BF16),
                   jax.ShapeDtypeStruct((b, WIDTH_B, WIDTH_B), F32),
                   jax.ShapeDtypeStruct((b, WIDTH_B, WIDTH_B), F32)),
        scratch_shapes=[state_scratch, state_scratch, sums_scratch, sums_scratch],
        compiler_params=_params(("parallel",)),
        name="hgrn_l%d" % l,
    )(hg, hg, hg, hg, hg, psum, amask, lomask, bd, hw, sf0, sb0)


def _dft_constants(l):
    k = np.arange(FN_GROUP_DIM)
    ang = 2.0 * np.pi * ((k[:, None] * k[None, :]) % FN_GROUP_DIM) / FN_GROUP_DIM
    eye = np.eye(FN_GROUPS)
    cs = np.concatenate([np.kron(eye, np.cos(ang)), np.kron(eye, np.sin(ang))], axis=1)
    n = np.arange(l)
    angl = 2.0 * np.pi * ((n[:, None] * n[None, :]) % l) / l
    return tuple(jnp.asarray(a, F32).astype(BF16) for a in (cs, np.cos(angl), np.sin(angl)))


def _fnet_kernel(scale, u_ref, cs_ref, cl_ref, sl_ref, wf_ref, o_ref):
    t = _dot(u_ref[...], cs_ref[...])
    uc = t[:, :WIDTH_C].astype(BF16)
    us = t[:, WIDTH_C:].astype(BF16)
    y = (_dot(cl_ref[...], uc) - _dot(sl_ref[...], us)) * scale
    o_ref[...] = _dot(y.astype(BF16), wf_ref[...]).astype(BF16)


def _fnet(uf, dft, wf_bf, li):
    b, l, _ = uf.shape
    cs, cl, sl = dft
    scale = 1.0 / math.sqrt(l * FN_GROUP_DIM)
    return pl.pallas_call(
        functools.partial(_fnet_kernel, scale),
        grid=(b,),
        in_specs=[
            pl.BlockSpec((None, l, WIDTH_C), lambda bi: (bi, 0, 0)),
            _const_spec(cs.shape, lambda bi: (0, 0)),
            _const_spec(cl.shape, lambda bi: (0, 0)),
            _const_spec(sl.shape, lambda bi: (0, 0)),
            _const_spec((None, WIDTH_C, WIDTH_C), lambda bi: (li, 0, 0)),
        ],
        out_specs=pl.BlockSpec((None, l, WIDTH_C), lambda bi: (bi, 0, 0)),
        out_shape=jax.ShapeDtypeStruct((b, l, WIDTH_C), BF16),
        compiler_params=_params(("parallel",)),
        name="fnet_l%d" % l,
    )(uf, cs, cl, sl, wf_bf)


def _outproj_kernel(att_ref, rec_ref, four_ref, x_ref, mod_ref, nw_ref, wo_ref, x1_ref, h2_ref):
    mix = jnp.concatenate([att_ref[...], rec_ref[...], four_ref[...]], axis=1)
    m = mod_ref[...]
    x1 = x_ref[...] + m[2:3] * _dot(mix, wo_ref[...])
    x1_ref[...] = x1
    y = x1 * _rms_scale(x1) * nw_ref[...]
    h2_ref[...] = (y * (1.0 + m[4:5]) + m[3:4]).astype(BF16)


def _outproj(att, rec, four, x2d, mod4, nw2, wo_bf, li, seg_len, mod_row, tm):
    rows = x2d.shape[0]
    tpb = seg_len // tm
    mod_map = (lambda i: (li, i // tpb, 0, 0)) if mod_row is None else (lambda i: (li, mod_row, 0, 0))
    row_spec = lambda w: pl.BlockSpec((tm, w), lambda i: (i, 0))
    return pl.pallas_call(
        _outproj_kernel,
        grid=(rows // tm,),
        in_specs=[row_spec(WIDTH_A), row_spec(WIDTH_B), row_spec(WIDTH_C), row_spec(D_MODEL),
                  pl.BlockSpec((None, None, 6, D_MODEL), mod_map),
                  pl.BlockSpec((None, 1, D_MODEL), lambda i: (li, 0, 0)),
                  _const_spec((None, D_MODEL, D_MODEL), lambda i: (li, 0, 0))],
        out_specs=(row_spec(D_MODEL), row_spec(D_MODEL)),
        out_shape=(jax.ShapeDtypeStruct((rows, D_MODEL), F32), jax.ShapeDtypeStruct((rows, D_MODEL), BF16)),
        compiler_params=_params(("parallel",)),
        name="outproj",
    )(att, rec, four, x2d, mod4, nw2, wo_bf)


def _ffn_kernel(tm, tpb, final, h_ref, hp_ref, hn_ref, x1_ref, mod_ref, wup_ref, cw_ref, cb_ref, wdn_ref,
                fw_ref, o_ref, a_ref):
    i = pl.program_id(0)
    t = i % tpb
    prev = (hp_ref[...].astype(F32) * jnp.where(t > 0, 1.0, 0.0)).astype(BF16)
    nxt = (hn_ref[...].astype(F32) * jnp.where(t < tpb - 1, 1.0, 0.0)).astype(BF16)
    hext = jnp.concatenate([prev, h_ref[...], nxt], axis=0)
    ext = tm + 2 * HALO

    def conv(u, c0):
        w = cw_ref[:, c0:c0 + FFN_COLS]
        up = pltpu.roll(u, 1, 0)
        dn = pltpu.roll(u, ext - 1, 0)
        r = up * w[0:1] + u * w[1:2] + dn * w[2:3] + cb_ref[:, c0:c0 + FFN_COLS]
        return r[HALO:HALO + tm]

    for j in range(D_FF // FFN_COLS):
        c0 = j * FFN_COLS
        gate = conv(_dot(hext, wup_ref[:, c0:c0 + FFN_COLS]), c0)
        val = conv(_dot(hext, wup_ref[:, D_FF + c0:D_FF + c0 + FFN_COLS]), D_FF + c0)
        a_ref[:, c0:c0 + FFN_COLS] = (gate * jax.nn.sigmoid(gate) * val).astype(BF16)
    m = mod_ref[...]
    x2 = x1_ref[...] + m[5:6] * _dot(a_ref[...], wdn_ref[...])
    if final:
        x2 = x2 * _rms_scale(x2) * fw_ref[...]
    o_ref[...] = x2


def _ffn(h2, x1, mod4, wup_bf, conv_w, conv_b, wdn_bf, final_w, li, seg_len, mod_row, tm, final):
    rows = x1.shape[0]
    tpb = seg_len // tm
    hb = tm // HALO
    nhb = rows // HALO
    mod_map = (lambda i: (li, i // tpb, 0, 0)) if mod_row is None else (lambda i: (li, mod_row, 0, 0))
    return pl.pallas_call(
        functools.partial(_ffn_kernel, tm, tpb, final),
        grid=(rows // tm,),
        in_specs=[
            pl.BlockSpec((tm, D_MODEL), lambda i: (i, 0)),
            pl.BlockSpec((HALO, D_MODEL), lambda i: (jnp.maximum(i * hb - 1, 0), 0)),
            pl.BlockSpec((HALO, D_MODEL), lambda i: (jnp.minimum((i + 1) * hb, nhb - 1), 0)),
            pl.BlockSpec((tm, D_MODEL), lambda i: (i, 0)),
            pl.BlockSpec((None, None, 6, D_MODEL), mod_map),
            _const_spec((None, D_MODEL, 2 * D_FF), lambda i: (li, 0, 0)),
            pl.BlockSpec((None, 3, 2 * D_FF), lambda i: (li, 0, 0)),
            pl.BlockSpec((None, 1, 2 * D_FF), lambda i: (li, 0, 0)),
            _const_spec((None, D_FF, D_MODEL), lambda i: (li, 0, 0)),
            pl.BlockSpec((1, D_MODEL), lambda i: (0, 0)),
        ],
        out_specs=pl.BlockSpec((tm, D_MODEL), lambda i: (i, 0)),
        out_shape=jax.ShapeDtypeStruct((rows, D_MODEL), F32),
        scratch_shapes=[pltpu.VMEM((tm, D_FF), BF16)],
        compiler_params=_params(("parallel",)),
        name="convffn_final" if final else "convffn",
    )(h2, h2, h2, x1, mod4, wup_bf, conv_w, conv_b, wdn_bf, final_w)


def _rope_tables(l):
    rows = l // GRID_W
    pos_r = np.repeat(np.arange(rows), GRID_W).astype(np.float32)
    pos_c = np.tile(np.arange(GRID_W), rows).astype(np.float32)
    half = DH_A // 2
    inv_freq = (ROPE_BASE ** (-np.arange(0, half, 2, dtype=np.float32) / half)).astype(np.float32)
    ang = np.concatenate([pos_r[:, None] * inv_freq] * 2 + [pos_c[:, None] * inv_freq] * 2, axis=-1)
    cos = np.cos(ang.astype(np.float32)).astype(np.float32)
    sin = np.sin(ang.astype(np.float32)).astype(np.float32)
    cos = np.tile(cos, (1, 2))
    sin = np.tile(sin, (1, 2))
    first = (np.arange(DV_A) % 32) < 16
    sin_a = np.where(first, -sin, 0.0).astype(np.float32)
    sin_b = np.where(first, 0.0, sin).astype(np.float32)
    return jnp.asarray(cos), jnp.asarray(sin_a), jnp.asarray(sin_b)


def kernel(x, c, ctx, c_ctx, w_ada, b_ada, norm1_w, norm2_w, w_in, lam_qk, subln_w, lb_param, hgrn_norm_w,
           w_fnet, w_out, w_up, conv_w, conv_b, w_down, final_norm_w):
    b, l, d = x.shape
    lc = ctx.shape[1]
    depth = w_ada.shape[0]
    assert d == D_MODEL and l % GRID_W == 0 and b <= 8

    lb_all = jnp.cumsum(jax.nn.softmax(lb_param.astype(F32), axis=0), axis=0)
    lb_all = lb_all - lb_all[0:1]
    lbc = jnp.stack([jnp.log(lb_all[:, 0]), jnp.log1p(-lb_all[:, 0]),
                     jnp.log(lb_all[:, 1]), jnp.log1p(-lb_all[:, 1])], axis=1)
    lq = lam_qk.astype(F32)
    lam_dyn = jnp.exp(jnp.sum(lq[:, 0] * lq[:, 1], axis=-1)) - jnp.exp(jnp.sum(lq[:, 2] * lq[:, 3], axis=-1))
    rope_tabs = _rope_tables(l)
    hconst = _hgrn_constants()
    dft_l = _dft_constants(l)
    dft_c = _dft_constants(lc)
    w_in_bf = w_in.astype(BF16)
    w_out_bf = w_out.astype(BF16)
    w_up_bf = w_up.astype(BF16)
    w_dn_bf = w_down.astype(BF16)
    w_fn_bf = w_fnet.astype(BF16)
    nw1 = norm1_w.reshape(depth, 1, d)
    nw2 = norm2_w.reshape(depth, 1, d)
    sub_w = subln_w.reshape(depth, 1, DV_A)
    hw = jnp.tile(hgrn_norm_w, (1, N_HEADS_B)).reshape(depth, 1, WIDTH_B)
    cb = conv_b.reshape(depth, 1, 2 * D_FF)
    fw = final_norm_w.reshape(1, d)

    cstack = jnp.concatenate([c, c_ctx[None, :], jnp.zeros((16 - b - 1, d), F32)], axis=0)
    mod4 = _modulation(cstack, w_ada, b_ada).reshape(depth, 16, 6, d)
    ctx_row = b

    zeros_state = jnp.zeros((b, WIDTH_B, WIDTH_B), F32)
    xl = x.reshape(b * l, d)
    xc = ctx.reshape(b * lc, d)
    tm_l = 512
    tm_c = 256
    for li in range(depth):
        last = li == depth - 1
        lam_init = 0.8 - 0.6 * math.exp(-0.3 * li)
        lam = (lam_dyn[li] + lam_init).reshape(1, 1)

        qa, ka, va, hg, uf = _inproj(xl, mod4, nw1, w_in_bf, lbc, rope_tabs, li, l, None, tm_l)
        qc, kc, vc, hgc, ufc = _inproj(xc, mod4, nw1, w_in_bf, lbc, None, li, lc, ctx_row, tm_c)
        r3 = lambda a, n: a.reshape(b, n, a.shape[-1])

        att = _attention(lam, r3(qa, l), [r3(kc, lc), r3(ka, l)], [r3(vc, lc), r3(va, l)], sub_w, li,
                         1.0 - lam_init, 512, 128)
        rec_c, sf, sb = _hgrn(r3(hgc, lc), hconst, hw, zeros_state, zeros_state, li)
        rec, _, _ = _hgrn(r3(hg, l), hconst, hw, sf, sb, li)
        four = _fnet(r3(uf, l), dft_l, w_fn_bf, li)

        x1, h2 = _outproj(att.reshape(b * l, -1), rec.reshape(b * l, -1), four.reshape(b * l, -1), xl, mod4,
                          nw2, w_out_bf, li, l, None, tm_l)
        xl = _ffn(h2, x1, mod4, w_up_bf, conv_w, cb, w_dn_bf, fw, li, l, None, tm_l, last)

        if not last:
            att_c = _attention(lam, r3(qc, lc), [r3(kc, lc)], [r3(vc, lc)], sub_w, li, 1.0 - lam_init, 256,
                               128)
            four_c = _fnet(r3(ufc, lc), dft_c, w_fn_bf, li)
            x1c, h2c = _outproj(att_c.reshape(b * lc, -1), rec_c.reshape(b * lc, -1),
                                four_c.reshape(b * lc, -1), xc, mod4, nw2, w_out_bf, li, lc, ctx_row, tm_c)
            xc = _ffn(h2c, x1c, mod4, w_up_bf, conv_w, cb, w_dn_bf, fw, li, lc, ctx_row, tm_c, False)
    return xl.reshape(b, l, d)
```

```python
import functools
import math

import numpy as np
import jax
import jax.numpy as jnp
from jax import lax
from jax.experimental import pallas as pl
from jax.experimental.pallas import tpu as pltpu

F32 = jnp.float32
BF16 = jnp.bfloat16

D_MODEL = 1024
GRID_W = 64
N_HEADS_A = 4
DH_A = 64
DV_A = 2 * DH_A
WIDTH_A = N_HEADS_A * DV_A
N_HEADS_B = 4
DK_B = 64
DV_B = 64
WIDTH_B = N_HEADS_B * DV_B
FN_GROUPS = 4
FN_GROUP_DIM = 64
WIDTH_C = FN_GROUPS * FN_GROUP_DIM
PROJ_WIDTH = 3 * WIDTH_A + 5 * WIDTH_B + WIDTH_C
D_FF = 2816
ROPE_BASE = 10000.0
EPS = 1e-6

V7X_LANES = 128
V7X_BF16_SUBLANES = 16
V7X_VMEM_BYTES = 64 * 1024 * 1024
VMEM_LIMIT = V7X_VMEM_BYTES - 8 * 1024 * 1024

HG_CHUNK = 64
HG_LEVELS = 6
HG_WIDTHS = tuple(HG_CHUNK >> (l + 1) for l in range(HG_LEVELS))
HG_PRE_UNROLL = 2
HG_UNROLL = 4
FFN_COLS = 256
HALO = V7X_BF16_SUBLANES


def _params(sem):
    return pltpu.CompilerParams(dimension_semantics=sem, vmem_limit_bytes=VMEM_LIMIT)


def _const_spec(shape, index_map):
    return pl.BlockSpec(shape, index_map, pipeline_mode=pl.Buffered(1))


def _dot(a, b):
    return jnp.dot(a, b, preferred_element_type=F32)


def _dot_nt(a, b):
    return lax.dot_general(a, b, (((1,), (1,)), ((), ())), preferred_element_type=F32)


def _dot_tn(a, b):
    return lax.dot_general(a, b, (((0,), (0,)), ((), ())), preferred_element_type=F32)


def _split3(x):
    h = x.astype(BF16)
    r = x - h.astype(F32)
    m = r.astype(BF16)
    return h, m, (r - m.astype(F32)).astype(BF16)


def _split2(x):
    h = x.astype(BF16)
    return h, (x - h.astype(F32)).astype(BF16)


def _rms_scale(x):
    return lax.rsqrt(jnp.mean(x * x, axis=-1, keepdims=True) + EPS)


def _mod_kernel(c_ref, w_ref, b_ref, o_ref):
    c = c_ref[...]
    a = (c * jax.nn.sigmoid(c)).astype(BF16)
    o_ref[0] = _dot(a, w_ref[0].astype(BF16)) + b_ref[0]


def _modulation(cstack, w_ada, b_ada):
    depth, _, n = w_ada.shape
    tn = 1536
    rows = cstack.shape[0]
    return pl.pallas_call(
        _mod_kernel,
        grid=(depth, n // tn),
        in_specs=[
            pl.BlockSpec((rows, D_MODEL), lambda l, j: (0, 0)),
            pl.BlockSpec((1, D_MODEL, tn), lambda l, j: (l, 0, j)),
            pl.BlockSpec((1, 1, tn), lambda l, j: (l, 0, j)),
        ],
        out_specs=pl.BlockSpec((1, rows, tn), lambda l, j: (l, 0, j)),
        out_shape=jax.ShapeDtypeStruct((depth, rows, n), F32),
        compiler_params=_params(("arbitrary", "arbitrary")),
        name="modulation",
    )(cstack, w_ada, b_ada.reshape(depth, 1, n))


def _log_forget(z, log_lb, log1m_lb):
    ls = jnp.minimum(z, 0.0) - jnp.log1p(jnp.exp(-jnp.abs(z)))
    b = log1m_lb + ls
    return jnp.maximum(log_lb, b) + jnp.log1p(jnp.exp(-jnp.abs(log_lb - b)))


def _inproj_kernel(rope, x_ref, mod_ref, nw_ref, w_ref, lbc_ref, *rest):
    if rope:
        cos_ref, sa_ref, sb_ref, qa_ref, ka_ref, va_ref, hg_ref, uf_ref = rest
    else:
        qa_ref, ka_ref, va_ref, hg_ref, uf_ref = rest
    x = x_ref[...]
    m = mod_ref[...]
    y = x * _rms_scale(x) * nw_ref[...]
    h = (y * (1.0 + m[1:2]) + m[0:1]).astype(BF16)

    def proj(c0, n):
        return _dot(h, w_ref[:, c0:c0 + n])

    for base, ref, scale in ((0, qa_ref, DH_A ** -0.5 * math.log2(math.e)), (WIDTH_A, ka_ref, None)):
        t = proj(base, WIDTH_A)
        for hd in range(N_HEADS_A):
            th = t[:, hd * DV_A:(hd + 1) * DV_A]
            if rope:
                th = (th * cos_ref[...]
                      + pltpu.roll(th, V7X_LANES - 16, 1) * sa_ref[...]
                      + pltpu.roll(th, 16, 1) * sb_ref[...])
            if scale is not None:
                th = th * scale
            ref[:, hd * DV_A:(hd + 1) * DV_A] = th.astype(BF16)
    va_ref[...] = proj(2 * WIDTH_A, WIDTH_A).astype(BF16)
    c0 = 3 * WIDTH_A
    hg_ref[:, 0:WIDTH_B] = proj(c0, WIDTH_B)
    lbc = lbc_ref[...]
    for dirn in range(2):
        z = proj(c0 + WIDTH_B * (1 + dirn), WIDTH_B)
        hg_ref[:, WIDTH_B * (1 + dirn):WIDTH_B * (2 + dirn)] = _log_forget(
            z, lbc[2 * dirn:2 * dirn + 1], lbc[2 * dirn + 1:2 * dirn + 2])
    hg_ref[:, 3 * WIDTH_B:5 * WIDTH_B] = proj(c0 + 3 * WIDTH_B, 2 * WIDTH_B)
    uf_ref[...] = proj(c0 + 5 * WIDTH_B, WIDTH_C).astype(BF16)


def _inproj(x2d, mod4, nw, w_bf, lbc, rope_tabs, li, seg_len, mod_row, tm):
    rows = x2d.shape[0]
    tpb = seg_len // tm
    rope = rope_tabs is not None
    mod_map = (lambda i: (li, i // tpb, 0, 0)) if mod_row is None else (lambda i: (li, mod_row, 0, 0))
    in_specs = [
        pl.BlockSpec((tm, D_MODEL), lambda i: (i, 0)),
        pl.BlockSpec((None, None, 6, D_MODEL), mod_map),
        pl.BlockSpec((None, 1, D_MODEL), lambda i: (li, 0, 0)),
        _const_spec((None, D_MODEL, PROJ_WIDTH), lambda i: (li, 0, 0)),
        pl.BlockSpec((None, 4, WIDTH_B), lambda i: (li, 0, 0)),
    ]
    args = [x2d, mod4, nw, w_bf, lbc]
    if rope:
        in_specs += [pl.BlockSpec((tm, DV_A), lambda i: (i % tpb, 0))] * 3
        args += list(rope_tabs)
    out_shape = (
        jax.ShapeDtypeStruct((rows, WIDTH_A), BF16),
        jax.ShapeDtypeStruct((rows, WIDTH_A), BF16),
        jax.ShapeDtypeStruct((rows, WIDTH_A), BF16),
        jax.ShapeDtypeStruct((rows, 5 * WIDTH_B), F32),
        jax.ShapeDtypeStruct((rows, WIDTH_C), BF16),
    )
    out_specs = (
        pl.BlockSpec((tm, WIDTH_A), lambda i: (i, 0)),
        pl.BlockSpec((tm, WIDTH_A), lambda i: (i, 0)),
        pl.BlockSpec((tm, WIDTH_A), lambda i: (i, 0)),
        pl.BlockSpec((tm, 5 * WIDTH_B), lambda i: (i, 0)),
        pl.BlockSpec((tm, WIDTH_C), lambda i: (i, 0)),
    )
    return pl.pallas_call(
        functools.partial(_inproj_kernel, rope),
        grid=(rows // tm,),
        in_specs=in_specs,
        out_specs=out_specs,
        out_shape=out_shape,
        compiler_params=_params(("parallel",)),
        name="inproj_rope" if rope else "inproj",
    )(*args)


def _attn_kernel(nseg, tq, sub, out_scale, lam_ref, q_ref, *rest):
    k_refs = rest[:nseg]
    v_refs = rest[nseg:2 * nseg]
    w_ref, o_ref = rest[2 * nseg:]
    vexts = []
    for v_ref in v_refs:
        v = v_ref[...]
        vl = lax.broadcasted_iota(jnp.int32, v.shape, 1)
        vexts.append(jnp.concatenate([v, jnp.where(vl == 0, 1.0, 0.0).astype(BF16)], axis=1))
    lam = lam_ref[0, 0]
    wn = w_ref[...] * out_scale
    lane = lax.broadcasted_iota(jnp.int32, (sub, DV_A), 1)
    for r0 in range(0, tq, sub):
        q = q_ref[r0:r0 + sub, :].astype(F32)
        qs = jnp.concatenate([jnp.where(lane < DH_A, q, 0.0), jnp.where(lane >= DH_A, q, 0.0)],
                             axis=0).astype(BF16)
        scores = [_dot_nt(qs, k_ref[...]) for k_ref in k_refs]
        mx = jnp.max(scores[0], axis=-1, keepdims=True)
        for s in scores[1:]:
            mx = jnp.maximum(mx, jnp.max(s, axis=-1, keepdims=True))
        acc = sum(_dot(jnp.exp2(s - mx).astype(BF16), vext) for s, vext in zip(scores, vexts))
        r = acc[:, :DV_A] / acc[:, DV_A:DV_A + 1]
        o = r[:sub] - lam * r[sub:]
        o_ref[r0:r0 + sub, :] = (o * _rms_scale(o) * wn).astype(BF16)


def _attention(lam, q, ks, vs, splits, subln_w, li, out_scale, tq, sub):
    b, lq, _ = q.shape
    in_specs = [
        pl.BlockSpec(memory_space=pltpu.SMEM),
        pl.BlockSpec((None, tq, DV_A), lambda bi, h, i: (bi, i, h)),
    ]
    k_args, v_args, kv_specs = [], [], []
    for k_arr, v_arr, n in zip(ks, vs, splits):
        for part in range(n):
            k_args.append(k_arr)
            v_args.append(v_arr)
            kv_specs.append(pl.BlockSpec((None, k_arr.shape[1] // n, DV_A),
                                         lambda bi, h, i, part=part: (bi, part, h)))
    nseg = len(k_args)
    ks, vs = k_args, v_args
    in_specs += kv_specs + kv_specs
    in_specs.append(pl.BlockSpec((None, 1, DV_A), lambda bi, h, i: (li, 0, 0)))
    return pl.pallas_call(
        functools.partial(_attn_kernel, nseg, tq, sub, out_scale),
        grid=(b, N_HEADS_A, lq // tq),
        in_specs=in_specs,
        out_specs=pl.BlockSpec((None, tq, DV_A), lambda bi, h, i: (bi, i, h)),
        out_shape=jax.ShapeDtypeStruct((b, lq, WIDTH_A), BF16),
        compiler_params=_params(("parallel", "parallel", "arbitrary")),
        name="diff_attention_%dseg" % nseg,
    )(lam, q, *ks, *vs, subln_w)


def _hgrn_constants():
    c = HG_CHUNK
    t = np.arange(c)[:, None]
    r = np.arange(c)[None, :]
    amask = []
    lomask = []
    for w in HG_WIDTHS:
        hi = (t % (2 * w)) >= w
        same_pair = (t // (2 * w)) == (r // (2 * w))
        amask.append(same_pair & (hi != ((r % (2 * w)) >= w)))
        lomask.append(np.broadcast_to(~hi, (c, c)))
    psum = np.zeros((2 * c, 2 * c), np.float32)
    psum[:c, :c] = r <= t
    psum[c:, c:] = r >= t
    tile = lambda a: np.tile(np.asarray(a, np.float32), (1, N_HEADS_B))
    amask = np.stack([tile(a) for a in amask])
    lomask = np.stack([tile(a) for a in lomask])
    hd = np.arange(WIDTH_B) // DK_B
    bd = (hd[:, None] == hd[None, :]).astype(np.float32)
    return (jnp.asarray(psum, BF16), jnp.asarray(amask, F32), jnp.asarray(lomask, F32), jnp.asarray(bd, F32))


def _hgrn_kernel(nc, q_ref, gf_ref, gb_ref, v_ref, og_ref, psum_ref, amask_ref, lomask_ref,
                 bd_ref, hw_ref, sf0_ref, sb0_ref, rec_ref, sf_ref, sb_ref, sfp_ref, sbn_ref, cf_ref, cbs_ref):
    c = HG_CHUNK
    bd = bd_ref[...]
    bd_bf = bd.astype(BF16)
    lane_head = lax.broadcasted_iota(jnp.int32, (c, WIDTH_B), 1) // DK_B

    def stack_heads(a):
        return jnp.concatenate([jnp.where(lane_head == h, a, 0.0).astype(BF16) for h in range(N_HEADS_B)],
                               axis=0)

    def rows(ref, n):
        return ref[pl.ds(pl.multiple_of(n * c, c), c), :]

    sf_ref[...] = sf0_ref[...]
    sb_ref[...] = sb0_ref[...]

    def state_update(i):
        nf = i
        nb = nc - 1 - i
        gf = rows(gf_ref, nf)
        gb = rows(gb_ref, nb)
        g = jnp.concatenate([gf, gb], axis=0)
        sums = sum(_dot(psum_ref[...], part) for part in _split3(g))
        cf = sums[:c]
        cbs = sums[c:]
        cf_ref[nf] = cf
        cbs_ref[nb] = cbs
        tot_f = cf[c - 1:c]
        tot_b = cbs[0:1]
        ktf = ((1.0 - jnp.exp(gf)) * jnp.exp(tot_f - cf)).astype(BF16)
        ktb = ((1.0 - jnp.exp(gb)) * jnp.exp(tot_b - cbs)).astype(BF16)
        upd_f = _dot_tn(rows(v_ref, nf).astype(BF16), ktf) * bd
        upd_b = _dot_tn(rows(v_ref, nb).astype(BF16), ktb) * bd
        return nf, nb, jnp.exp(tot_f), upd_f, jnp.exp(tot_b), upd_b

    def pre_body(j, carry):
        steps = [state_update(j * HG_PRE_UNROLL + u) for u in range(HG_PRE_UNROLL)]
        for nf, nb, dec_f, upd_f, dec_b, upd_b in steps:
            stf = sf_ref[...]
            sfp_ref[nf] = stf.astype(BF16)
            sf_ref[...] = stf * dec_f + upd_f
            stb = sb_ref[...]
            sbn_ref[nb] = stb.astype(BF16)
            sb_ref[...] = stb * dec_b + upd_b
        return carry

    lax.fori_loop(0, nc // HG_PRE_UNROLL, pre_body, 0)

    def chunk(n):
        q = rows(q_ref, n)
        gf = rows(gf_ref, n)
        gb = rows(gb_ref, n)
        v = rows(v_ref, n)
        cf = cf_ref[n]
        cbs = cbs_ref[n]
        kf = 1.0 - jnp.exp(gf)
        kb = 1.0 - jnp.exp(gb)
        a_all = jnp.zeros((c, N_HEADS_B * c), F32)
        for lvl, w in enumerate(HG_WIDTHS):
            lo = lomask_ref[lvl] > 0.5
            if 2 * w >= 8:
                anchor = lambda a, r: jnp.broadcast_to(
                    a.reshape(c // (2 * w), 2 * w, WIDTH_B)[:, r:r + 1, :],
                    (c // (2 * w), 2 * w, WIDTH_B)).reshape(c, WIDTH_B)
                d_f = cf - anchor(cf, w - 1)
                d_b = cbs - anchor(cbs, w)
                eq = jnp.exp(jnp.where(lo, d_b, d_f))
                ek = jnp.exp(-jnp.where(lo, d_f, d_b))
            elif w == 2:
                r4 = lax.broadcasted_iota(jnp.int32, (c, WIDTH_B), 0) & 3
                gf_up, gf_dn = pltpu.roll(gf, c - 1, 0), pltpu.roll(gf, 1, 0)
                gb_up, gb_dn = pltpu.roll(gb, c - 1, 0), pltpu.roll(gb, 1, 0)
                eq = jnp.exp(jnp.where(r4 == 0, gb + gb_up, jnp.where(r4 == 1, gb,
                             jnp.where(r4 == 2, gf, gf + gf_dn))))
                ek = jnp.exp(jnp.where(r4 == 0, gf_up, jnp.where(r4 == 3, gb_dn, 0.0)))
            else:
                eq = jnp.exp(jnp.where(lo, gb, gf))
                ek = None
            klev = jnp.where(lo, kf, kb)
            qt = (q * eq).astype(BF16)
            kt = klev if ek is None else klev * ek
            a_all = a_all + _dot_nt(qt, stack_heads(kt)) * amask_ref[lvl]
        o = _dot(a_all.astype(BF16), stack_heads(v))
        o = o + _dot((q * (kf + kb)).astype(BF16), bd_bf) * v
        o = o + _dot_nt((q * jnp.exp(cf)).astype(BF16), sfp_ref[n])
        o = o + _dot_nt((q * jnp.exp(cbs)).astype(BF16), sbn_ref[n])
        sq_h, sq_l = _split2(o * o)
        ms = _dot(jnp.concatenate([sq_h, sq_l], axis=0), bd_bf)
        ms = (ms[:c] + ms[c:]) * (1.0 / DV_B)
        og = rows(og_ref, n)
        rec = o * lax.rsqrt(ms + EPS) * hw_ref[...] * (og * jax.nn.sigmoid(og))
        rec_ref[pl.ds(pl.multiple_of(n * c, c), c), :] = rec.astype(BF16)

    def body(i, carry):
        for u in range(HG_UNROLL):
            chunk(i * HG_UNROLL + u)
        return carry

    lax.fori_loop(0, nc // HG_UNROLL, body, 0)


def _hgrn(hg, hconst, hw, sf0, sb0, li):
    b, l, _ = hg.shape
    nc = l // HG_CHUNK
    psum, amask, lomask, bd = hconst
    col = lambda j: pl.BlockSpec((None, l, WIDTH_B), lambda bi: (bi, 0, j))
    full = lambda a: _const_spec(a.shape, lambda bi: (0,) * a.ndim)
    st_spec = pl.BlockSpec((None, WIDTH_B, WIDTH_B), lambda bi: (bi, 0, 0))
    state_scratch = pltpu.VMEM((nc, WIDTH_B, WIDTH_B), BF16)
    sums_scratch = pltpu.VMEM((nc, HG_CHUNK, WIDTH_B), F32)
    return pl.pallas_call(
        functools.partial(_hgrn_kernel, nc),
        grid=(b,),
        in_specs=[col(0), col(1), col(2), col(3), col(4), full(psum), full(amask),
                  full(lomask), full(bd), pl.BlockSpec((None, 1, WIDTH_B), lambda bi: (li, 0, 0)),
                  st_spec, st_spec],
        out_specs=(pl.BlockSpec((None, l, WIDTH_B), lambda bi: (bi, 0, 0)), st_spec, st_spec),
        out_shape=(jax.ShapeDtypeStruct((b, l, WIDTH_B), BF16),
                   jax.ShapeDtypeStruct((b, WIDTH_B, WIDTH_B), F32),
                   jax.ShapeDtypeStruct((b, WIDTH_B, WIDTH_B), F32)),
        scratch_shapes=[state_scratch, state_scratch, sums_scratch, sums_scratch],
        compiler_params=_params(("parallel",)),
        name="hgrn_l%d" % l,
    )(hg, hg, hg, hg, hg, psum, amask, lomask, bd, hw, sf0, sb0)


def _dft_constants(l):
    k = np.arange(FN_GROUP_DIM)
    ang = 2.0 * np.pi * ((k[:, None] * k[None, :]) % FN_GROUP_DIM) / FN_GROUP_DIM
    eye = np.eye(FN_GROUPS)
    cs = np.concatenate([np.kron(eye, np.cos(ang)), np.kron(eye, np.sin(ang))], axis=1)
    n = np.arange(l)
    angl = 2.0 * np.pi * ((n[:, None] * n[None, :]) % l) / l
    return tuple(jnp.asarray(a, F32).astype(BF16) for a in (cs, np.cos(angl), np.sin(angl)))


def _fnet_kernel(scale, u_ref, cs_ref, cl_ref, sl_ref, wf_ref, o_ref):
    t = _dot(u_ref[...], cs_ref[...])
    uc = t[:, :WIDTH_C].astype(BF16)
    us = t[:, WIDTH_C:].astype(BF16)
    y = (_dot(cl_ref[...], uc) - _dot(sl_ref[...], us)) * scale
    o_ref[...] = _dot(y.astype(BF16), wf_ref[...]).astype(BF16)


def _fnet(uf, dft, wf_bf, li):
    b, l, _ = uf.shape
    cs, cl, sl = dft
    scale = 1.0 / math.sqrt(l * FN_GROUP_DIM)
    return pl.pallas_call(
        functools.partial(_fnet_kernel, scale),
        grid=(b,),
        in_specs=[
            pl.BlockSpec((None, l, WIDTH_C), lambda bi: (bi, 0, 0)),
            _const_spec(cs.shape, lambda bi: (0, 0)),
            _const_spec(cl.shape, lambda bi: (0, 0)),
            _const_spec(sl.shape, lambda bi: (0, 0)),
            _const_spec((None, WIDTH_C, WIDTH_C), lambda bi: (li, 0, 0)),
        ],
        out_specs=pl.BlockSpec((None, l, WIDTH_C), lambda bi: (bi, 0, 0)),
        out_shape=jax.ShapeDtypeStruct((b, l, WIDTH_C), BF16),
        compiler_params=_params(("parallel",)),
        name="fnet_l%d" % l,
    )(uf, cs, cl, sl, wf_bf)


def _outproj_kernel(att_ref, rec_ref, four_ref, x_ref, mod_ref, nw_ref, wo_ref, x1_ref, h2_ref):
    mix = jnp.concatenate([att_ref[...], rec_ref[...], four_ref[...]], axis=1)
    m = mod_ref[...]
    x1 = x_ref[...] + m[2:3] * _dot(mix, wo_ref[...])
    x1_ref[...] = x1
    y = x1 * _rms_scale(x1) * nw_ref[...]
    h2_ref[...] = (y * (1.0 + m[4:5]) + m[3:4]).astype(BF16)


def _outproj(att, rec, four, x2d, mod4, nw2, wo_bf, li, seg_len, mod_row, tm):
    rows = x2d.shape[0]
    tpb = seg_len // tm
    mod_map = (lambda i: (li, i // tpb, 0, 0)) if mod_row is None else (lambda i: (li, mod_row, 0, 0))
    row_spec = lambda w: pl.BlockSpec((tm, w), lambda i: (i, 0))
    return pl.pallas_call(
        _outproj_kernel,
        grid=(rows // tm,),
        in_specs=[row_spec(WIDTH_A), row_spec(WIDTH_B), row_spec(WIDTH_C), row_spec(D_MODEL),
                  pl.BlockSpec((None, None, 6, D_MODEL), mod_map),
                  pl.BlockSpec((None, 1, D_MODEL), lambda i: (li, 0, 0)),
                  _const_spec((None, D_MODEL, D_MODEL), lambda i: (li, 0, 0))],
        out_specs=(row_spec(D_MODEL), row_spec(D_MODEL)),
        out_shape=(jax.ShapeDtypeStruct((rows, D_MODEL), F32), jax.ShapeDtypeStruct((rows, D_MODEL), BF16)),
        compiler_params=_params(("parallel",)),
        name="outproj",
    )(att, rec, four, x2d, mod4, nw2, wo_bf)


def _ffn_kernel(tm, tpb, final, h_ref, hp_ref, hn_ref, x1_ref, mod_ref, wup_ref, cw_ref, cb_ref, wdn_ref,
                fw_ref, o_ref, a_ref):
    i = pl.program_id(0)
    t = i % tpb
    prev = (hp_ref[...].astype(F32) * jnp.where(t > 0, 1.0, 0.0)).astype(BF16)
    nxt = (hn_ref[...].astype(F32) * jnp.where(t < tpb - 1, 1.0, 0.0)).astype(BF16)
    hext = jnp.concatenate([prev, h_ref[...], nxt], axis=0)
    ext = tm + 2 * HALO

    def conv(u, c0):
        w = cw_ref[:, c0:c0 + FFN_COLS]
        up = pltpu.roll(u, 1, 0)
        dn = pltpu.roll(u, ext - 1, 0)
        r = up * w[0:1] + u * w[1:2] + dn * w[2:3] + cb_ref[:, c0:c0 + FFN_COLS]
        return r[HALO:HALO + tm]

    for j in range(D_FF // FFN_COLS):
        c0 = j * FFN_COLS
        gate = conv(_dot(hext, wup_ref[:, c0:c0 + FFN_COLS]), c0)
        val = conv(_dot(hext, wup_ref[:, D_FF + c0:D_FF + c0 + FFN_COLS]), D_FF + c0)
        a_ref[:, c0:c0 + FFN_COLS] = (gate * jax.nn.sigmoid(gate) * val).astype(BF16)
    m = mod_ref[...]
    x2 = x1_ref[...] + m[5:6] * _dot(a_ref[...], wdn_ref[...])
    if final:
        x2 = x2 * _rms_scale(x2) * fw_ref[...]
    o_ref[...] = x2


def _ffn(h2, x1, mod4, wup_bf, conv_w, conv_b, wdn_bf, final_w, li, seg_len, mod_row, tm, final):
    rows = x1.shape[0]
    tpb = seg_len // tm
    hb = tm // HALO
    nhb = rows // HALO
    mod_map = (lambda i: (li, i // tpb, 0, 0)) if mod_row is None else (lambda i: (li, mod_row, 0, 0))
    return pl.pallas_call(
        functools.partial(_ffn_kernel, tm, tpb, final),
        grid=(rows // tm,),
        in_specs=[
            pl.BlockSpec((tm, D_MODEL), lambda i: (i, 0)),
            pl.BlockSpec((HALO, D_MODEL), lambda i: (jnp.maximum(i * hb - 1, 0), 0)),
            pl.BlockSpec((HALO, D_MODEL), lambda i: (jnp.minimum((i + 1) * hb, nhb - 1), 0)),
            pl.BlockSpec((tm, D_MODEL), lambda i: (i, 0)),
            pl.BlockSpec((None, None, 6, D_MODEL), mod_map),
            _const_spec((None, D_MODEL, 2 * D_FF), lambda i: (li, 0, 0)),
            pl.BlockSpec((None, 3, 2 * D_FF), lambda i: (li, 0, 0)),
            pl.BlockSpec((None, 1, 2 * D_FF), lambda i: (li, 0, 0)),
            _const_spec((None, D_FF, D_MODEL), lambda i: (li, 0, 0)),
            pl.BlockSpec((1, D_MODEL), lambda i: (0, 0)),
        ],
        out_specs=pl.BlockSpec((tm, D_MODEL), lambda i: (i, 0)),
        out_shape=jax.ShapeDtypeStruct((rows, D_MODEL), F32),
        scratch_shapes=[pltpu.VMEM((tm, D_FF), BF16)],
        compiler_params=_params(("parallel",)),
        name="convffn_final" if final else "convffn",
    )(h2, h2, h2, x1, mod4, wup_bf, conv_w, conv_b, wdn_bf, final_w)


def _rope_tables(l):
    rows = l // GRID_W
    pos_r = np.repeat(np.arange(rows), GRID_W).astype(np.float32)
    pos_c = np.tile(np.arange(GRID_W), rows).astype(np.float32)
    half = DH_A // 2
    inv_freq = (ROPE_BASE ** (-np.arange(0, half, 2, dtype=np.float32) / half)).astype(np.float32)
    ang = np.concatenate([pos_r[:, None] * inv_freq] * 2 + [pos_c[:, None] * inv_freq] * 2, axis=-1)
    cos = np.cos(ang.astype(np.float32)).astype(np.float32)
    sin = np.sin(ang.astype(np.float32)).astype(np.float32)
    cos = np.tile(cos, (1, 2))
    sin = np.tile(sin, (1, 2))
    first = (np.arange(DV_A) % 32) < 16
    sin_a = np.where(first, -sin, 0.0).astype(np.float32)
    sin_b = np.where(first, 0.0, sin).astype(np.float32)
    return jnp.asarray(cos), jnp.asarray(sin_a), jnp.asarray(sin_b)


def kernel(x, c, ctx, c_ctx, w_ada, b_ada, norm1_w, norm2_w, w_in, lam_qk, subln_w, lb_param, hgrn_norm_w,
           w_fnet, w_out, w_up, conv_w, conv_b, w_down, final_norm_w):
    b, l, d = x.shape
    lc = ctx.shape[1]
    depth = w_ada.shape[0]
    assert d == D_MODEL and l % GRID_W == 0 and b <= 8

    lb_all = jnp.cumsum(jax.nn.softmax(lb_param.astype(F32), axis=0), axis=0)
    lb_all = lb_all - lb_all[0:1]
    lbc = jnp.stack([jnp.log(lb_all[:, 0]), jnp.log1p(-lb_all[:, 0]),
                     jnp.log(lb_all[:, 1]), jnp.log1p(-lb_all[:, 1])], axis=1)
    lq = lam_qk.astype(F32)
    lam_dyn = jnp.exp(jnp.sum(lq[:, 0] * lq[:, 1], axis=-1)) - jnp.exp(jnp.sum(lq[:, 2] * lq[:, 3], axis=-1))
    rope_tabs = _rope_tables(l)
    hconst = _hgrn_constants()
    dft_l = _dft_constants(l)
    dft_c = _dft_constants(lc)
    w_in_bf = w_in.astype(BF16)
    w_out_bf = w_out.astype(BF16)
    w_up_bf = w_up.astype(BF16)
    w_dn_bf = w_down.astype(BF16)
    w_fn_bf = w_fnet.astype(BF16)
    nw1 = norm1_w.reshape(depth, 1, d)
    nw2 = norm2_w.reshape(depth, 1, d)
    sub_w = subln_w.reshape(depth, 1, DV_A)
    hw = jnp.tile(hgrn_norm_w, (1, N_HEADS_B)).reshape(depth, 1, WIDTH_B)
    cb = conv_b.reshape(depth, 1, 2 * D_FF)
    fw = final_norm_w.reshape(1, d)

    cstack = jnp.concatenate([c, c_ctx[None, :], jnp.zeros((16 - b - 1, d), F32)], axis=0)
    mod4 = _modulation(cstack, w_ada, b_ada).reshape(depth, 16, 6, d)
    ctx_row = b

    zeros_state = jnp.zeros((b, WIDTH_B, WIDTH_B), F32)
    xl = x.reshape(b * l, d)
    xc = ctx.reshape(b * lc, d)
    tm_l = 512
    tm_c = 256
    for li in range(depth):
        last = li == depth - 1
        lam_init = 0.8 - 0.6 * math.exp(-0.3 * li)
        lam = (lam_dyn[li] + lam_init).reshape(1, 1)

        qa, ka, va, hg, uf = _inproj(xl, mod4, nw1, w_in_bf, lbc, rope_tabs, li, l, None, tm_l)
        qc, kc, vc, hgc, ufc = _inproj(xc, mod4, nw1, w_in_bf, lbc, None, li, lc, ctx_row, tm_c)
        r3 = lambda a, n: a.reshape(b, n, a.shape[-1])

        att = _attention(lam, r3(qa, l), [r3(kc, lc), r3(ka, l)], [r3(vc, lc), r3(va, l)], (1, 1), sub_w, li,
                         1.0 - lam_init, l, 128)
        rec_c, sf, sb = _hgrn(r3(hgc, lc), hconst, hw, zeros_state, zeros_state, li)
        rec, _, _ = _hgrn(r3(hg, l), hconst, hw, sf, sb, li)
        four = _fnet(r3(uf, l), dft_l, w_fn_bf, li)

        x1, h2 = _outproj(att.reshape(b * l, -1), rec.reshape(b * l, -1), four.reshape(b * l, -1), xl, mod4,
                          nw2, w_out_bf, li, l, None, tm_l)
        xl = _ffn(h2, x1, mod4, w_up_bf, conv_w, cb, w_dn_bf, fw, li, l, None, tm_l, last)

        if not last:
            att_c = _attention(lam, r3(qc, lc), [r3(kc, lc)], [r3(vc, lc)], (1,), sub_w, li, 1.0 - lam_init,
                               256, 128)
            four_c = _fnet(r3(ufc, lc), dft_c, w_fn_bf, li)
            x1c, h2c = _outproj(att_c.reshape(b * lc, -1), rec_c.reshape(b * lc, -1),
                                four_c.reshape(b * lc, -1), xc, mod4, nw2, w_out_bf, li, lc, ctx_row, tm_c)
            xc = _ffn(h2c, x1c, mod4, w_up_bf, conv_w, cb, w_dn_bf, fw, li, lc, ctx_row, tm_c, False)
    return xl.reshape(b, l, d)
```

```python
import functools
import math

import numpy as np
import jax
import jax.numpy as jnp
from jax import lax
from jax.experimental import pallas as pl
from jax.experimental.pallas import tpu as pltpu

F32 = jnp.float32
BF16 = jnp.bfloat16

D_MODEL = 1024
GRID_W = 64
N_HEADS_A = 4
DH_A = 64
DV_A = 2 * DH_A
WIDTH_A = N_HEADS_A * DV_A
N_HEADS_B = 4
DK_B = 64
DV_B = 64
WIDTH_B = N_HEADS_B * DV_B
FN_GROUPS = 4
FN_GROUP_DIM = 64
WIDTH_C = FN_GROUPS * FN_GROUP_DIM
PROJ_WIDTH = 3 * WIDTH_A + 5 * WIDTH_B + WIDTH_C
D_FF = 2816
ROPE_BASE = 10000.0
EPS = 1e-6
LOG2E = math.log2(math.e)
exp2 = jnp.exp2
HG_UNIT = LOG2E

V7X_LANES = 128
V7X_BF16_SUBLANES = 16
V7X_VMEM_BYTES = 64 * 1024 * 1024
VMEM_LIMIT = V7X_VMEM_BYTES - 8 * 1024 * 1024

HG_CHUNK = 64
HG_LEVELS = 6
HG_WIDTHS = tuple(HG_CHUNK >> (l + 1) for l in range(HG_LEVELS))
FFN_COLS = 256
HALO = V7X_BF16_SUBLANES


def _params(sem):
    return pltpu.CompilerParams(dimension_semantics=sem, vmem_limit_bytes=VMEM_LIMIT)


def _const_spec(shape, index_map):
    return pl.BlockSpec(shape, index_map, pipeline_mode=pl.Buffered(1))


def _dot(a, b):
    return jnp.dot(a, b, preferred_element_type=F32)


def _dot_nt(a, b):
    return lax.dot_general(a, b, (((1,), (1,)), ((), ())), preferred_element_type=F32)


def _dot_tn(a, b):
    return lax.dot_general(a, b, (((0,), (0,)), ((), ())), preferred_element_type=F32)


def _split3(x):
    h = x.astype(BF16)
    r = x - h.astype(F32)
    m = r.astype(BF16)
    return h, m, (r - m.astype(F32)).astype(BF16)


def _split2(x):
    h = x.astype(BF16)
    return h, (x - h.astype(F32)).astype(BF16)


def _rms_scale(x):
    return lax.rsqrt(jnp.mean(x * x, axis=-1, keepdims=True) + EPS)


def _mod_kernel(c_ref, w_ref, b_ref, o_ref):
    c = c_ref[...]
    a = (c * jax.nn.sigmoid(c)).astype(BF16)
    o_ref[0] = _dot(a, w_ref[0].astype(BF16)) + b_ref[0]


def _modulation(cstack, w_ada, b_ada):
    depth, _, n = w_ada.shape
    tn = 1536
    rows = cstack.shape[0]
    return pl.pallas_call(
        _mod_kernel,
        grid=(depth, n // tn),
        in_specs=[
            pl.BlockSpec((rows, D_MODEL), lambda l, j: (0, 0)),
            pl.BlockSpec((1, D_MODEL, tn), lambda l, j: (l, 0, j)),
            pl.BlockSpec((1, 1, tn), lambda l, j: (l, 0, j)),
        ],
        out_specs=pl.BlockSpec((1, rows, tn), lambda l, j: (l, 0, j)),
        out_shape=jax.ShapeDtypeStruct((depth, rows, n), F32),
        compiler_params=_params(("arbitrary", "arbitrary")),
        name="modulation",
    )(cstack, w_ada, b_ada.reshape(depth, 1, n))


def _log_forget(z, log_lb, log1m_lb):
    ls = jnp.minimum(z, 0.0) - jnp.log1p(jnp.exp(-jnp.abs(z)))
    b = log1m_lb + ls
    return jnp.maximum(log_lb, b) + jnp.log1p(jnp.exp(-jnp.abs(log_lb - b)))


def _inproj_kernel(rope, x_ref, mod_ref, nw_ref, w_ref, lbc_ref, *rest):
    if rope:
        cos_ref, sa_ref, sb_ref, qa_ref, ka_ref, va_ref, hg_ref, uf_ref = rest
    else:
        qa_ref, ka_ref, va_ref, hg_ref, uf_ref = rest
    x = x_ref[...]
    m = mod_ref[...]
    y = x * _rms_scale(x) * nw_ref[...]
    h = (y * (1.0 + m[1:2]) + m[0:1]).astype(BF16)

    def proj(c0, n):
        return _dot(h, w_ref[:, c0:c0 + n])

    for base, ref, scale in ((0, qa_ref, DH_A ** -0.5 * math.log2(math.e)), (WIDTH_A, ka_ref, None)):
        t = proj(base, WIDTH_A)
        for hd in range(N_HEADS_A):
            th = t[:, hd * DV_A:(hd + 1) * DV_A]
            if rope:
                th = (th * cos_ref[...]
                      + pltpu.roll(th, V7X_LANES - 16, 1) * sa_ref[...]
                      + pltpu.roll(th, 16, 1) * sb_ref[...])
            if scale is not None:
                th = th * scale
            ref[:, hd * DV_A:(hd + 1) * DV_A] = th.astype(BF16)
    va_ref[...] = proj(2 * WIDTH_A, WIDTH_A).astype(BF16)
    c0 = 3 * WIDTH_A
    hg_ref[:, 0:WIDTH_B] = proj(c0, WIDTH_B)
    lbc = lbc_ref[...]
    for dirn in range(2):
        z = proj(c0 + WIDTH_B * (1 + dirn), WIDTH_B)
        hg_ref[:, WIDTH_B * (1 + dirn):WIDTH_B * (2 + dirn)] = _log_forget(
            z, lbc[2 * dirn:2 * dirn + 1], lbc[2 * dirn + 1:2 * dirn + 2])
    hg_ref[:, 3 * WIDTH_B:5 * WIDTH_B] = proj(c0 + 3 * WIDTH_B, 2 * WIDTH_B)
    uf_ref[...] = proj(c0 + 5 * WIDTH_B, WIDTH_C).astype(BF16)


def _inproj(x2d, mod4, nw, w_bf, lbc, rope_tabs, li, seg_len, mod_row, tm):
    rows = x2d.shape[0]
    tpb = seg_len // tm
    rope = rope_tabs is not None
    mod_map = (lambda i: (li, i // tpb, 0, 0)) if mod_row is None else (lambda i: (li, mod_row, 0, 0))
    in_specs = [
        pl.BlockSpec((tm, D_MODEL), lambda i: (i, 0)),
        pl.BlockSpec((None, None, 6, D_MODEL), mod_map),
        pl.BlockSpec((None, 1, D_MODEL), lambda i: (li, 0, 0)),
        _const_spec((None, D_MODEL, PROJ_WIDTH), lambda i: (li, 0, 0)),
        pl.BlockSpec((None, 4, WIDTH_B), lambda i: (li, 0, 0)),
    ]
    args = [x2d, mod4, nw, w_bf, lbc]
    if rope:
        in_specs += [pl.BlockSpec((tm, DV_A), lambda i: (i % tpb, 0))] * 3
        args += list(rope_tabs)
    out_shape = (
        jax.ShapeDtypeStruct((rows, WIDTH_A), BF16),
        jax.ShapeDtypeStruct((rows, WIDTH_A), BF16),
        jax.ShapeDtypeStruct((rows, WIDTH_A), BF16),
        jax.ShapeDtypeStruct((rows, 5 * WIDTH_B), F32),
        jax.ShapeDtypeStruct((rows, WIDTH_C), BF16),
    )
    out_specs = (
        pl.BlockSpec((tm, WIDTH_A), lambda i: (i, 0)),
        pl.BlockSpec((tm, WIDTH_A), lambda i: (i, 0)),
        pl.BlockSpec((tm, WIDTH_A), lambda i: (i, 0)),
        pl.BlockSpec((tm, 5 * WIDTH_B), lambda i: (i, 0)),
        pl.BlockSpec((tm, WIDTH_C), lambda i: (i, 0)),
    )
    return pl.pallas_call(
        functools.partial(_inproj_kernel, rope),
        grid=(rows // tm,),
        in_specs=in_specs,
        out_specs=out_specs,
        out_shape=out_shape,
        compiler_params=_params(("parallel",)),
        name="inproj_rope" if rope else "inproj",
    )(*args)


def _attn_kernel(nseg, tq, sub, out_scale, lam_ref, q_ref, *rest):
    k_refs = rest[:nseg]
    v_refs = rest[nseg:2 * nseg]
    w_ref, o_ref = rest[2 * nseg:]
    vexts = []
    for v_ref in v_refs:
        v = v_ref[...]
        vl = lax.broadcasted_iota(jnp.int32, v.shape, 1)
        vexts.append(jnp.concatenate([v, jnp.where(vl == 0, 1.0, 0.0).astype(BF16)], axis=1))
    lam = lam_ref[0, 0]
    wn = w_ref[...] * out_scale
    lane = lax.broadcasted_iota(jnp.int32, (sub, DV_A), 1)
    for r0 in range(0, tq, sub):
        q = q_ref[r0:r0 + sub, :].astype(F32)
        qs = jnp.concatenate([jnp.where(lane < DH_A, q, 0.0), jnp.where(lane >= DH_A, q, 0.0)],
                             axis=0).astype(BF16)
        scores = [_dot_nt(qs, k_ref[...]) for k_ref in k_refs]
        mx = jnp.max(scores[0], axis=-1, keepdims=True)
        for s in scores[1:]:
            mx = jnp.maximum(mx, jnp.max(s, axis=-1, keepdims=True))
        acc = sum(_dot(exp2(s - mx).astype(BF16), vext) for s, vext in zip(scores, vexts))
        r = acc[:, :DV_A] / acc[:, DV_A:DV_A + 1]
        o = r[:sub] - lam * r[sub:]
        o_ref[r0:r0 + sub, :] = (o * _rms_scale(o) * wn).astype(BF16)


def _attention(lam, q, ks, vs, splits, subln_w, li, out_scale, tq, sub):
    b, lq, _ = q.shape
    in_specs = [
        pl.BlockSpec(memory_space=pltpu.SMEM),
        pl.BlockSpec((None, tq, DV_A), lambda bi, h, i: (bi, i, h)),
    ]
    k_args, v_args, kv_specs = [], [], []
    for k_arr, v_arr, n in zip(ks, vs, splits):
        for part in range(n):
            k_args.append(k_arr)
            v_args.append(v_arr)
            kv_specs.append(pl.BlockSpec((None, k_arr.shape[1] // n, DV_A),
                                         lambda bi, h, i, part=part: (bi, part, h)))
    nseg = len(k_args)
    ks, vs = k_args, v_args
    in_specs += kv_specs + kv_specs
    in_specs.append(pl.BlockSpec((None, 1, DV_A), lambda bi, h, i: (li, 0, 0)))
    return pl.pallas_call(
        functools.partial(_attn_kernel, nseg, tq, sub, out_scale),
        grid=(b, N_HEADS_A, lq // tq),
        in_specs=in_specs,
        out_specs=pl.BlockSpec((None, tq, DV_A), lambda bi, h, i: (bi, i, h)),
        out_shape=jax.ShapeDtypeStruct((b, lq, WIDTH_A), BF16),
        compiler_params=_params(("parallel", "parallel", "arbitrary")),
        name="diff_attention_%dseg" % nseg,
    )(lam, q, *ks, *vs, subln_w)


def _hgrn_constants():
    c = HG_CHUNK
    t = np.arange(c)[:, None]
    r = np.arange(c)[None, :]
    amask = []
    lomask = []
    for w in HG_WIDTHS:
        hi = (t % (2 * w)) >= w
        same_pair = (t // (2 * w)) == (r // (2 * w))
        amask.append(same_pair & (hi != ((r % (2 * w)) >= w)))
        lomask.append(np.broadcast_to(~hi, (c, c)))
    psum = np.zeros((2 * c, 2 * c), np.float32)
    psum[:c, :c] = r <= t
    psum[c:, c:] = r >= t
    tile = lambda a: np.tile(np.asarray(a, np.float32), (1, N_HEADS_B))
    amask = np.stack([tile(a) for a in amask])
    lomask = np.stack([tile(a) for a in lomask])
    hd = np.arange(WIDTH_B) // DK_B
    bd = (hd[:, None] == hd[None, :]).astype(np.float32)
    return (jnp.asarray(psum, BF16), jnp.asarray(amask, F32), jnp.asarray(lomask, F32), jnp.asarray(bd, F32))


def _hgrn_kernel(nc, q_ref, gf_ref, gb_ref, v_ref, og_ref, psum_ref, amask_ref, lomask_ref,
                 bd_ref, hw_ref, sf0_ref, sb0_ref, rec_ref, sf_ref, sb_ref, sfp_ref, sbn_ref, cf_ref, cbs_ref,
                 qa_ref, ka_ref, qb_ref, kb_ref, aa_ref, pa_ref, ab_ref, pb_ref, oa_ref, ob_ref):
    c = HG_CHUNK
    bd = bd_ref[...]
    bd_bf = bd.astype(BF16)
    lane_head = lax.broadcasted_iota(jnp.int32, (c, WIDTH_B), 1) // DK_B

    def stack_heads(a):
        return jnp.concatenate([jnp.where(lane_head == h, a, 0.0).astype(BF16) for h in range(N_HEADS_B)],
                               axis=0)

    def rows(ref, n):
        if isinstance(n, int):
            return ref[n * c:(n + 1) * c, :]
        return ref[pl.ds(pl.multiple_of(n * c, c), c), :]

    sf_ref[...] = sf0_ref[...]
    sb_ref[...] = sb0_ref[...]

    def decayed_keys(i, kt_ref, dec_ref):
        nf = i
        nb = nc - 1 - i
        gf = rows(gf_ref, nf)
        gb = rows(gb_ref, nb)
        g = jnp.concatenate([gf, gb], axis=0)
        sums = sum(_dot(psum_ref[...], part) for part in _split3(g))
        cf = sums[:c]
        cbs = sums[c:]
        cf_ref[nf] = cf * HG_UNIT
        cbs_ref[nb] = cbs * HG_UNIT
        tot_f = cf[c - 1:c]
        tot_b = cbs[0:1]
        kt_ref[0] = (1.0 - jnp.exp(gf)) * jnp.exp(tot_f - cf)
        kt_ref[1] = (1.0 - jnp.exp(gb)) * jnp.exp(tot_b - cbs)
        dec_ref[0:1, :] = jnp.exp(tot_f)
        dec_ref[8:9, :] = jnp.exp(tot_b)

    def apply_step(i, kt_ref, dec_ref):
        nf = i
        nb = nc - 1 - i
        upd_f = _dot_tn(rows(v_ref, nf).astype(BF16), kt_ref[0].astype(BF16)) * bd
        upd_b = _dot_tn(rows(v_ref, nb).astype(BF16), kt_ref[1].astype(BF16)) * bd
        stf = sf_ref[...]
        sfp_ref[nf] = stf.astype(BF16)
        sf_ref[...] = stf * dec_ref[0:1, :] + upd_f
        stb = sb_ref[...]
        sbn_ref[nb] = stb.astype(BF16)
        sb_ref[...] = stb * dec_ref[8:9, :] + upd_b

    pre = ((qa_ref, oa_ref), (qb_ref, ob_ref))
    decayed_keys(0, *pre[0])

    def pre_body(j, carry):
        for par in (0, 1):
            i = 2 * j + par
            apply_step(i, *pre[par])
            decayed_keys(jnp.minimum(i + 1, nc - 1), *pre[1 - par])
        return carry

    lax.fori_loop(0, nc // 2, pre_body, 0)

    def prepare(n, qop_ref, kop_ref):
        q = rows(q_ref, n)
        gf = rows(gf_ref, n) * HG_UNIT
        gb = rows(gb_ref, n) * HG_UNIT
        cf = cf_ref[n]
        cbs = cbs_ref[n]
        kf = 1.0 - exp2(gf)
        kb = 1.0 - exp2(gb)
        for lvl, w in enumerate(HG_WIDTHS):
            lo = lomask_ref[lvl] > 0.5
            if 2 * w >= 8:
                anchor = lambda a, r: jnp.broadcast_to(
                    a.reshape(c // (2 * w), 2 * w, WIDTH_B)[:, r:r + 1, :],
                    (c // (2 * w), 2 * w, WIDTH_B)).reshape(c, WIDTH_B)
                d_f = cf - anchor(cf, w - 1)
                d_b = cbs - anchor(cbs, w)
                eq = exp2(jnp.where(lo, d_b, d_f))
                ek = exp2(-jnp.where(lo, d_f, d_b))
            elif w == 2:
                r4 = lax.broadcasted_iota(jnp.int32, (c, WIDTH_B), 0) & 3
                gf_up, gf_dn = pltpu.roll(gf, c - 1, 0), pltpu.roll(gf, 1, 0)
                gb_up, gb_dn = pltpu.roll(gb, c - 1, 0), pltpu.roll(gb, 1, 0)
                eq = exp2(jnp.where(r4 == 0, gb + gb_up, jnp.where(r4 == 1, gb,
                              jnp.where(r4 == 2, gf, gf + gf_dn))))
                ek = exp2(jnp.where(r4 == 0, gf_up, jnp.where(r4 == 3, gb_dn, 0.0)))
            else:
                eq = exp2(jnp.where(lo, gb, gf))
                ek = None
            klev = jnp.where(lo, kf, kb)
            qop_ref[lvl] = q * eq
            kop_ref[lvl] = klev if ek is None else klev * ek
        qop_ref[HG_LEVELS] = q * exp2(cf)
        qop_ref[HG_LEVELS + 1] = q * exp2(cbs)
        qop_ref[HG_LEVELS + 2] = q * (kf + kb)

    def scores(n, qop_ref, kop_ref, a_ref, part_ref):
        a_all = jnp.zeros((c, N_HEADS_B * c), F32)
        for lvl in range(HG_LEVELS):
            a_all = a_all + _dot_nt(qop_ref[lvl].astype(BF16), stack_heads(kop_ref[lvl])) * amask_ref[lvl]
        a_ref[...] = a_all
        part = _dot(qop_ref[HG_LEVELS + 2].astype(BF16), bd_bf) * rows(v_ref, n)
        part = part + _dot_nt(qop_ref[HG_LEVELS].astype(BF16), sfp_ref[n])
        part_ref[...] = part + _dot_nt(qop_ref[HG_LEVELS + 1].astype(BF16), sbn_ref[n])

    def values(n, a_ref, part_ref, o_ref):
        o_ref[...] = (_dot(a_ref[...].astype(BF16), stack_heads(rows(v_ref, n)))
                      + part_ref[...])

    def finish(n, o_ref):
        o = o_ref[...]
        sq_h, sq_l = _split2(o * o)
        ms = _dot(jnp.concatenate([sq_h, sq_l], axis=0), bd_bf)
        ms = (ms[:c] + ms[c:]) * (1.0 / DV_B)
        og = rows(og_ref, n)
        rec = o * lax.rsqrt(ms + EPS) * hw_ref[...] * (og * jax.nn.sigmoid(og))
        rec_ref[pl.ds(pl.multiple_of(n * c, c), c), :] = rec.astype(BF16)

    ops = ((qa_ref, ka_ref), (qb_ref, kb_ref))
    sc = ((aa_ref, pa_ref), (ab_ref, pb_ref))
    ob = (oa_ref, ob_ref)
    last = nc - 1

    def step(m, par, clamp):
        idx = (lambda i: jnp.minimum(i, last)) if clamp else (lambda i: i)
        if clamp or m >= 0:
            finish(m, ob[par])
        if clamp or m + 1 >= 0:
            values(idx(m + 1), *sc[1 - par], ob[1 - par])
        if clamp or m + 2 >= 0:
            scores(idx(m + 2), *ops[par], *sc[par])
        prepare(idx(m + 3), *ops[1 - par])

    for m in (-3, -2, -1):
        step(m, m % 2, False)

    def body(j, carry):
        step(2 * j, 0, True)
        step(2 * j + 1, 1, True)
        return carry

    lax.fori_loop(0, nc // 2, body, 0)


def _hgrn(hg, hconst, hw, sf0, sb0, li):
    b, l, _ = hg.shape
    nc = l // HG_CHUNK
    psum, amask, lomask, bd = hconst
    col = lambda j: pl.BlockSpec((None, l, WIDTH_B), lambda bi: (bi, 0, j))
    full = lambda a: _const_spec(a.shape, lambda bi: (0,) * a.ndim)
    st_spec = pl.BlockSpec((None, WIDTH_B, WIDTH_B), lambda bi: (bi, 0, 0))
    state_scratch = pltpu.VMEM((nc, WIDTH_B, WIDTH_B), BF16)
    sums_scratch = pltpu.VMEM((nc, HG_CHUNK, WIDTH_B), F32)
    q_operands = pltpu.VMEM((HG_LEVELS + 3, HG_CHUNK, WIDTH_B), F32)
    k_operands = pltpu.VMEM((HG_LEVELS, HG_CHUNK, WIDTH_B), F32)
    chunk_f32 = pltpu.VMEM((HG_CHUNK, WIDTH_B), F32)
    return pl.pallas_call(
        functools.partial(_hgrn_kernel, nc),
        grid=(b,),
        in_specs=[col(0), col(1), col(2), col(3), col(4), full(psum), full(amask),
                  full(lomask), full(bd), pl.BlockSpec((None, 1, WIDTH_B), lambda bi: (li, 0, 0)),
                  st_spec, st_spec],
        out_specs=(pl.BlockSpec((None, l, WIDTH_B), lambda bi: (bi, 0, 0)), st_spec, st_spec),
        out_shape=(jax.ShapeDtypeStruct((b, l, WIDTH_B), BF16),
                   jax.ShapeDtypeStruct((b, WIDTH_B, WIDTH_B), F32),
                   jax.ShapeDtypeStruct((b, WIDTH_B, WIDTH_B), F32)),
        scratch_shapes=[state_scratch, state_scratch, sums_scratch, sums_scratch,
                        q_operands, k_operands, q_operands, k_operands,
                        chunk_f32, chunk_f32, chunk_f32, chunk_f32, chunk_f32, chunk_f32],
        compiler_params=_params(("parallel",)),
        name="hgrn_l%d" % l,
    )(hg, hg, hg, hg, hg, psum, amask, lomask, bd, hw, sf0, sb0)


def _dft_constants(l):
    k = np.arange(FN_GROUP_DIM)
    ang = 2.0 * np.pi * ((k[:, None] * k[None, :]) % FN_GROUP_DIM) / FN_GROUP_DIM
    eye = np.eye(FN_GROUPS)
    cs = np.concatenate([np.kron(eye, np.cos(ang)), np.kron(eye, np.sin(ang))], axis=1)
    n = np.arange(l)
    angl = 2.0 * np.pi * ((n[:, None] * n[None, :]) % l) / l
    return tuple(jnp.asarray(a, F32).astype(BF16) for a in (cs, np.cos(angl), np.sin(angl)))


def _fnet_kernel(scale, u_ref, cs_ref, cl_ref, sl_ref, wf_ref, o_ref):
    t = _dot(u_ref[...], cs_ref[...])
    uc = t[:, :WIDTH_C].astype(BF16)
    us = t[:, WIDTH_C:].astype(BF16)
    y = (_dot(cl_ref[...], uc) - _dot(sl_ref[...], us)) * scale
    o_ref[...] = _dot(y.astype(BF16), wf_ref[...]).astype(BF16)


def _fnet(uf, dft, wf_bf, li):
    b, l, _ = uf.shape
    cs, cl, sl = dft
    scale = 1.0 / math.sqrt(l * FN_GROUP_DIM)
    return pl.pallas_call(
        functools.partial(_fnet_kernel, scale),
        grid=(b,),
        in_specs=[
            pl.BlockSpec((None, l, WIDTH_C), lambda bi: (bi, 0, 0)),
            _const_spec(cs.shape, lambda bi: (0, 0)),
            _const_spec(cl.shape, lambda bi: (0, 0)),
            _const_spec(sl.shape, lambda bi: (0, 0)),
            _const_spec((None, WIDTH_C, WIDTH_C), lambda bi: (li, 0, 0)),
        ],
        out_specs=pl.BlockSpec((None, l, WIDTH_C), lambda bi: (bi, 0, 0)),
        out_shape=jax.ShapeDtypeStruct((b, l, WIDTH_C), BF16),
        compiler_params=_params(("parallel",)),
        name="fnet_l%d" % l,
    )(uf, cs, cl, sl, wf_bf)


def _outproj_kernel(att_ref, rec_ref, four_ref, x_ref, mod_ref, nw_ref, wo_ref, x1_ref, h2_ref):
    mix = jnp.concatenate([att_ref[...], rec_ref[...], four_ref[...]], axis=1)
    m = mod_ref[...]
    x1 = x_ref[...] + m[2:3] * _dot(mix, wo_ref[...])
    x1_ref[...] = x1
    y = x1 * _rms_scale(x1) * nw_ref[...]
    h2_ref[...] = (y * (1.0 + m[4:5]) + m[3:4]).astype(BF16)


def _outproj(att, rec, four, x2d, mod4, nw2, wo_bf, li, seg_len, mod_row, tm):
    rows = x2d.shape[0]
    tpb = seg_len // tm
    mod_map = (lambda i: (li, i // tpb, 0, 0)) if mod_row is None else (lambda i: (li, mod_row, 0, 0))
    row_spec = lambda w: pl.BlockSpec((tm, w), lambda i: (i, 0))
    return pl.pallas_call(
        _outproj_kernel,
        grid=(rows // tm,),
        in_specs=[row_spec(WIDTH_A), row_spec(WIDTH_B), row_spec(WIDTH_C), row_spec(D_MODEL),
                  pl.BlockSpec((None, None, 6, D_MODEL), mod_map),
                  pl.BlockSpec((None, 1, D_MODEL), lambda i: (li, 0, 0)),
                  _const_spec((None, D_MODEL, D_MODEL), lambda i: (li, 0, 0))],
        out_specs=(row_spec(D_MODEL), row_spec(D_MODEL)),
        out_shape=(jax.ShapeDtypeStruct((rows, D_MODEL), F32), jax.ShapeDtypeStruct((rows, D_MODEL), BF16)),
        compiler_params=_params(("parallel",)),
        name="outproj",
    )(att, rec, four, x2d, mod4, nw2, wo_bf)


def _ffn_kernel(tm, tpb, final, h_ref, hp_ref, hn_ref, x1_ref, mod_ref, wup_ref, cw_ref, cb_ref, wdn_ref,
                fw_ref, o_ref, a_ref):
    i = pl.program_id(0)
    t = i % tpb
    prev = (hp_ref[...].astype(F32) * jnp.where(t > 0, 1.0, 0.0)).astype(BF16)
    nxt = (hn_ref[...].astype(F32) * jnp.where(t < tpb - 1, 1.0, 0.0)).astype(BF16)
    hext = jnp.concatenate([prev, h_ref[...], nxt], axis=0)
    ext = tm + 2 * HALO

    def conv(u, c0):
        w = cw_ref[:, c0:c0 + FFN_COLS]
        up = pltpu.roll(u, 1, 0)
        dn = pltpu.roll(u, ext - 1, 0)
        r = up * w[0:1] + u * w[1:2] + dn * w[2:3] + cb_ref[:, c0:c0 + FFN_COLS]
        return r[HALO:HALO + tm]

    for j in range(D_FF // FFN_COLS):
        c0 = j * FFN_COLS
        gate = conv(_dot(hext, wup_ref[:, c0:c0 + FFN_COLS]), c0)
        val = conv(_dot(hext, wup_ref[:, D_FF + c0:D_FF + c0 + FFN_COLS]), D_FF + c0)
        a_ref[:, c0:c0 + FFN_COLS] = (gate * jax.nn.sigmoid(gate) * val).astype(BF16)
    m = mod_ref[...]
    x2 = x1_ref[...] + m[5:6] * _dot(a_ref[...], wdn_ref[...])
    if final:
        x2 = x2 * _rms_scale(x2) * fw_ref[...]
    o_ref[...] = x2


def _ffn(h2, x1, mod4, wup_bf, conv_w, conv_b, wdn_bf, final_w, li, seg_len, mod_row, tm, final):
    rows = x1.shape[0]
    tpb = seg_len // tm
    hb = tm // HALO
    nhb = rows // HALO
    mod_map = (lambda i: (li, i // tpb, 0, 0)) if mod_row is None else (lambda i: (li, mod_row, 0, 0))
    return pl.pallas_call(
        functools.partial(_ffn_kernel, tm, tpb, final),
        grid=(rows // tm,),
        in_specs=[
            pl.BlockSpec((tm, D_MODEL), lambda i: (i, 0)),
            pl.BlockSpec((HALO, D_MODEL), lambda i: (jnp.maximum(i * hb - 1, 0), 0)),
            pl.BlockSpec((HALO, D_MODEL), lambda i: (jnp.minimum((i + 1) * hb, nhb - 1), 0)),
            pl.BlockSpec((tm, D_MODEL), lambda i: (i, 0)),
            pl.BlockSpec((None, None, 6, D_MODEL), mod_map),
            _const_spec((None, D_MODEL, 2 * D_FF), lambda i: (li, 0, 0)),
            pl.BlockSpec((None, 3, 2 * D_FF), lambda i: (li, 0, 0)),
            pl.BlockSpec((None, 1, 2 * D_FF), lambda i: (li, 0, 0)),
            _const_spec((None, D_FF, D_MODEL), lambda i: (li, 0, 0)),
            pl.BlockSpec((1, D_MODEL), lambda i: (0, 0)),
        ],
        out_specs=pl.BlockSpec((tm, D_MODEL), lambda i: (i, 0)),
        out_shape=jax.ShapeDtypeStruct((rows, D_MODEL), F32),
        scratch_shapes=[pltpu.VMEM((tm, D_FF), BF16)],
        compiler_params=_params(("parallel",)),
        name="convffn_final" if final else "convffn",
    )(h2, h2, h2, x1, mod4, wup_bf, conv_w, conv_b, wdn_bf, final_w)


def _rope_tables(l):
    rows = l // GRID_W
    pos_r = np.repeat(np.arange(rows), GRID_W).astype(np.float32)
    pos_c = np.tile(np.arange(GRID_W), rows).astype(np.float32)
    half = DH_A // 2
    inv_freq = (ROPE_BASE ** (-np.arange(0, half, 2, dtype=np.float32) / half)).astype(np.float32)
    ang = np.concatenate([pos_r[:, None] * inv_freq] * 2 + [pos_c[:, None] * inv_freq] * 2, axis=-1)
    cos = np.cos(ang.astype(np.float32)).astype(np.float32)
    sin = np.sin(ang.astype(np.float32)).astype(np.float32)
    cos = np.tile(cos, (1, 2))
    sin = np.tile(sin, (1, 2))
    first = (np.arange(DV_A) % 32) < 16
    sin_a = np.where(first, -sin, 0.0).astype(np.float32)
    sin_b = np.where(first, 0.0, sin).astype(np.float32)
    return jnp.asarray(cos), jnp.asarray(sin_a), jnp.asarray(sin_b)


def kernel(x, c, ctx, c_ctx, w_ada, b_ada, norm1_w, norm2_w, w_in, lam_qk, subln_w, lb_param, hgrn_norm_w,
           w_fnet, w_out, w_up, conv_w, conv_b, w_down, final_norm_w):
    b, l, d = x.shape
    lc = ctx.shape[1]
    depth = w_ada.shape[0]
    assert d == D_MODEL and l % GRID_W == 0 and b <= 8

    lb_all = jnp.cumsum(jax.nn.softmax(lb_param.astype(F32), axis=0), axis=0)
    lb_all = lb_all - lb_all[0:1]
    lbc = jnp.stack([jnp.log(lb_all[:, 0]), jnp.log1p(-lb_all[:, 0]),
                     jnp.log(lb_all[:, 1]), jnp.log1p(-lb_all[:, 1])], axis=1)
    lq = lam_qk.astype(F32)
    lam_dyn = jnp.exp(jnp.sum(lq[:, 0] * lq[:, 1], axis=-1)) - jnp.exp(jnp.sum(lq[:, 2] * lq[:, 3], axis=-1))
    rope_tabs = _rope_tables(l)
    hconst = _hgrn_constants()
    dft_l = _dft_constants(l)
    dft_c = _dft_constants(lc)
    w_in_bf = w_in.astype(BF16)
    w_out_bf = w_out.astype(BF16)
    w_up_bf = w_up.astype(BF16)
    w_dn_bf = w_down.astype(BF16)
    w_fn_bf = w_fnet.astype(BF16)
    nw1 = norm1_w.reshape(depth, 1, d)
    nw2 = norm2_w.reshape(depth, 1, d)
    sub_w = subln_w.reshape(depth, 1, DV_A)
    hw = jnp.tile(hgrn_norm_w, (1, N_HEADS_B)).reshape(depth, 1, WIDTH_B)
    cb = conv_b.reshape(depth, 1, 2 * D_FF)
    fw = final_norm_w.reshape(1, d)

    cstack = jnp.concatenate([c, c_ctx[None, :], jnp.zeros((16 - b - 1, d), F32)], axis=0)
    mod4 = _modulation(cstack, w_ada, b_ada).reshape(depth, 16, 6, d)
    ctx_row = b

    zeros_state = jnp.zeros((b, WIDTH_B, WIDTH_B), F32)
    xl = x.reshape(b * l, d)
    xc = ctx.reshape(b * lc, d)
    tm_l = 1024
    tm_c = 256
    for li in range(depth):
        last = li == depth - 1
        lam_init = 0.8 - 0.6 * math.exp(-0.3 * li)
        lam = (lam_dyn[li] + lam_init).reshape(1, 1)

        qa, ka, va, hg, uf = _inproj(xl, mod4, nw1, w_in_bf, lbc, rope_tabs, li, l, None, tm_l)
        qc, kc, vc, hgc, ufc = _inproj(xc, mod4, nw1, w_in_bf, lbc, None, li, lc, ctx_row, tm_c)
        r3 = lambda a, n: a.reshape(b, n, a.shape[-1])

        att = _attention(lam, r3(qa, l), [r3(kc, lc), r3(ka, l)], [r3(vc, lc), r3(va, l)], (1, 1), sub_w, li,
                         1.0 - lam_init, l, 128)
        rec_c, sf, sb = _hgrn(r3(hgc, lc), hconst, hw, zeros_state, zeros_state, li)
        rec, _, _ = _hgrn(r3(hg, l), hconst, hw, sf, sb, li)
        four = _fnet(r3(uf, l), dft_l, w_fn_bf, li)

        x1, h2 = _outproj(att.reshape(b * l, -1), rec.reshape(b * l, -1), four.reshape(b * l, -1), xl, mod4,
                          nw2, w_out_bf, li, l, None, tm_l)
        xl = _ffn(h2, x1, mod4, w_up_bf, conv_w, cb, w_dn_bf, fw, li, l, None, tm_l, last)

        if not last:
            att_c = _attention(lam, r3(qc, lc), [r3(kc, lc)], [r3(vc, lc)], (1,), sub_w, li, 1.0 - lam_init,
                               256, 128)
            four_c = _fnet(r3(ufc, lc), dft_c, w_fn_bf, li)
            x1c, h2c = _outproj(att_c.reshape(b * lc, -1), rec_c.reshape(b * lc, -1),
                                four_c.reshape(b * lc, -1), xc, mod4, nw2, w_out_bf, li, lc, ctx_row, tm_c)
            xc = _ffn(h2c, x1c, mod4, w_up_bf, conv_w, cb, w_dn_bf, fw, li, lc, ctx_row, tm_c, False)
    return xl.reshape(b, l, d)
```

```python
import functools
import math

import numpy as np
import jax
import jax.numpy as jnp
from jax import lax
from jax.experimental import pallas as pl
from jax.experimental.pallas import tpu as pltpu

F32 = jnp.float32
BF16 = jnp.bfloat16

D_MODEL = 1024
GRID_W = 64
N_HEADS_A = 4
DH_A = 64
DV_A = 2 * DH_A
WIDTH_A = N_HEADS_A * DV_A
N_HEADS_B = 4
DK_B = 64
DV_B = 64
WIDTH_B = N_HEADS_B * DV_B
FN_GROUPS = 4
FN_GROUP_DIM = 64
WIDTH_C = FN_GROUPS * FN_GROUP_DIM
PROJ_WIDTH = 3 * WIDTH_A + 5 * WIDTH_B + WIDTH_C
D_FF = 2816
ROPE_BASE = 10000.0
EPS = 1e-6
LOG2E = math.log2(math.e)
exp2 = jnp.exp2
HG_UNIT = LOG2E

V7X_LANES = 128
V7X_BF16_SUBLANES = 16
V7X_VMEM_BYTES = 64 * 1024 * 1024
VMEM_LIMIT = V7X_VMEM_BYTES - 8 * 1024 * 1024

HG_CHUNK = 64
HG_LEVELS = 6
HG_WIDTHS = tuple(HG_CHUNK >> (l + 1) for l in range(HG_LEVELS))
INPROJ_SUB = 256
FFN_COLS = 256
HALO = V7X_BF16_SUBLANES


def _params(sem):
    return pltpu.CompilerParams(dimension_semantics=sem, vmem_limit_bytes=VMEM_LIMIT)


def _const_spec(shape, index_map):
    return pl.BlockSpec(shape, index_map, pipeline_mode=pl.Buffered(1))


def _dot(a, b):
    return jnp.dot(a, b, preferred_element_type=F32)


def _dot_nt(a, b):
    return lax.dot_general(a, b, (((1,), (1,)), ((), ())), preferred_element_type=F32)


def _dot_tn(a, b):
    return lax.dot_general(a, b, (((0,), (0,)), ((), ())), preferred_element_type=F32)


def _split3(x):
    h = x.astype(BF16)
    r = x - h.astype(F32)
    m = r.astype(BF16)
    return h, m, (r - m.astype(F32)).astype(BF16)


def _split2(x):
    h = x.astype(BF16)
    return h, (x - h.astype(F32)).astype(BF16)


def _rms_scale(x):
    return lax.rsqrt(jnp.mean(x * x, axis=-1, keepdims=True) + EPS)


def _mod_kernel(c_ref, w_ref, b_ref, o_ref):
    c = c_ref[...]
    a = (c * jax.nn.sigmoid(c)).astype(BF16)
    o_ref[0] = _dot(a, w_ref[0].astype(BF16)) + b_ref[0]


def _modulation(cstack, w_ada, b_ada):
    depth, _, n = w_ada.shape
    tn = 1536
    rows = cstack.shape[0]
    return pl.pallas_call(
        _mod_kernel,
        grid=(depth, n // tn),
        in_specs=[
            pl.BlockSpec((rows, D_MODEL), lambda l, j: (0, 0)),
            pl.BlockSpec((1, D_MODEL, tn), lambda l, j: (l, 0, j)),
            pl.BlockSpec((1, 1, tn), lambda l, j: (l, 0, j)),
        ],
        out_specs=pl.BlockSpec((1, rows, tn), lambda l, j: (l, 0, j)),
        out_shape=jax.ShapeDtypeStruct((depth, rows, n), F32),
        compiler_params=_params(("arbitrary", "arbitrary")),
        name="modulation",
    )(cstack, w_ada, b_ada.reshape(depth, 1, n))


def _log_forget(z, log_lb, log1m_lb):
    ls = jnp.minimum(z, 0.0) - jnp.log(1.0 + jnp.exp(-jnp.abs(z)))
    b = log1m_lb + ls
    return jnp.maximum(log_lb, b) + jnp.log(1.0 + jnp.exp(-jnp.abs(log_lb - b)))


def _inproj_kernel(rope, x_ref, mod_ref, nw_ref, w_ref, lbc_ref, *rest):
    if rope:
        cos_ref, sa_ref, sb_ref, qa_ref, ka_ref, va_ref, hg_ref, uf_ref = rest
    else:
        qa_ref, ka_ref, va_ref, hg_ref, uf_ref = rest
    m = mod_ref[...]
    lbc = lbc_ref[...]
    c0 = 3 * WIDTH_A
    for r0 in range(0, x_ref.shape[0], INPROJ_SUB):
        rs = slice(r0, r0 + INPROJ_SUB)
        x = x_ref[rs, :]
        y = x * _rms_scale(x) * nw_ref[...]
        h = (y * (1.0 + m[1:2]) + m[0:1]).astype(BF16)

        def proj(col, n):
            return _dot(h, w_ref[:, col:col + n])

        for base, ref, scale in ((0, qa_ref, DH_A ** -0.5 * LOG2E), (WIDTH_A, ka_ref, None)):
            t = proj(base, WIDTH_A)
            for hd in range(N_HEADS_A):
                th = t[:, hd * DV_A:(hd + 1) * DV_A]
                if rope:
                    th = (th * cos_ref[rs, :]
                          + pltpu.roll(th, V7X_LANES - 16, 1) * sa_ref[rs, :]
                          + pltpu.roll(th, 16, 1) * sb_ref[rs, :])
                if scale is not None:
                    th = th * scale
                ref[rs, hd * DV_A:(hd + 1) * DV_A] = th.astype(BF16)
        va_ref[rs, :] = proj(2 * WIDTH_A, WIDTH_A).astype(BF16)
        hg_ref[rs, 0:WIDTH_B] = proj(c0, WIDTH_B)
        for dirn in range(2):
            z = proj(c0 + WIDTH_B * (1 + dirn), WIDTH_B)
            hg_ref[rs, WIDTH_B * (1 + dirn):WIDTH_B * (2 + dirn)] = _log_forget(
                z, lbc[2 * dirn:2 * dirn + 1], lbc[2 * dirn + 1:2 * dirn + 2])
        hg_ref[rs, 3 * WIDTH_B:5 * WIDTH_B] = proj(c0 + 3 * WIDTH_B, 2 * WIDTH_B)
        uf_ref[rs, :] = proj(c0 + 5 * WIDTH_B, WIDTH_C).astype(BF16)


def _inproj(x2d, mod4, nw, w_bf, lbc, rope_tabs, li, seg_len, mod_row, tm):
    rows = x2d.shape[0]
    tpb = seg_len // tm
    rope = rope_tabs is not None
    mod_map = (lambda i: (li, i // tpb, 0, 0)) if mod_row is None else (lambda i: (li, mod_row, 0, 0))
    in_specs = [
        pl.BlockSpec((tm, D_MODEL), lambda i: (i, 0)),
        pl.BlockSpec((None, None, 6, D_MODEL), mod_map),
        pl.BlockSpec((None, 1, D_MODEL), lambda i: (li, 0, 0)),
        _const_spec((None, D_MODEL, PROJ_WIDTH), lambda i: (li, 0, 0)),
        pl.BlockSpec((None, 4, WIDTH_B), lambda i: (li, 0, 0)),
    ]
    args = [x2d, mod4, nw, w_bf, lbc]
    if rope:
        in_specs += [pl.BlockSpec((tm, DV_A), lambda i: (i % tpb, 0))] * 3
        args += list(rope_tabs)
    out_shape = (
        jax.ShapeDtypeStruct((rows, WIDTH_A), BF16),
        jax.ShapeDtypeStruct((rows, WIDTH_A), BF16),
        jax.ShapeDtypeStruct((rows, WIDTH_A), BF16),
        jax.ShapeDtypeStruct((rows, 5 * WIDTH_B), F32),
        jax.ShapeDtypeStruct((rows, WIDTH_C), BF16),
    )
    out_specs = (
        pl.BlockSpec((tm, WIDTH_A), lambda i: (i, 0)),
        pl.BlockSpec((tm, WIDTH_A), lambda i: (i, 0)),
        pl.BlockSpec((tm, WIDTH_A), lambda i: (i, 0)),
        pl.BlockSpec((tm, 5 * WIDTH_B), lambda i: (i, 0)),
        pl.BlockSpec((tm, WIDTH_C), lambda i: (i, 0)),
    )
    return pl.pallas_call(
        functools.partial(_inproj_kernel, rope),
        grid=(rows // tm,),
        in_specs=in_specs,
        out_specs=out_specs,
        out_shape=out_shape,
        compiler_params=_params(("parallel",)),
        name="inproj_rope" if rope else "inproj",
    )(*args)


def _attn_kernel(nseg, hps, tq, sub, out_scale, lam_ref, q_ref, *rest):
    k_refs = rest[:nseg]
    v_refs = rest[nseg:2 * nseg]
    w_ref, o_ref = rest[2 * nseg:]
    lam = lam_ref[0, 0]
    wn = w_ref[...] * out_scale
    lane = lax.broadcasted_iota(jnp.int32, (sub, DV_A), 1)
    for hh in range(hps):
        hl = slice(hh * DV_A, (hh + 1) * DV_A)
        vexts = []
        for v_ref in v_refs:
            v = v_ref[:, hl]
            vl = lax.broadcasted_iota(jnp.int32, v.shape, 1)
            vexts.append(jnp.concatenate([v, jnp.where(vl == 0, 1.0, 0.0).astype(BF16)], axis=1))
        for r0 in range(0, tq, sub):
            q = q_ref[r0:r0 + sub, hl].astype(F32)
            qs = jnp.concatenate([jnp.where(lane < DH_A, q, 0.0), jnp.where(lane >= DH_A, q, 0.0)],
                                 axis=0).astype(BF16)
            scores = [_dot_nt(qs, k_ref[:, hl]) for k_ref in k_refs]
            mx = jnp.max(scores[0], axis=-1, keepdims=True)
            for s in scores[1:]:
                mx = jnp.maximum(mx, jnp.max(s, axis=-1, keepdims=True))
            acc = sum(_dot(exp2(s - mx).astype(BF16), vext) for s, vext in zip(scores, vexts))
            r = acc[:, :DV_A] / acc[:, DV_A:DV_A + 1]
            o = r[:sub] - lam * r[sub:]
            o_ref[r0:r0 + sub, hl] = (o * _rms_scale(o) * wn).astype(BF16)


def _attention(lam, q, ks, vs, subln_w, li, out_scale, hps, tq, sub):
    b, lq, _ = q.shape
    nseg = len(ks)
    w = hps * DV_A
    in_specs = [
        pl.BlockSpec(memory_space=pltpu.SMEM),
        pl.BlockSpec((None, tq, w), lambda bi, h, i: (bi, i, h)),
    ]
    for arr in list(ks) + list(vs):
        in_specs.append(pl.BlockSpec((None, arr.shape[1], w), lambda bi, h, i: (bi, 0, h)))
    in_specs.append(pl.BlockSpec((None, 1, DV_A), lambda bi, h, i: (li, 0, 0)))
    return pl.pallas_call(
        functools.partial(_attn_kernel, nseg, hps, tq, sub, out_scale),
        grid=(b, N_HEADS_A // hps, lq // tq),
        in_specs=in_specs,
        out_specs=pl.BlockSpec((None, tq, w), lambda bi, h, i: (bi, i, h)),
        out_shape=jax.ShapeDtypeStruct((b, lq, WIDTH_A), BF16),
        compiler_params=_params(("parallel", "parallel", "arbitrary")),
        name="diff_attention_%dseg" % nseg,
    )(lam, q, *ks, *vs, subln_w)


def _hgrn_constants():
    c = HG_CHUNK
    t = np.arange(c)[:, None]
    r = np.arange(c)[None, :]
    amask = []
    lomask = []
    for w in HG_WIDTHS:
        hi = (t % (2 * w)) >= w
        same_pair = (t // (2 * w)) == (r // (2 * w))
        amask.append(same_pair & (hi != ((r % (2 * w)) >= w)))
        lomask.append(np.broadcast_to(~hi, (c, c)))
    psum = np.zeros((2 * c, 2 * c), np.float32)
    psum[:c, :c] = r <= t
    psum[c:, c:] = r >= t
    tile = lambda a: np.tile(np.asarray(a, np.float32), (1, N_HEADS_B))
    amask = np.stack([tile(a) for a in amask])
    lomask = np.stack([tile(a) for a in lomask])
    hd = np.arange(WIDTH_B) // DK_B
    bd = (hd[:, None] == hd[None, :]).astype(np.float32)
    return (jnp.asarray(psum, BF16), jnp.asarray(amask, F32), jnp.asarray(lomask, F32), jnp.asarray(bd, F32))


def _hgrn_kernel(nc, q_ref, gf_ref, gb_ref, v_ref, og_ref, psum_ref, amask_ref, lomask_ref,
                 bd_ref, hw_ref, sf0_ref, sb0_ref, rec_ref, sf_ref, sb_ref, sfp_ref, sbn_ref, cf_ref, cbs_ref,
                 qa_ref, ka_ref, qb_ref, kb_ref, aa_ref, pa_ref, ab_ref, pb_ref, oa_ref, ob_ref):
    c = HG_CHUNK
    bd = bd_ref[...]
    bd_bf = bd.astype(BF16)
    lane_head = lax.broadcasted_iota(jnp.int32, (c, WIDTH_B), 1) // DK_B

    def stack_heads(a):
        return jnp.concatenate([jnp.where(lane_head == h, a, 0.0).astype(BF16) for h in range(N_HEADS_B)],
                               axis=0)

    def rows(ref, n):
        if isinstance(n, int):
            return ref[n * c:(n + 1) * c, :]
        return ref[pl.ds(pl.multiple_of(n * c, c), c), :]

    sf_ref[...] = sf0_ref[...]
    sb_ref[...] = sb0_ref[...]

    def decayed_keys(i, kt_ref, dec_ref):
        nf = i
        nb = nc - 1 - i
        gf = rows(gf_ref, nf)
        gb = rows(gb_ref, nb)
        g = jnp.concatenate([gf, gb], axis=0)
        sums = sum(_dot(psum_ref[...], part) for part in _split3(g))
        cf = sums[:c]
        cbs = sums[c:]
        cf_ref[nf] = cf * HG_UNIT
        cbs_ref[nb] = cbs * HG_UNIT
        tot_f = cf[c - 1:c]
        tot_b = cbs[0:1]
        kt_ref[0] = (1.0 - jnp.exp(gf)) * jnp.exp(tot_f - cf)
        kt_ref[1] = (1.0 - jnp.exp(gb)) * jnp.exp(tot_b - cbs)
        dec_ref[0:1, :] = jnp.exp(tot_f)
        dec_ref[8:9, :] = jnp.exp(tot_b)

    def apply_step(i, kt_ref, dec_ref):
        nf = i
        nb = nc - 1 - i
        upd_f = _dot_tn(rows(v_ref, nf).astype(BF16), kt_ref[0].astype(BF16)) * bd
        upd_b = _dot_tn(rows(v_ref, nb).astype(BF16), kt_ref[1].astype(BF16)) * bd
        stf = sf_ref[...]
        sfp_ref[nf] = stf.astype(BF16)
        sf_ref[...] = stf * dec_ref[0:1, :] + upd_f
        stb = sb_ref[...]
        sbn_ref[nb] = stb.astype(BF16)
        sb_ref[...] = stb * dec_ref[8:9, :] + upd_b

    pre = ((qa_ref, oa_ref), (qb_ref, ob_ref))
    decayed_keys(0, *pre[0])

    def pre_body(j, carry):
        for par in (0, 1):
            i = 2 * j + par
            apply_step(i, *pre[par])
            decayed_keys(jnp.minimum(i + 1, nc - 1), *pre[1 - par])
        return carry

    lax.fori_loop(0, nc // 2, pre_body, 0)

    def prepare(n, qop_ref, kop_ref):
        q = rows(q_ref, n)
        gf = rows(gf_ref, n) * HG_UNIT
        gb = rows(gb_ref, n) * HG_UNIT
        cf = cf_ref[n]
        cbs = cbs_ref[n]
        kf = 1.0 - exp2(gf)
        kb = 1.0 - exp2(gb)
        for lvl, w in enumerate(HG_WIDTHS):
            lo = lomask_ref[lvl] > 0.5
            if 2 * w >= 8:
                anchor = lambda a, r: jnp.broadcast_to(
                    a.reshape(c // (2 * w), 2 * w, WIDTH_B)[:, r:r + 1, :],
                    (c // (2 * w), 2 * w, WIDTH_B)).reshape(c, WIDTH_B)
                d_f = cf - anchor(cf, w - 1)
                d_b = cbs - anchor(cbs, w)
                eq = exp2(jnp.where(lo, d_b, d_f))
                ek = exp2(-jnp.where(lo, d_f, d_b))
            elif w == 2:
                r4 = lax.broadcasted_iota(jnp.int32, (c, WIDTH_B), 0) & 3
                gf_up, gf_dn = pltpu.roll(gf, c - 1, 0), pltpu.roll(gf, 1, 0)
                gb_up, gb_dn = pltpu.roll(gb, c - 1, 0), pltpu.roll(gb, 1, 0)
                eq = exp2(jnp.where(r4 == 0, gb + gb_up, jnp.where(r4 == 1, gb,
                              jnp.where(r4 == 2, gf, gf + gf_dn))))
                ek = exp2(jnp.where(r4 == 0, gf_up, jnp.where(r4 == 3, gb_dn, 0.0)))
            else:
                eq = exp2(jnp.where(lo, gb, gf))
                ek = None
            klev = jnp.where(lo, kf, kb)
            qop_ref[lvl] = q * eq
            kop_ref[lvl] = klev if ek is None else klev * ek
        qop_ref[HG_LEVELS] = q * exp2(cf)
        qop_ref[HG_LEVELS + 1] = q * exp2(cbs)
        qop_ref[HG_LEVELS + 2] = q * (kf + kb)

    def scores(n, qop_ref, kop_ref, a_ref, part_ref):
        a_all = jnp.zeros((c, N_HEADS_B * c), F32)
        for lvl in range(HG_LEVELS):
            a_all = a_all + _dot_nt(qop_ref[lvl].astype(BF16), stack_heads(kop_ref[lvl])) * amask_ref[lvl]
        a_ref[...] = a_all
        part = _dot(qop_ref[HG_LEVELS + 2].astype(BF16), bd_bf) * rows(v_ref, n)
        part = part + _dot_nt(qop_ref[HG_LEVELS].astype(BF16), sfp_ref[n])
        part_ref[...] = part + _dot_nt(qop_ref[HG_LEVELS + 1].astype(BF16), sbn_ref[n])

    def values(n, a_ref, part_ref, o_ref):
        o_ref[...] = (_dot(a_ref[...].astype(BF16), stack_heads(rows(v_ref, n)))
                      + part_ref[...])

    def finish(n, o_ref):
        o = o_ref[...]
        sq_h, sq_l = _split2(o * o)
        ms = _dot(jnp.concatenate([sq_h, sq_l], axis=0), bd_bf)
        ms = (ms[:c] + ms[c:]) * (1.0 / DV_B)
        og = rows(og_ref, n)
        rec = o * lax.rsqrt(ms + EPS) * hw_ref[...] * (og * jax.nn.sigmoid(og))
        rec_ref[pl.ds(pl.multiple_of(n * c, c), c), :] = rec.astype(BF16)

    ops = ((qa_ref, ka_ref), (qb_ref, kb_ref))
    sc = ((aa_ref, pa_ref), (ab_ref, pb_ref))
    ob = (oa_ref, ob_ref)
    last = nc - 1

    def step(m, par, clamp):
        idx = (lambda i: jnp.minimum(i, last)) if clamp else (lambda i: i)
        if clamp or m >= 0:
            finish(m, ob[par])
        if clamp or m + 1 >= 0:
            values(idx(m + 1), *sc[1 - par], ob[1 - par])
        if clamp or m + 2 >= 0:
            scores(idx(m + 2), *ops[par], *sc[par])
        prepare(idx(m + 3), *ops[1 - par])

    for m in (-3, -2, -1):
        step(m, m % 2, False)

    def body(j, carry):
        step(2 * j, 0, True)
        step(2 * j + 1, 1, True)
        return carry

    lax.fori_loop(0, nc // 2, body, 0)


def _hgrn(hg, hconst, hw, sf0, sb0, li):
    b, l, _ = hg.shape
    nc = l // HG_CHUNK
    psum, amask, lomask, bd = hconst
    col = lambda j: pl.BlockSpec((None, l, WIDTH_B), lambda bi: (bi, 0, j))
    full = lambda a: _const_spec(a.shape, lambda bi: (0,) * a.ndim)
    st_spec = pl.BlockSpec((None, WIDTH_B, WIDTH_B), lambda bi: (bi, 0, 0))
    state_scratch = pltpu.VMEM((nc, WIDTH_B, WIDTH_B), BF16)
    sums_scratch = pltpu.VMEM((nc, HG_CHUNK, WIDTH_B), F32)
    q_operands = pltpu.VMEM((HG_LEVELS + 3, HG_CHUNK, WIDTH_B), F32)
    k_operands = pltpu.VMEM((HG_LEVELS, HG_CHUNK, WIDTH_B), F32)
    chunk_f32 = pltpu.VMEM((HG_CHUNK, WIDTH_B), F32)
    return pl.pallas_call(
        functools.partial(_hgrn_kernel, nc),
        grid=(b,),
        in_specs=[col(0), col(1), col(2), col(3), col(4), full(psum), full(amask),
                  full(lomask), full(bd), pl.BlockSpec((None, 1, WIDTH_B), lambda bi: (li, 0, 0)),
                  st_spec, st_spec],
        out_specs=(pl.BlockSpec((None, l, WIDTH_B), lambda bi: (bi, 0, 0)), st_spec, st_spec),
        out_shape=(jax.ShapeDtypeStruct((b, l, WIDTH_B), BF16),
                   jax.ShapeDtypeStruct((b, WIDTH_B, WIDTH_B), F32),
                   jax.ShapeDtypeStruct((b, WIDTH_B, WIDTH_B), F32)),
        scratch_shapes=[state_scratch, state_scratch, sums_scratch, sums_scratch,
                        q_operands, k_operands, q_operands, k_operands,
                        chunk_f32, chunk_f32, chunk_f32, chunk_f32, chunk_f32, chunk_f32],
        compiler_params=_params(("parallel",)),
        name="hgrn_l%d" % l,
    )(hg, hg, hg, hg, hg, psum, amask, lomask, bd, hw, sf0, sb0)


def _dft_constants(l):
    k = np.arange(FN_GROUP_DIM)
    ang = 2.0 * np.pi * ((k[:, None] * k[None, :]) % FN_GROUP_DIM) / FN_GROUP_DIM
    eye = np.eye(FN_GROUPS)
    cs = np.concatenate([np.kron(eye, np.cos(ang)), np.kron(eye, np.sin(ang))], axis=1)
    n = np.arange(l)
    angl = 2.0 * np.pi * ((n[:, None] * n[None, :]) % l) / l
    return tuple(jnp.asarray(a, F32).astype(BF16) for a in (cs, np.cos(angl), np.sin(angl)))


def _fnet_kernel(scale, u_ref, cs_ref, cl_ref, sl_ref, wf_ref, o_ref):
    t = _dot(u_ref[...], cs_ref[...])
    uc = t[:, :WIDTH_C].astype(BF16)
    us = t[:, WIDTH_C:].astype(BF16)
    y = (_dot(cl_ref[...], uc) - _dot(sl_ref[...], us)) * scale
    o_ref[...] = _dot(y.astype(BF16), wf_ref[...]).astype(BF16)


def _fnet(uf, dft, wf_bf, li):
    b, l, _ = uf.shape
    cs, cl, sl = dft
    scale = 1.0 / math.sqrt(l * FN_GROUP_DIM)
    return pl.pallas_call(
        functools.partial(_fnet_kernel, scale),
        grid=(b,),
        in_specs=[
            pl.BlockSpec((None, l, WIDTH_C), lambda bi: (bi, 0, 0)),
            _const_spec(cs.shape, lambda bi: (0, 0)),
            _const_spec(cl.shape, lambda bi: (0, 0)),
            _const_spec(sl.shape, lambda bi: (0, 0)),
            _const_spec((None, WIDTH_C, WIDTH_C), lambda bi: (li, 0, 0)),
        ],
        out_specs=pl.BlockSpec((None, l, WIDTH_C), lambda bi: (bi, 0, 0)),
        out_shape=jax.ShapeDtypeStruct((b, l, WIDTH_C), BF16),
        compiler_params=_params(("parallel",)),
        name="fnet_l%d" % l,
    )(uf, cs, cl, sl, wf_bf)


def _mix_ffn_kernel(tm, tpb, final, att_ref, attp_ref, attn_ref, rec_ref, recp_ref, recn_ref, four_ref,
                    fourp_ref, fourn_ref, x_ref, xp_ref, xn_ref, mod_ref, nw_ref, wo_ref, wup_ref, cw_ref,
                    cb_ref, wdn_ref, fw_ref, o_ref, a_ref):
    t = pl.program_id(0) % tpb
    ext = tm + 2 * HALO
    m = mod_ref[...]

    def with_halo(prev_ref, main_ref, next_ref):
        return jnp.concatenate([prev_ref[...], main_ref[...], next_ref[...]], axis=0)

    mix = jnp.concatenate([with_halo(attp_ref, att_ref, attn_ref), with_halo(recp_ref, rec_ref, recn_ref),
                           with_halo(fourp_ref, four_ref, fourn_ref)], axis=1)
    x1e = with_halo(xp_ref, x_ref, xn_ref) + m[2:3] * _dot(mix, wo_ref[...])
    h = (x1e * _rms_scale(x1e) * nw_ref[...]) * (1.0 + m[4:5]) + m[3:4]
    row = lax.broadcasted_iota(jnp.int32, (ext, 1), 0)
    outside = ((row < HALO) & (t == 0)) | ((row >= HALO + tm) & (t == tpb - 1))
    hext = jnp.where(outside, 0.0, h).astype(BF16)

    def conv(u, c0):
        w = cw_ref[:, c0:c0 + FFN_COLS]
        up = pltpu.roll(u, 1, 0)
        dn = pltpu.roll(u, ext - 1, 0)
        r = up * w[0:1] + u * w[1:2] + dn * w[2:3] + cb_ref[:, c0:c0 + FFN_COLS]
        return r[HALO:HALO + tm]

    for j in range(D_FF // FFN_COLS):
        c0 = j * FFN_COLS
        gate = conv(_dot(hext, wup_ref[:, c0:c0 + FFN_COLS]), c0)
        val = conv(_dot(hext, wup_ref[:, D_FF + c0:D_FF + c0 + FFN_COLS]), D_FF + c0)
        a_ref[:, c0:c0 + FFN_COLS] = (gate * jax.nn.sigmoid(gate) * val).astype(BF16)
    x2 = x1e[HALO:HALO + tm] + m[5:6] * _dot(a_ref[...], wdn_ref[...])
    if final:
        x2 = x2 * _rms_scale(x2) * fw_ref[...]
    o_ref[...] = x2


def _mix_ffn(att, rec, four, x2d, mod4, nw2, wo_bf, wup_bf, conv_w, conv_b, wdn_bf, final_w, li, seg_len,
             mod_row, tm, final):
    rows = x2d.shape[0]
    tpb = seg_len // tm
    hb = tm // HALO
    nhb = rows // HALO
    mod_map = (lambda i: (li, i // tpb, 0, 0)) if mod_row is None else (lambda i: (li, mod_row, 0, 0))

    def row_specs(w):
        return [pl.BlockSpec((tm, w), lambda i: (i, 0)),
                pl.BlockSpec((HALO, w), lambda i: (jnp.maximum(i * hb - 1, 0), 0)),
                pl.BlockSpec((HALO, w), lambda i: (jnp.minimum((i + 1) * hb, nhb - 1), 0))]

    return pl.pallas_call(
        functools.partial(_mix_ffn_kernel, tm, tpb, final),
        grid=(rows // tm,),
        in_specs=row_specs(WIDTH_A) + row_specs(WIDTH_B) + row_specs(WIDTH_C) + row_specs(D_MODEL) + [
            pl.BlockSpec((None, None, 6, D_MODEL), mod_map),
            pl.BlockSpec((None, 1, D_MODEL), lambda i: (li, 0, 0)),
            _const_spec((None, D_MODEL, D_MODEL), lambda i: (li, 0, 0)),
            _const_spec((None, D_MODEL, 2 * D_FF), lambda i: (li, 0, 0)),
            pl.BlockSpec((None, 3, 2 * D_FF), lambda i: (li, 0, 0)),
            pl.BlockSpec((None, 1, 2 * D_FF), lambda i: (li, 0, 0)),
            _const_spec((None, D_FF, D_MODEL), lambda i: (li, 0, 0)),
            pl.BlockSpec((1, D_MODEL), lambda i: (0, 0)),
        ],
        out_specs=pl.BlockSpec((tm, D_MODEL), lambda i: (i, 0)),
        out_shape=jax.ShapeDtypeStruct((rows, D_MODEL), F32),
        scratch_shapes=[pltpu.VMEM((tm, D_FF), BF16)],
        compiler_params=_params(("parallel",)),
        name="mix_convffn_final" if final else "mix_convffn",
    )(att, att, att, rec, rec, rec, four, four, four, x2d, x2d, x2d, mod4, nw2, wo_bf, wup_bf, conv_w, conv_b,
      wdn_bf, final_w)


def _rope_tables(l):
    rows = l // GRID_W
    pos_r = np.repeat(np.arange(rows), GRID_W).astype(np.float32)
    pos_c = np.tile(np.arange(GRID_W), rows).astype(np.float32)
    half = DH_A // 2
    inv_freq = (ROPE_BASE ** (-np.arange(0, half, 2, dtype=np.float32) / half)).astype(np.float32)
    ang = np.concatenate([pos_r[:, None] * inv_freq] * 2 + [pos_c[:, None] * inv_freq] * 2, axis=-1)
    cos = np.cos(ang.astype(np.float32)).astype(np.float32)
    sin = np.sin(ang.astype(np.float32)).astype(np.float32)
    cos = np.tile(cos, (1, 2))
    sin = np.tile(sin, (1, 2))
    first = (np.arange(DV_A) % 32) < 16
    sin_a = np.where(first, -sin, 0.0).astype(np.float32)
    sin_b = np.where(first, 0.0, sin).astype(np.float32)
    return jnp.asarray(cos), jnp.asarray(sin_a), jnp.asarray(sin_b)


def kernel(x, c, ctx, c_ctx, w_ada, b_ada, norm1_w, norm2_w, w_in, lam_qk, subln_w, lb_param, hgrn_norm_w,
           w_fnet, w_out, w_up, conv_w, conv_b, w_down, final_norm_w):
    b, l, d = x.shape
    lc = ctx.shape[1]
    depth = w_ada.shape[0]
    assert d == D_MODEL and l % GRID_W == 0 and b <= 8

    lb_all = jnp.cumsum(jax.nn.softmax(lb_param.astype(F32), axis=0), axis=0)
    lb_all = lb_all - lb_all[0:1]
    lbc = jnp.stack([jnp.log(lb_all[:, 0]), jnp.log1p(-lb_all[:, 0]),
                     jnp.log(lb_all[:, 1]), jnp.log1p(-lb_all[:, 1])], axis=1)
    lq = lam_qk.astype(F32)
    lam_dyn = jnp.exp(jnp.sum(lq[:, 0] * lq[:, 1], axis=-1)) - jnp.exp(jnp.sum(lq[:, 2] * lq[:, 3], axis=-1))
    rope_tabs = _rope_tables(l)
    hconst = _hgrn_constants()
    dft_l = _dft_constants(l)
    dft_c = _dft_constants(lc)
    w_in_bf = w_in.astype(BF16)
    w_out_bf = w_out.astype(BF16)
    w_up_bf = w_up.astype(BF16)
    w_dn_bf = w_down.astype(BF16)
    w_fn_bf = w_fnet.astype(BF16)
    nw1 = norm1_w.reshape(depth, 1, d)
    nw2 = norm2_w.reshape(depth, 1, d)
    sub_w = subln_w.reshape(depth, 1, DV_A)
    hw = jnp.tile(hgrn_norm_w, (1, N_HEADS_B)).reshape(depth, 1, WIDTH_B)
    cb = conv_b.reshape(depth, 1, 2 * D_FF)
    fw = final_norm_w.reshape(1, d)

    cstack = jnp.concatenate([c, c_ctx[None, :], jnp.zeros((16 - b - 1, d), F32)], axis=0)
    mod4 = _modulation(cstack, w_ada, b_ada).reshape(depth, 16, 6, d)
    ctx_row = b

    zeros_state = jnp.zeros((b, WIDTH_B, WIDTH_B), F32)
    xl = x.reshape(b * l, d)
    xc = ctx.reshape(b * lc, d)
    tm_l = 1024
    tm_f = 512
    tm_c = 256
    for li in range(depth):
        last = li == depth - 1
        lam_init = 0.8 - 0.6 * math.exp(-0.3 * li)
        lam = (lam_dyn[li] + lam_init).reshape(1, 1)

        qa, ka, va, hg, uf = _inproj(xl, mod4, nw1, w_in_bf, lbc, rope_tabs, li, l, None, tm_l)
        qc, kc, vc, hgc, ufc = _inproj(xc, mod4, nw1, w_in_bf, lbc, None, li, lc, ctx_row, tm_c)
        r3 = lambda a, n: a.reshape(b, n, a.shape[-1])

        att = _attention(lam, r3(qa, l), [r3(kc, lc), r3(ka, l)], [r3(vc, lc), r3(va, l)], sub_w, li,
                         1.0 - lam_init, 1, l, 128)
        rec_c, sf, sb = _hgrn(r3(hgc, lc), hconst, hw, zeros_state, zeros_state, li)
        rec, _, _ = _hgrn(r3(hg, l), hconst, hw, sf, sb, li)
        four = _fnet(r3(uf, l), dft_l, w_fn_bf, li)

        xl = _mix_ffn(att.reshape(b * l, -1), rec.reshape(b * l, -1), four.reshape(b * l, -1), xl, mod4, nw2,
                      w_out_bf, w_up_bf, conv_w, cb, w_dn_bf, fw, li, l, None, tm_f, last)

        if not last:
            att_c = _attention(lam, r3(qc, lc), [r3(kc, lc)], [r3(vc, lc)], sub_w, li, 1.0 - lam_init,
                               1, lc, 128)
            four_c = _fnet(r3(ufc, lc), dft_c, w_fn_bf, li)
            xc = _mix_ffn(att_c.reshape(b * lc, -1), rec_c.reshape(b * lc, -1), four_c.reshape(b * lc, -1), xc,
                          mod4, nw2, w_out_bf, w_up_bf, conv_w, cb, w_dn_bf, fw, li, lc, ctx_row, tm_c, False)
    return xl.reshape(b, l, d)
```

```python
import functools
import math

import numpy as np
import jax
import jax.numpy as jnp
from jax import lax
from jax.experimental import pallas as pl
from jax.experimental.pallas import tpu as pltpu

F32 = jnp.float32
BF16 = jnp.bfloat16

D_MODEL = 1024
GRID_W = 64
N_HEADS_A = 4
DH_A = 64
DV_A = 2 * DH_A
WIDTH_A = N_HEADS_A * DV_A
N_HEADS_B = 4
DK_B = 64
DV_B = 64
WIDTH_B = N_HEADS_B * DV_B
FN_GROUPS = 4
FN_GROUP_DIM = 64
WIDTH_C = FN_GROUPS * FN_GROUP_DIM
PROJ_WIDTH = 3 * WIDTH_A + 5 * WIDTH_B + WIDTH_C
D_FF = 2816
ROPE_BASE = 10000.0
EPS = 1e-6
LOG2E = math.log2(math.e)
exp2 = jnp.exp2
HG_UNIT = LOG2E

V7X_LANES = 128
V7X_BF16_SUBLANES = 16
V7X_VMEM_BYTES = 64 * 1024 * 1024
VMEM_LIMIT = V7X_VMEM_BYTES - 8 * 1024 * 1024

HG_CHUNK = 64
HG_LEVELS = 6
HG_WIDTHS = tuple(HG_CHUNK >> (l + 1) for l in range(HG_LEVELS))
INPROJ_SUB = 256
FFN_COLS = 256
HALO = V7X_BF16_SUBLANES


def _params(sem):
    return pltpu.CompilerParams(dimension_semantics=sem, vmem_limit_bytes=VMEM_LIMIT)


def _const_spec(shape, index_map):
    return pl.BlockSpec(shape, index_map, pipeline_mode=pl.Buffered(1))


def _dot(a, b):
    return jnp.dot(a, b, preferred_element_type=F32)


def _dot_nt(a, b):
    return lax.dot_general(a, b, (((1,), (1,)), ((), ())), preferred_element_type=F32)


def _dot_tn(a, b):
    return lax.dot_general(a, b, (((0,), (0,)), ((), ())), preferred_element_type=F32)


def _split3(x):
    h = x.astype(BF16)
    r = x - h.astype(F32)
    m = r.astype(BF16)
    return h, m, (r - m.astype(F32)).astype(BF16)


def _split2(x):
    h = x.astype(BF16)
    return h, (x - h.astype(F32)).astype(BF16)


def _rms_scale(x):
    return lax.rsqrt(jnp.mean(x * x, axis=-1, keepdims=True) + EPS)


def _mod_kernel(c_ref, w_ref, b_ref, o_ref):
    c = c_ref[...]
    a = (c * jax.nn.sigmoid(c)).astype(BF16)
    o_ref[0] = _dot(a, w_ref[0].astype(BF16)) + b_ref[0]


def _modulation(cstack, w_ada, b_ada):
    depth, _, n = w_ada.shape
    tn = 1536
    rows = cstack.shape[0]
    return pl.pallas_call(
        _mod_kernel,
        grid=(depth, n // tn),
        in_specs=[
            pl.BlockSpec((rows, D_MODEL), lambda l, j: (0, 0)),
            pl.BlockSpec((1, D_MODEL, tn), lambda l, j: (l, 0, j)),
            pl.BlockSpec((1, 1, tn), lambda l, j: (l, 0, j)),
        ],
        out_specs=pl.BlockSpec((1, rows, tn), lambda l, j: (l, 0, j)),
        out_shape=jax.ShapeDtypeStruct((depth, rows, n), F32),
        compiler_params=_params(("arbitrary", "arbitrary")),
        name="modulation",
    )(cstack, w_ada, b_ada.reshape(depth, 1, n))


def _log_forget(z, log_lb, log1m_lb):
    ls = jnp.minimum(z, 0.0) - jnp.log(1.0 + jnp.exp(-jnp.abs(z)))
    b = log1m_lb + ls
    return jnp.maximum(log_lb, b) + jnp.log(1.0 + jnp.exp(-jnp.abs(log_lb - b)))


def _inproj_kernel(rope, x_ref, mod_ref, nw_ref, w_ref, lbc_ref, *rest):
    if rope:
        cos_ref, sa_ref, sb_ref, qa_ref, ka_ref, va_ref, hg_ref, uf_ref = rest
    else:
        qa_ref, ka_ref, va_ref, hg_ref, uf_ref = rest
    m = mod_ref[...]
    lbc = lbc_ref[...]
    c0 = 3 * WIDTH_A
    for r0 in range(0, x_ref.shape[0], INPROJ_SUB):
        rs = slice(r0, r0 + INPROJ_SUB)
        x = x_ref[rs, :]
        y = x * _rms_scale(x) * nw_ref[...]
        h = (y * (1.0 + m[1:2]) + m[0:1]).astype(BF16)

        def proj(col, n):
            return _dot(h, w_ref[:, col:col + n])

        for base, ref, scale in ((0, qa_ref, DH_A ** -0.5 * LOG2E), (WIDTH_A, ka_ref, None)):
            t = proj(base, WIDTH_A)
            for hd in range(N_HEADS_A):
                th = t[:, hd * DV_A:(hd + 1) * DV_A]
                if rope:
                    th = (th * cos_ref[rs, :]
                          + pltpu.roll(th, V7X_LANES - 16, 1) * sa_ref[rs, :]
                          + pltpu.roll(th, 16, 1) * sb_ref[rs, :])
                if scale is not None:
                    th = th * scale
                ref[rs, hd * DV_A:(hd + 1) * DV_A] = th.astype(BF16)
        va_ref[rs, :] = proj(2 * WIDTH_A, WIDTH_A).astype(BF16)
        hg_ref[rs, 0:WIDTH_B] = proj(c0, WIDTH_B)
        for dirn in range(2):
            z = proj(c0 + WIDTH_B * (1 + dirn), WIDTH_B)
            hg_ref[rs, WIDTH_B * (1 + dirn):WIDTH_B * (2 + dirn)] = _log_forget(
                z, lbc[2 * dirn:2 * dirn + 1], lbc[2 * dirn + 1:2 * dirn + 2])
        hg_ref[rs, 3 * WIDTH_B:5 * WIDTH_B] = proj(c0 + 3 * WIDTH_B, 2 * WIDTH_B)
        uf_ref[rs, :] = proj(c0 + 5 * WIDTH_B, WIDTH_C).astype(BF16)


def _inproj(x2d, mod4, nw, w_bf, lbc, rope_tabs, li, seg_len, mod_row, tm):
    rows = x2d.shape[0]
    tpb = seg_len // tm
    rope = rope_tabs is not None
    mod_map = (lambda i: (li, i // tpb, 0, 0)) if mod_row is None else (lambda i: (li, mod_row, 0, 0))
    in_specs = [
        pl.BlockSpec((tm, D_MODEL), lambda i: (i, 0)),
        pl.BlockSpec((None, None, 6, D_MODEL), mod_map),
        pl.BlockSpec((None, 1, D_MODEL), lambda i: (li, 0, 0)),
        _const_spec((None, D_MODEL, PROJ_WIDTH), lambda i: (li, 0, 0)),
        pl.BlockSpec((None, 4, WIDTH_B), lambda i: (li, 0, 0)),
    ]
    args = [x2d, mod4, nw, w_bf, lbc]
    if rope:
        in_specs += [pl.BlockSpec((tm, DV_A), lambda i: (i % tpb, 0))] * 3
        args += list(rope_tabs)
    out_shape = (
        jax.ShapeDtypeStruct((rows, WIDTH_A), BF16),
        jax.ShapeDtypeStruct((rows, WIDTH_A), BF16),
        jax.ShapeDtypeStruct((rows, WIDTH_A), BF16),
        jax.ShapeDtypeStruct((rows, 5 * WIDTH_B), F32),
        jax.ShapeDtypeStruct((rows, WIDTH_C), BF16),
    )
    out_specs = (
        pl.BlockSpec((tm, WIDTH_A), lambda i: (i, 0)),
        pl.BlockSpec((tm, WIDTH_A), lambda i: (i, 0)),
        pl.BlockSpec((tm, WIDTH_A), lambda i: (i, 0)),
        pl.BlockSpec((tm, 5 * WIDTH_B), lambda i: (i, 0)),
        pl.BlockSpec((tm, WIDTH_C), lambda i: (i, 0)),
    )
    return pl.pallas_call(
        functools.partial(_inproj_kernel, rope),
        grid=(rows // tm,),
        in_specs=in_specs,
        out_specs=out_specs,
        out_shape=out_shape,
        compiler_params=_params(("parallel",)),
        name="inproj_rope" if rope else "inproj",
    )(*args)


def _attn_kernel(nseg, hps, tq, sub, out_scale, lam_ref, q_ref, *rest):
    k_refs = rest[:nseg]
    v_refs = rest[nseg:2 * nseg]
    w_ref, o_ref = rest[2 * nseg:]
    lam = lam_ref[0, 0]
    wn = w_ref[...] * out_scale
    lane = lax.broadcasted_iota(jnp.int32, (sub, DV_A), 1)
    for hh in range(hps):
        hl = slice(hh * DV_A, (hh + 1) * DV_A)
        vexts = []
        for v_ref in v_refs:
            v = v_ref[:, hl]
            vl = lax.broadcasted_iota(jnp.int32, v.shape, 1)
            vexts.append(jnp.concatenate([v, jnp.where(vl == 0, 1.0, 0.0).astype(BF16)], axis=1))
        for r0 in range(0, tq, sub):
            q = q_ref[r0:r0 + sub, hl].astype(F32)
            qs = jnp.concatenate([jnp.where(lane < DH_A, q, 0.0), jnp.where(lane >= DH_A, q, 0.0)],
                                 axis=0).astype(BF16)
            scores = [_dot_nt(qs, k_ref[:, hl]) for k_ref in k_refs]
            mx = jnp.max(scores[0], axis=-1, keepdims=True)
            for s in scores[1:]:
                mx = jnp.maximum(mx, jnp.max(s, axis=-1, keepdims=True))
            acc = sum(_dot(exp2(s - mx).astype(BF16), vext) for s, vext in zip(scores, vexts))
            r = acc[:, :DV_A] / acc[:, DV_A:DV_A + 1]
            o = r[:sub] - lam * r[sub:]
            o_ref[r0:r0 + sub, hl] = (o * _rms_scale(o) * wn).astype(BF16)


def _attention(lam, q, ks, vs, subln_w, li, out_scale, hps, tq, sub):
    b, lq, _ = q.shape
    nseg = len(ks)
    w = hps * DV_A
    in_specs = [
        pl.BlockSpec(memory_space=pltpu.SMEM),
        pl.BlockSpec((None, tq, w), lambda bi, h, i: (bi, i, h)),
    ]
    for arr in list(ks) + list(vs):
        in_specs.append(pl.BlockSpec((None, arr.shape[1], w), lambda bi, h, i: (bi, 0, h)))
    in_specs.append(pl.BlockSpec((None, 1, DV_A), lambda bi, h, i: (li, 0, 0)))
    return pl.pallas_call(
        functools.partial(_attn_kernel, nseg, hps, tq, sub, out_scale),
        grid=(b, N_HEADS_A // hps, lq // tq),
        in_specs=in_specs,
        out_specs=pl.BlockSpec((None, tq, w), lambda bi, h, i: (bi, i, h)),
        out_shape=jax.ShapeDtypeStruct((b, lq, WIDTH_A), BF16),
        compiler_params=_params(("parallel", "parallel", "arbitrary")),
        name="diff_attention_%dseg" % nseg,
    )(lam, q, *ks, *vs, subln_w)


def _hgrn_constants():
    c = HG_CHUNK
    t = np.arange(c)[:, None]
    r = np.arange(c)[None, :]
    amask = []
    lomask = []
    for w in HG_WIDTHS:
        hi = (t % (2 * w)) >= w
        same_pair = (t // (2 * w)) == (r // (2 * w))
        amask.append(same_pair & (hi != ((r % (2 * w)) >= w)))
        lomask.append(np.broadcast_to(~hi, (c, c)))
    psum = np.zeros((2 * c, 2 * c), np.float32)
    psum[:c, :c] = r <= t
    psum[c:, c:] = r >= t
    tile = lambda a: np.tile(np.asarray(a, np.float32), (1, N_HEADS_B))
    amask = np.stack([tile(a) for a in amask])
    lomask = np.stack([tile(a) for a in lomask])
    hd = np.arange(WIDTH_B) // DK_B
    bd = (hd[:, None] == hd[None, :]).astype(np.float32)
    return (jnp.asarray(psum, BF16), jnp.asarray(amask, F32), jnp.asarray(lomask, F32), jnp.asarray(bd, F32))


def _hgrn_kernel(nc, states_only, q_ref, gf_ref, gb_ref, v_ref, og_ref, psum_ref, amask_ref, lomask_ref,
                 bd_ref, hw_ref, sf0_ref, sb0_ref, *outs_and_scratch):
    if states_only:
        rec_ref = None
        sf_ref, sb_ref = outs_and_scratch[:2]
        scratch = outs_and_scratch[2:]
    else:
        rec_ref, sf_ref, sb_ref = outs_and_scratch[:3]
        scratch = outs_and_scratch[3:]
    (sfp_ref, sbn_ref, cf_ref, cbs_ref, qa_ref, ka_ref, qb_ref, kb_ref, aa_ref, pa_ref, ab_ref, pb_ref, oa_ref,
     ob_ref) = scratch
    c = HG_CHUNK
    bd = bd_ref[...]
    bd_bf = bd.astype(BF16)
    lane_head = lax.broadcasted_iota(jnp.int32, (c, WIDTH_B), 1) // DK_B

    def stack_heads(a):
        return jnp.concatenate([jnp.where(lane_head == h, a, 0.0).astype(BF16) for h in range(N_HEADS_B)],
                               axis=0)

    def rows(ref, n):
        if isinstance(n, int):
            return ref[n * c:(n + 1) * c, :]
        return ref[pl.ds(pl.multiple_of(n * c, c), c), :]

    sf_ref[...] = sf0_ref[...]
    sb_ref[...] = sb0_ref[...]

    def decayed_keys(i, kt_ref, dec_ref):
        nf = i
        nb = nc - 1 - i
        gf = rows(gf_ref, nf)
        gb = rows(gb_ref, nb)
        g = jnp.concatenate([gf, gb], axis=0)
        sums = sum(_dot(psum_ref[...], part) for part in _split3(g))
        cf = sums[:c]
        cbs = sums[c:]
        cf_ref[nf] = cf * HG_UNIT
        cbs_ref[nb] = cbs * HG_UNIT
        tot_f = cf[c - 1:c]
        tot_b = cbs[0:1]
        kt_ref[0] = (1.0 - jnp.exp(gf)) * jnp.exp(tot_f - cf)
        kt_ref[1] = (1.0 - jnp.exp(gb)) * jnp.exp(tot_b - cbs)
        dec_ref[0:1, :] = jnp.exp(tot_f)
        dec_ref[8:9, :] = jnp.exp(tot_b)

    def apply_step(i, kt_ref, dec_ref):
        nf = i
        nb = nc - 1 - i
        upd_f = _dot_tn(rows(v_ref, nf).astype(BF16), kt_ref[0].astype(BF16)) * bd
        upd_b = _dot_tn(rows(v_ref, nb).astype(BF16), kt_ref[1].astype(BF16)) * bd
        stf = sf_ref[...]
        sfp_ref[nf] = stf.astype(BF16)
        sf_ref[...] = stf * dec_ref[0:1, :] + upd_f
        stb = sb_ref[...]
        sbn_ref[nb] = stb.astype(BF16)
        sb_ref[...] = stb * dec_ref[8:9, :] + upd_b

    pre = ((qa_ref, oa_ref), (qb_ref, ob_ref))
    decayed_keys(0, *pre[0])

    def pre_body(j, carry):
        for par in (0, 1):
            i = 2 * j + par
            apply_step(i, *pre[par])
            decayed_keys(jnp.minimum(i + 1, nc - 1), *pre[1 - par])
        return carry

    lax.fori_loop(0, nc // 2, pre_body, 0)
    if states_only:
        return

    def prepare(n, qop_ref, kop_ref):
        q = rows(q_ref, n)
        gf = rows(gf_ref, n) * HG_UNIT
        gb = rows(gb_ref, n) * HG_UNIT
        cf = cf_ref[n]
        cbs = cbs_ref[n]
        kf = 1.0 - exp2(gf)
        kb = 1.0 - exp2(gb)
        for lvl, w in enumerate(HG_WIDTHS):
            lo = lomask_ref[lvl] > 0.5
            if 2 * w >= 8:
                anchor = lambda a, r: jnp.broadcast_to(
                    a.reshape(c // (2 * w), 2 * w, WIDTH_B)[:, r:r + 1, :],
                    (c // (2 * w), 2 * w, WIDTH_B)).reshape(c, WIDTH_B)
                d_f = cf - anchor(cf, w - 1)
                d_b = cbs - anchor(cbs, w)
                eq = exp2(jnp.where(lo, d_b, d_f))
                ek = exp2(-jnp.where(lo, d_f, d_b))
            elif w == 2:
                r4 = lax.broadcasted_iota(jnp.int32, (c, WIDTH_B), 0) & 3
                gf_up, gf_dn = pltpu.roll(gf, c - 1, 0), pltpu.roll(gf, 1, 0)
                gb_up, gb_dn = pltpu.roll(gb, c - 1, 0), pltpu.roll(gb, 1, 0)
                eq = exp2(jnp.where(r4 == 0, gb + gb_up, jnp.where(r4 == 1, gb,
                              jnp.where(r4 == 2, gf, gf + gf_dn))))
                ek = exp2(jnp.where(r4 == 0, gf_up, jnp.where(r4 == 3, gb_dn, 0.0)))
            else:
                eq = exp2(jnp.where(lo, gb, gf))
                ek = None
            klev = jnp.where(lo, kf, kb)
            qop_ref[lvl] = q * eq
            kop_ref[lvl] = klev if ek is None else klev * ek
        qop_ref[HG_LEVELS] = q * exp2(cf)
        qop_ref[HG_LEVELS + 1] = q * exp2(cbs)
        qop_ref[HG_LEVELS + 2] = q * (kf + kb)

    def scores(n, qop_ref, kop_ref, a_ref, part_ref):
        a_all = jnp.zeros((c, N_HEADS_B * c), F32)
        for lvl in range(HG_LEVELS):
            a_all = a_all + _dot_nt(qop_ref[lvl].astype(BF16), stack_heads(kop_ref[lvl])) * amask_ref[lvl]
        a_ref[...] = a_all
        part = _dot(qop_ref[HG_LEVELS + 2].astype(BF16), bd_bf) * rows(v_ref, n)
        part = part + _dot_nt(qop_ref[HG_LEVELS].astype(BF16), sfp_ref[n])
        part_ref[...] = part + _dot_nt(qop_ref[HG_LEVELS + 1].astype(BF16), sbn_ref[n])

    def values(n, a_ref, part_ref, o_ref):
        o_ref[...] = (_dot(a_ref[...].astype(BF16), stack_heads(rows(v_ref, n)))
                      + part_ref[...])

    def finish(n, o_ref):
        o = o_ref[...]
        sq_h, sq_l = _split2(o * o)
        ms = _dot(jnp.concatenate([sq_h, sq_l], axis=0), bd_bf)
        ms = (ms[:c] + ms[c:]) * (1.0 / DV_B)
        og = rows(og_ref, n)
        rec = o * lax.rsqrt(ms + EPS) * hw_ref[...] * (og * jax.nn.sigmoid(og))
        rec_ref[pl.ds(pl.multiple_of(n * c, c), c), :] = rec.astype(BF16)

    ops = ((qa_ref, ka_ref), (qb_ref, kb_ref))
    sc = ((aa_ref, pa_ref), (ab_ref, pb_ref))
    ob = (oa_ref, ob_ref)
    last = nc - 1

    def step(m, par, clamp):
        idx = (lambda i: jnp.minimum(i, last)) if clamp else (lambda i: i)
        if clamp or m >= 0:
            finish(m, ob[par])
        if clamp or m + 1 >= 0:
            values(idx(m + 1), *sc[1 - par], ob[1 - par])
        if clamp or m + 2 >= 0:
            scores(idx(m + 2), *ops[par], *sc[par])
        prepare(idx(m + 3), *ops[1 - par])

    for m in (-3, -2, -1):
        step(m, m % 2, False)

    def body(j, carry):
        step(2 * j, 0, True)
        step(2 * j + 1, 1, True)
        return carry

    lax.fori_loop(0, nc // 2, body, 0)


def _hgrn(hg, hconst, hw, sf0, sb0, li, states_only=False):
    b, l, _ = hg.shape
    nc = l // HG_CHUNK
    psum, amask, lomask, bd = hconst
    col = lambda j: pl.BlockSpec((None, l, WIDTH_B), lambda bi: (bi, 0, j))
    full = lambda a: _const_spec(a.shape, lambda bi: (0,) * a.ndim)
    st_spec = pl.BlockSpec((None, WIDTH_B, WIDTH_B), lambda bi: (bi, 0, 0))
    state_scratch = pltpu.VMEM((nc, WIDTH_B, WIDTH_B), BF16)
    sums_scratch = pltpu.VMEM((nc, HG_CHUNK, WIDTH_B), F32)
    q_operands = pltpu.VMEM((HG_LEVELS + 3, HG_CHUNK, WIDTH_B), F32)
    k_operands = pltpu.VMEM((HG_LEVELS, HG_CHUNK, WIDTH_B), F32)
    chunk_f32 = pltpu.VMEM((HG_CHUNK, WIDTH_B), F32)
    out_specs = [pl.BlockSpec((None, l, WIDTH_B), lambda bi: (bi, 0, 0)), st_spec, st_spec]
    out_shape = [jax.ShapeDtypeStruct((b, l, WIDTH_B), BF16),
                 jax.ShapeDtypeStruct((b, WIDTH_B, WIDTH_B), F32),
                 jax.ShapeDtypeStruct((b, WIDTH_B, WIDTH_B), F32)]
    if states_only:
        out_specs, out_shape = out_specs[1:], out_shape[1:]
    outs = pl.pallas_call(
        functools.partial(_hgrn_kernel, nc, states_only),
        grid=(b,),
        in_specs=[col(0), col(1), col(2), col(3), col(4), full(psum), full(amask),
                  full(lomask), full(bd), pl.BlockSpec((None, 1, WIDTH_B), lambda bi: (li, 0, 0)),
                  st_spec, st_spec],
        out_specs=tuple(out_specs),
        out_shape=tuple(out_shape),
        scratch_shapes=[state_scratch, state_scratch, sums_scratch, sums_scratch,
                        q_operands, k_operands, q_operands, k_operands,
                        chunk_f32, chunk_f32, chunk_f32, chunk_f32, chunk_f32, chunk_f32],
        compiler_params=_params(("parallel",)),
        name="hgrn_states_l%d" % l if states_only else "hgrn_l%d" % l,
    )(hg, hg, hg, hg, hg, psum, amask, lomask, bd, hw, sf0, sb0)
    return (None,) + tuple(outs) if states_only else tuple(outs)


def _dft_constants(l):
    k = np.arange(FN_GROUP_DIM)
    ang = 2.0 * np.pi * ((k[:, None] * k[None, :]) % FN_GROUP_DIM) / FN_GROUP_DIM
    eye = np.eye(FN_GROUPS)
    cs = np.concatenate([np.kron(eye, np.cos(ang)), np.kron(eye, np.sin(ang))], axis=1)
    n = np.arange(l)
    angl = 2.0 * np.pi * ((n[:, None] * n[None, :]) % l) / l
    return tuple(jnp.asarray(a, F32).astype(BF16) for a in (cs, np.cos(angl), np.sin(angl)))


def _fnet_kernel(scale, u_ref, cs_ref, cl_ref, sl_ref, wf_ref, o_ref):
    t = _dot(u_ref[...], cs_ref[...])
    uc = t[:, :WIDTH_C].astype(BF16)
    us = t[:, WIDTH_C:].astype(BF16)
    y = (_dot(cl_ref[...], uc) - _dot(sl_ref[...], us)) * scale
    o_ref[...] = _dot(y.astype(BF16), wf_ref[...]).astype(BF16)


def _fnet(uf, dft, wf_bf, li):
    b, l, _ = uf.shape
    cs, cl, sl = dft
    scale = 1.0 / math.sqrt(l * FN_GROUP_DIM)
    return pl.pallas_call(
        functools.partial(_fnet_kernel, scale),
        grid=(b,),
        in_specs=[
            pl.BlockSpec((None, l, WIDTH_C), lambda bi: (bi, 0, 0)),
            _const_spec(cs.shape, lambda bi: (0, 0)),
            _const_spec(cl.shape, lambda bi: (0, 0)),
            _const_spec(sl.shape, lambda bi: (0, 0)),
            _const_spec((None, WIDTH_C, WIDTH_C), lambda bi: (li, 0, 0)),
        ],
        out_specs=pl.BlockSpec((None, l, WIDTH_C), lambda bi: (bi, 0, 0)),
        out_shape=jax.ShapeDtypeStruct((b, l, WIDTH_C), BF16),
        compiler_params=_params(("parallel",)),
        name="fnet_l%d" % l,
    )(uf, cs, cl, sl, wf_bf)


def _mix_ffn_kernel(tm, tpb, final, att_ref, attp_ref, attn_ref, rec_ref, recp_ref, recn_ref, four_ref,
                    fourp_ref, fourn_ref, x_ref, xp_ref, xn_ref, mod_ref, nw_ref, wo_ref, wup_ref, cw_ref,
                    cb_ref, wdn_ref, fw_ref, o_ref, a_ref):
    t = pl.program_id(0) % tpb
    ext = tm + 2 * HALO
    m = mod_ref[...]

    def with_halo(prev_ref, main_ref, next_ref):
        return jnp.concatenate([prev_ref[...], main_ref[...], next_ref[...]], axis=0)

    mix = jnp.concatenate([with_halo(attp_ref, att_ref, attn_ref), with_halo(recp_ref, rec_ref, recn_ref),
                           with_halo(fourp_ref, four_ref, fourn_ref)], axis=1)
    x1e = with_halo(xp_ref, x_ref, xn_ref) + m[2:3] * _dot(mix, wo_ref[...])
    h = (x1e * _rms_scale(x1e) * nw_ref[...]) * (1.0 + m[4:5]) + m[3:4]
    row = lax.broadcasted_iota(jnp.int32, (ext, 1), 0)
    outside = ((row < HALO) & (t == 0)) | ((row >= HALO + tm) & (t == tpb - 1))
    hext = jnp.where(outside, 0.0, h).astype(BF16)

    def conv(u, c0):
        w = cw_ref[:, c0:c0 + FFN_COLS]
        up = pltpu.roll(u, 1, 0)
        dn = pltpu.roll(u, ext - 1, 0)
        r = up * w[0:1] + u * w[1:2] + dn * w[2:3] + cb_ref[:, c0:c0 + FFN_COLS]
        return r[HALO:HALO + tm]

    for j in range(D_FF // FFN_COLS):
        c0 = j * FFN_COLS
        gate = conv(_dot(hext, wup_ref[:, c0:c0 + FFN_COLS]), c0)
        val = conv(_dot(hext, wup_ref[:, D_FF + c0:D_FF + c0 + FFN_COLS]), D_FF + c0)
        a_ref[:, c0:c0 + FFN_COLS] = (gate * jax.nn.sigmoid(gate) * val).astype(BF16)
    x2 = x1e[HALO:HALO + tm] + m[5:6] * _dot(a_ref[...], wdn_ref[...])
    if final:
        x2 = x2 * _rms_scale(x2) * fw_ref[...]
    o_ref[...] = x2


def _mix_ffn(att, rec, four, x2d, mod4, nw2, wo_bf, wup_bf, conv_w, conv_b, wdn_bf, final_w, li, seg_len,
             mod_row, tm, final):
    rows = x2d.shape[0]
    tpb = seg_len // tm
    hb = tm // HALO
    nhb = rows // HALO
    mod_map = (lambda i: (li, i // tpb, 0, 0)) if mod_row is None else (lambda i: (li, mod_row, 0, 0))

    def row_specs(w):
        return [pl.BlockSpec((tm, w), lambda i: (i, 0)),
                pl.BlockSpec((HALO, w), lambda i: (jnp.maximum(i * hb - 1, 0), 0)),
                pl.BlockSpec((HALO, w), lambda i: (jnp.minimum((i + 1) * hb, nhb - 1), 0))]

    return pl.pallas_call(
        functools.partial(_mix_ffn_kernel, tm, tpb, final),
        grid=(rows // tm,),
        in_specs=row_specs(WIDTH_A) + row_specs(WIDTH_B) + row_specs(WIDTH_C) + row_specs(D_MODEL) + [
            pl.BlockSpec((None, None, 6, D_MODEL), mod_map),
            pl.BlockSpec((None, 1, D_MODEL), lambda i: (li, 0, 0)),
            _const_spec((None, D_MODEL, D_MODEL), lambda i: (li, 0, 0)),
            _const_spec((None, D_MODEL, 2 * D_FF), lambda i: (li, 0, 0)),
            pl.BlockSpec((None, 3, 2 * D_FF), lambda i: (li, 0, 0)),
            pl.BlockSpec((None, 1, 2 * D_FF), lambda i: (li, 0, 0)),
            _const_spec((None, D_FF, D_MODEL), lambda i: (li, 0, 0)),
            pl.BlockSpec((1, D_MODEL), lambda i: (0, 0)),
        ],
        out_specs=pl.BlockSpec((tm, D_MODEL), lambda i: (i, 0)),
        out_shape=jax.ShapeDtypeStruct((rows, D_MODEL), F32),
        scratch_shapes=[pltpu.VMEM((tm, D_FF), BF16)],
        compiler_params=_params(("parallel",)),
        name="mix_convffn_final" if final else "mix_convffn",
    )(att, att, att, rec, rec, rec, four, four, four, x2d, x2d, x2d, mod4, nw2, wo_bf, wup_bf, conv_w, conv_b,
      wdn_bf, final_w)


def _rope_tables(l):
    rows = l // GRID_W
    pos_r = np.repeat(np.arange(rows), GRID_W).astype(np.float32)
    pos_c = np.tile(np.arange(GRID_W), rows).astype(np.float32)
    half = DH_A // 2
    inv_freq = (ROPE_BASE ** (-np.arange(0, half, 2, dtype=np.float32) / half)).astype(np.float32)
    ang = np.concatenate([pos_r[:, None] * inv_freq] * 2 + [pos_c[:, None] * inv_freq] * 2, axis=-1)
    cos = np.cos(ang.astype(np.float32)).astype(np.float32)
    sin = np.sin(ang.astype(np.float32)).astype(np.float32)
    cos = np.tile(cos, (1, 2))
    sin = np.tile(sin, (1, 2))
    first = (np.arange(DV_A) % 32) < 16
    sin_a = np.where(first, -sin, 0.0).astype(np.float32)
    sin_b = np.where(first, 0.0, sin).astype(np.float32)
    return jnp.asarray(cos), jnp.asarray(sin_a), jnp.asarray(sin_b)


def kernel(x, c, ctx, c_ctx, w_ada, b_ada, norm1_w, norm2_w, w_in, lam_qk, subln_w, lb_param, hgrn_norm_w,
           w_fnet, w_out, w_up, conv_w, conv_b, w_down, final_norm_w):
    b, l, d = x.shape
    lc = ctx.shape[1]
    depth = w_ada.shape[0]
    assert d == D_MODEL and l % GRID_W == 0 and b <= 8

    lb_all = jnp.cumsum(jax.nn.softmax(lb_param.astype(F32), axis=0), axis=0)
    lb_all = lb_all - lb_all[0:1]
    lbc = jnp.stack([jnp.log(lb_all[:, 0]), jnp.log1p(-lb_all[:, 0]),
                     jnp.log(lb_all[:, 1]), jnp.log1p(-lb_all[:, 1])], axis=1)
    lq = lam_qk.astype(F32)
    lam_dyn = jnp.exp(jnp.sum(lq[:, 0] * lq[:, 1], axis=-1)) - jnp.exp(jnp.sum(lq[:, 2] * lq[:, 3], axis=-1))
    rope_tabs = _rope_tables(l)
    hconst = _hgrn_constants()
    dft_l = _dft_constants(l)
    dft_c = _dft_constants(lc)
    w_in_bf = w_in.astype(BF16)
    w_out_bf = w_out.astype(BF16)
    w_up_bf = w_up.astype(BF16)
    w_dn_bf = w_down.astype(BF16)
    w_fn_bf = w_fnet.astype(BF16)
    nw1 = norm1_w.reshape(depth, 1, d)
    nw2 = norm2_w.reshape(depth, 1, d)
    sub_w = subln_w.reshape(depth, 1, DV_A)
    hw = jnp.tile(hgrn_norm_w, (1, N_HEADS_B)).reshape(depth, 1, WIDTH_B)
    cb = conv_b.reshape(depth, 1, 2 * D_FF)
    fw = final_norm_w.reshape(1, d)

    cstack = jnp.concatenate([c, c_ctx[None, :], jnp.zeros((16 - b - 1, d), F32)], axis=0)
    mod4 = _modulation(cstack, w_ada, b_ada).reshape(depth, 16, 6, d)
    ctx_row = b

    zeros_state = jnp.zeros((b, WIDTH_B, WIDTH_B), F32)
    xl = x.reshape(b * l, d)
    xc = ctx.reshape(b * lc, d)
    tm_l = 1024
    tm_f = 1024
    tm_c = 256
    for li in range(depth):
        last = li == depth - 1
        lam_init = 0.8 - 0.6 * math.exp(-0.3 * li)
        lam = (lam_dyn[li] + lam_init).reshape(1, 1)

        qa, ka, va, hg, uf = _inproj(xl, mod4, nw1, w_in_bf, lbc, rope_tabs, li, l, None, tm_l)
        qc, kc, vc, hgc, ufc = _inproj(xc, mod4, nw1, w_in_bf, lbc, None, li, lc, ctx_row, tm_c)
        r3 = lambda a, n: a.reshape(b, n, a.shape[-1])

        att = _attention(lam, r3(qa, l), [r3(kc, lc), r3(ka, l)], [r3(vc, lc), r3(va, l)], sub_w, li,
                         1.0 - lam_init, 1, l, 128)
        rec_c, sf, sb = _hgrn(r3(hgc, lc), hconst, hw, zeros_state, zeros_state, li, states_only=last)
        rec, _, _ = _hgrn(r3(hg, l), hconst, hw, sf, sb, li)
        four = _fnet(r3(uf, l), dft_l, w_fn_bf, li)

        xl = _mix_ffn(att.reshape(b * l, -1), rec.reshape(b * l, -1), four.reshape(b * l, -1), xl, mod4, nw2,
                      w_out_bf, w_up_bf, conv_w, cb, w_dn_bf, fw, li, l, None, tm_f, last)

        if not last:
            att_c = _attention(lam, r3(qc, lc), [r3(kc, lc)], [r3(vc, lc)], sub_w, li, 1.0 - lam_init,
                               N_HEADS_A, lc, 128)
            four_c = _fnet(r3(ufc, lc), dft_c, w_fn_bf, li)
            xc = _mix_ffn(att_c.reshape(b * lc, -1), rec_c.reshape(b * lc, -1), four_c.reshape(b * lc, -1), xc,
                          mod4, nw2, w_out_bf, w_up_bf, conv_w, cb, w_dn_bf, fw, li, lc, ctx_row, tm_c, False)
    return xl.reshape(b, l, d)
```

```python
import functools
import math

import numpy as np
import jax
import jax.numpy as jnp
from jax import lax
from jax.experimental import pallas as pl
from jax.experimental.pallas import tpu as pltpu

F32 = jnp.float32
BF16 = jnp.bfloat16

D_MODEL = 1024
GRID_W = 64
N_HEADS_A = 4
DH_A = 64
DV_A = 2 * DH_A
WIDTH_A = N_HEADS_A * DV_A
N_HEADS_B = 4
DK_B = 64
DV_B = 64
WIDTH_B = N_HEADS_B * DV_B
FN_GROUPS = 4
FN_GROUP_DIM = 64
WIDTH_C = FN_GROUPS * FN_GROUP_DIM
PROJ_WIDTH = 3 * WIDTH_A + 5 * WIDTH_B + WIDTH_C
D_FF = 2816
ROPE_BASE = 10000.0
EPS = 1e-6
LOG2E = math.log2(math.e)
exp2 = jnp.exp2
HG_UNIT = LOG2E

V7X_LANES = 128
V7X_BF16_SUBLANES = 16
V7X_VMEM_BYTES = 64 * 1024 * 1024
VMEM_LIMIT = V7X_VMEM_BYTES - 8 * 1024 * 1024

HG_CHUNK = 64
HG_LEVELS = 6
HG_WIDTHS = tuple(HG_CHUNK >> (l + 1) for l in range(HG_LEVELS))
INPROJ_SUB = 256
FFN_COLS = 256
HALO = V7X_BF16_SUBLANES


def _params(sem):
    return pltpu.CompilerParams(dimension_semantics=sem, vmem_limit_bytes=VMEM_LIMIT)


def _const_spec(shape, index_map):
    return pl.BlockSpec(shape, index_map, pipeline_mode=pl.Buffered(1))


def _dot(a, b):
    return jnp.dot(a, b, preferred_element_type=F32)


def _dot_nt(a, b):
    return lax.dot_general(a, b, (((1,), (1,)), ((), ())), preferred_element_type=F32)


def _dot_tn(a, b):
    return lax.dot_general(a, b, (((0,), (0,)), ((), ())), preferred_element_type=F32)


def _split3(x):
    h = x.astype(BF16)
    r = x - h.astype(F32)
    m = r.astype(BF16)
    return h, m, (r - m.astype(F32)).astype(BF16)


def _split2(x):
    h = x.astype(BF16)
    return h, (x - h.astype(F32)).astype(BF16)


def _rms_scale(x):
    return lax.rsqrt(jnp.mean(x * x, axis=-1, keepdims=True) + EPS)


def _mod_kernel(c_ref, w_ref, b_ref, o_ref):
    c = c_ref[...]
    a = (c * jax.nn.sigmoid(c)).astype(BF16)
    o_ref[0] = _dot(a, w_ref[0].astype(BF16)) + b_ref[0]


def _modulation(cstack, w_ada, b_ada):
    depth, _, n = w_ada.shape
    tn = 1536
    rows = cstack.shape[0]
    return pl.pallas_call(
        _mod_kernel,
        grid=(depth, n // tn),
        in_specs=[
            pl.BlockSpec((rows, D_MODEL), lambda l, j: (0, 0)),
            pl.BlockSpec((1, D_MODEL, tn), lambda l, j: (l, 0, j)),
            pl.BlockSpec((1, 1, tn), lambda l, j: (l, 0, j)),
        ],
        out_specs=pl.BlockSpec((1, rows, tn), lambda l, j: (l, 0, j)),
        out_shape=jax.ShapeDtypeStruct((depth, rows, n), F32),
        compiler_params=_params(("arbitrary", "arbitrary")),
        name="modulation",
    )(cstack, w_ada, b_ada.reshape(depth, 1, n))


def _log_forget(z, log_lb, log1m_lb):
    ls = jnp.minimum(z, 0.0) - jnp.log(1.0 + jnp.exp(-jnp.abs(z)))
    b = log1m_lb + ls
    return jnp.maximum(log_lb, b) + jnp.log(1.0 + jnp.exp(-jnp.abs(log_lb - b)))


def _inproj_kernel(rope, x_ref, mod_ref, nw_ref, w_ref, lbc_ref, *rest):
    if rope:
        cos_ref, sa_ref, sb_ref, qa_ref, ka_ref, va_ref, hg_ref, uf_ref = rest
    else:
        qa_ref, ka_ref, va_ref, hg_ref, uf_ref = rest
    m = mod_ref[...]
    lbc = lbc_ref[...]
    c0 = 3 * WIDTH_A
    for r0 in range(0, x_ref.shape[0], INPROJ_SUB):
        rs = slice(r0, r0 + INPROJ_SUB)
        x = x_ref[rs, :]
        y = x * _rms_scale(x) * nw_ref[...]
        h = (y * (1.0 + m[1:2]) + m[0:1]).astype(BF16)

        def proj(col, n):
            return _dot(h, w_ref[:, col:col + n])

        for base, ref, scale in ((0, qa_ref, DH_A ** -0.5 * LOG2E), (WIDTH_A, ka_ref, None)):
            t = proj(base, WIDTH_A)
            for hd in range(N_HEADS_A):
                th = t[:, hd * DV_A:(hd + 1) * DV_A]
                if rope:
                    th = (th * cos_ref[rs, :]
                          + pltpu.roll(th, V7X_LANES - 16, 1) * sa_ref[rs, :]
                          + pltpu.roll(th, 16, 1) * sb_ref[rs, :])
                if scale is not None:
                    th = th * scale
                ref[rs, hd * DV_A:(hd + 1) * DV_A] = th.astype(BF16)
        va_ref[rs, :] = proj(2 * WIDTH_A, WIDTH_A).astype(BF16)
        hg_ref[rs, 0:WIDTH_B] = proj(c0, WIDTH_B)
        for dirn in range(2):
            z = proj(c0 + WIDTH_B * (1 + dirn), WIDTH_B)
            hg_ref[rs, WIDTH_B * (1 + dirn):WIDTH_B * (2 + dirn)] = _log_forget(
                z, lbc[2 * dirn:2 * dirn + 1], lbc[2 * dirn + 1:2 * dirn + 2])
        hg_ref[rs, 3 * WIDTH_B:5 * WIDTH_B] = proj(c0 + 3 * WIDTH_B, 2 * WIDTH_B)
        uf_ref[rs, :] = proj(c0 + 5 * WIDTH_B, WIDTH_C).astype(BF16)


def _inproj(x2d, mod4, nw, w_bf, lbc, rope_tabs, li, seg_len, mod_row, tm):
    rows = x2d.shape[0]
    tpb = seg_len // tm
    rope = rope_tabs is not None
    mod_map = (lambda i: (li, i // tpb, 0, 0)) if mod_row is None else (lambda i: (li, mod_row, 0, 0))
    in_specs = [
        pl.BlockSpec((tm, D_MODEL), lambda i: (i, 0)),
        pl.BlockSpec((None, None, 6, D_MODEL), mod_map),
        pl.BlockSpec((None, 1, D_MODEL), lambda i: (li, 0, 0)),
        _const_spec((None, D_MODEL, PROJ_WIDTH), lambda i: (li, 0, 0)),
        pl.BlockSpec((None, 4, WIDTH_B), lambda i: (li, 0, 0)),
    ]
    args = [x2d, mod4, nw, w_bf, lbc]
    if rope:
        in_specs += [pl.BlockSpec((tm, DV_A), lambda i: (i % tpb, 0))] * 3
        args += list(rope_tabs)
    out_shape = (
        jax.ShapeDtypeStruct((rows, WIDTH_A), BF16),
        jax.ShapeDtypeStruct((rows, WIDTH_A), BF16),
        jax.ShapeDtypeStruct((rows, WIDTH_A), BF16),
        jax.ShapeDtypeStruct((rows, 5 * WIDTH_B), F32),
        jax.ShapeDtypeStruct((rows, WIDTH_C), BF16),
    )
    out_specs = (
        pl.BlockSpec((tm, WIDTH_A), lambda i: (i, 0)),
        pl.BlockSpec((tm, WIDTH_A), lambda i: (i, 0)),
        pl.BlockSpec((tm, WIDTH_A), lambda i: (i, 0)),
        pl.BlockSpec((tm, 5 * WIDTH_B), lambda i: (i, 0)),
        pl.BlockSpec((tm, WIDTH_C), lambda i: (i, 0)),
    )
    return pl.pallas_call(
        functools.partial(_inproj_kernel, rope),
        grid=(rows // tm,),
        in_specs=in_specs,
        out_specs=out_specs,
        out_shape=out_shape,
        compiler_params=_params(("parallel",)),
        name="inproj_rope" if rope else "inproj",
    )(*args)


def _attn_kernel(nseg, hps, tq, sub, out_scale, lam_ref, q_ref, *rest):
    k_refs = rest[:nseg]
    v_refs = rest[nseg:2 * nseg]
    w_ref, o_ref = rest[2 * nseg:]
    lam = lam_ref[0, 0]
    wn = w_ref[...] * out_scale
    lane = lax.broadcasted_iota(jnp.int32, (sub, DV_A), 1)
    for hh in range(hps):
        hl = slice(hh * DV_A, (hh + 1) * DV_A)
        vexts = []
        for v_ref in v_refs:
            v = v_ref[:, hl]
            vl = lax.broadcasted_iota(jnp.int32, v.shape, 1)
            vexts.append(jnp.concatenate([v, jnp.where(vl == 0, 1.0, 0.0).astype(BF16)], axis=1))
        for r0 in range(0, tq, sub):
            q = q_ref[r0:r0 + sub, hl].astype(F32)
            qs = jnp.concatenate([jnp.where(lane < DH_A, q, 0.0), jnp.where(lane >= DH_A, q, 0.0)],
                                 axis=0).astype(BF16)
            scores = [_dot_nt(qs, k_ref[:, hl]) for k_ref in k_refs]
            mx = jnp.max(scores[0], axis=-1, keepdims=True)
            for s in scores[1:]:
                mx = jnp.maximum(mx, jnp.max(s, axis=-1, keepdims=True))
            acc = sum(_dot(exp2(s - mx).astype(BF16), vext) for s, vext in zip(scores, vexts))
            r = acc[:, :DV_A] / acc[:, DV_A:DV_A + 1]
            o = r[:sub] - lam * r[sub:]
            o_ref[r0:r0 + sub, hl] = (o * _rms_scale(o) * wn).astype(BF16)


def _attention(lam, q, ks, vs, subln_w, li, out_scale, hps, tq, sub):
    b, lq, _ = q.shape
    nseg = len(ks)
    w = hps * DV_A
    in_specs = [
        pl.BlockSpec(memory_space=pltpu.SMEM),
        pl.BlockSpec((None, tq, w), lambda bi, h, i: (bi, i, h)),
    ]
    for arr in list(ks) + list(vs):
        in_specs.append(pl.BlockSpec((None, arr.shape[1], w), lambda bi, h, i: (bi, 0, h)))
    in_specs.append(pl.BlockSpec((None, 1, DV_A), lambda bi, h, i: (li, 0, 0)))
    return pl.pallas_call(
        functools.partial(_attn_kernel, nseg, hps, tq, sub, out_scale),
        grid=(b, N_HEADS_A // hps, lq // tq),
        in_specs=in_specs,
        out_specs=pl.BlockSpec((None, tq, w), lambda bi, h, i: (bi, i, h)),
        out_shape=jax.ShapeDtypeStruct((b, lq, WIDTH_A), BF16),
        compiler_params=_params(("parallel", "parallel", "arbitrary")),
        name="diff_attention_%dseg" % nseg,
    )(lam, q, *ks, *vs, subln_w)


def _hgrn_constants():
    c = HG_CHUNK
    t = np.arange(c)[:, None]
    r = np.arange(c)[None, :]
    amask = []
    lomask = []
    for w in HG_WIDTHS:
        hi = (t % (2 * w)) >= w
        same_pair = (t // (2 * w)) == (r // (2 * w))
        amask.append(same_pair & (hi != ((r % (2 * w)) >= w)))
        lomask.append(np.broadcast_to(~hi, (c, c)))
    psum = np.zeros((2 * c, 2 * c), np.float32)
    psum[:c, :c] = r <= t
    psum[c:, c:] = r >= t
    tile = lambda a: np.tile(np.asarray(a, np.float32), (1, N_HEADS_B))
    amask = np.stack([tile(a) for a in amask])
    lomask = np.stack([tile(a) for a in lomask])
    hd = np.arange(WIDTH_B) // DK_B
    bd = (hd[:, None] == hd[None, :]).astype(np.float32)
    return (jnp.asarray(psum, BF16), jnp.asarray(amask, F32), jnp.asarray(lomask, F32), jnp.asarray(bd, F32))


def _hgrn_kernel(nc, states_only, q_ref, gf_ref, gb_ref, v_ref, og_ref, psum_ref, amask_ref, lomask_ref,
                 bd_ref, hw_ref, sf0_ref, sb0_ref, *outs_and_scratch):
    if states_only:
        rec_ref = None
        sf_ref, sb_ref = outs_and_scratch[:2]
        scratch = outs_and_scratch[2:]
    else:
        rec_ref, sf_ref, sb_ref = outs_and_scratch[:3]
        scratch = outs_and_scratch[3:]
    (sfp_ref, sbn_ref, cf_ref, cbs_ref, qa_ref, ka_ref, qb_ref, kb_ref, aa_ref, pa_ref, ab_ref, pb_ref, oa_ref,
     ob_ref) = scratch
    c = HG_CHUNK
    bd = bd_ref[...]
    bd_bf = bd.astype(BF16)
    lane_head = lax.broadcasted_iota(jnp.int32, (c, WIDTH_B), 1) // DK_B

    def stack_heads(a):
        return jnp.concatenate([jnp.where(lane_head == h, a, 0.0).astype(BF16) for h in range(N_HEADS_B)],
                               axis=0)

    def rows(ref, n):
        if isinstance(n, int):
            return ref[n * c:(n + 1) * c, :]
        return ref[pl.ds(pl.multiple_of(n * c, c), c), :]

    sf_ref[...] = sf0_ref[...]
    sb_ref[...] = sb0_ref[...]

    def decayed_keys(i, kt_ref, dec_ref):
        nf = i
        nb = nc - 1 - i
        gf = rows(gf_ref, nf)
        gb = rows(gb_ref, nb)
        g = jnp.concatenate([gf, gb], axis=0)
        sums = sum(_dot(psum_ref[...], part) for part in _split3(g))
        cf = sums[:c]
        cbs = sums[c:]
        cf_ref[nf] = cf * HG_UNIT
        cbs_ref[nb] = cbs * HG_UNIT
        tot_f = cf[c - 1:c]
        tot_b = cbs[0:1]
        kt_ref[0] = (1.0 - jnp.exp(gf)) * jnp.exp(tot_f - cf)
        kt_ref[1] = (1.0 - jnp.exp(gb)) * jnp.exp(tot_b - cbs)
        dec_ref[0:1, :] = jnp.exp(tot_f)
        dec_ref[8:9, :] = jnp.exp(tot_b)

    def apply_step(i, kt_ref, dec_ref):
        nf = i
        nb = nc - 1 - i
        upd_f = _dot_tn(rows(v_ref, nf).astype(BF16), kt_ref[0].astype(BF16)) * bd
        upd_b = _dot_tn(rows(v_ref, nb).astype(BF16), kt_ref[1].astype(BF16)) * bd
        stf = sf_ref[...]
        sfp_ref[nf] = stf.astype(BF16)
        sf_ref[...] = stf * dec_ref[0:1, :] + upd_f
        stb = sb_ref[...]
        sbn_ref[nb] = stb.astype(BF16)
        sb_ref[...] = stb * dec_ref[8:9, :] + upd_b

    pre = ((qa_ref, oa_ref), (qb_ref, ob_ref))
    decayed_keys(0, *pre[0])

    def pre_body(j, carry):
        for par in (0, 1):
            i = 2 * j + par
            apply_step(i, *pre[par])
            decayed_keys(jnp.minimum(i + 1, nc - 1), *pre[1 - par])
        return carry

    lax.fori_loop(0, nc // 2, pre_body, 0)
    if states_only:
        return

    def prepare(n, qop_ref, kop_ref):
        q = rows(q_ref, n)
        gf = rows(gf_ref, n) * HG_UNIT
        gb = rows(gb_ref, n) * HG_UNIT
        cf = cf_ref[n]
        cbs = cbs_ref[n]
        kf = 1.0 - exp2(gf)
        kb = 1.0 - exp2(gb)
        for lvl, w in enumerate(HG_WIDTHS):
            lo = lomask_ref[lvl] > 0.5
            if 2 * w >= 8:
                anchor = lambda a, r: jnp.broadcast_to(
                    a.reshape(c // (2 * w), 2 * w, WIDTH_B)[:, r:r + 1, :],
                    (c // (2 * w), 2 * w, WIDTH_B)).reshape(c, WIDTH_B)
                d_f = cf - anchor(cf, w - 1)
                d_b = cbs - anchor(cbs, w)
                eq = exp2(jnp.where(lo, d_b, d_f))
                ek = exp2(-jnp.where(lo, d_f, d_b))
            elif w == 2:
                r4 = lax.broadcasted_iota(jnp.int32, (c, WIDTH_B), 0) & 3
                gf_up, gf_dn = pltpu.roll(gf, c - 1, 0), pltpu.roll(gf, 1, 0)
                gb_up, gb_dn = pltpu.roll(gb, c - 1, 0), pltpu.roll(gb, 1, 0)
                eq = exp2(jnp.where(r4 == 0, gb + gb_up, jnp.where(r4 == 1, gb,
                              jnp.where(r4 == 2, gf, gf + gf_dn))))
                ek = exp2(jnp.where(r4 == 0, gf_up, jnp.where(r4 == 3, gb_dn, 0.0)))
            else:
                eq = exp2(jnp.where(lo, gb, gf))
                ek = None
            klev = jnp.where(lo, kf, kb)
            qop_ref[lvl] = q * eq
            kop_ref[lvl] = klev if ek is None else klev * ek
        qop_ref[HG_LEVELS] = q * exp2(cf)
        qop_ref[HG_LEVELS + 1] = q * exp2(cbs)
        qop_ref[HG_LEVELS + 2] = q * (kf + kb)

    def scores(n, qop_ref, kop_ref, a_ref, part_ref):
        a_all = jnp.zeros((c, N_HEADS_B * c), F32)
        for lvl in range(HG_LEVELS):
            a_all = a_all + _dot_nt(qop_ref[lvl].astype(BF16), stack_heads(kop_ref[lvl])) * amask_ref[lvl]
        a_ref[...] = a_all
        part = _dot(qop_ref[HG_LEVELS + 2].astype(BF16), bd_bf) * rows(v_ref, n)
        part = part + _dot_nt(qop_ref[HG_LEVELS].astype(BF16), sfp_ref[n])
        part_ref[...] = part + _dot_nt(qop_ref[HG_LEVELS + 1].astype(BF16), sbn_ref[n])

    def values(n, a_ref, part_ref, o_ref):
        o_ref[...] = (_dot(a_ref[...].astype(BF16), stack_heads(rows(v_ref, n)))
                      + part_ref[...])

    def finish(n, o_ref):
        o = o_ref[...]
        sq_h, sq_l = _split2(o * o)
        ms = _dot(jnp.concatenate([sq_h, sq_l], axis=0), bd_bf)
        ms = (ms[:c] + ms[c:]) * (1.0 / DV_B)
        og = rows(og_ref, n)
        rec = o * lax.rsqrt(ms + EPS) * hw_ref[...] * (og * jax.nn.sigmoid(og))
        rec_ref[pl.ds(pl.multiple_of(n * c, c), c), :] = rec.astype(BF16)

    ops = ((qa_ref, ka_ref), (qb_ref, kb_ref))
    sc = ((aa_ref, pa_ref), (ab_ref, pb_ref))
    ob = (oa_ref, ob_ref)
    last = nc - 1

    def step(m, par, clamp):
        idx = (lambda i: jnp.minimum(i, last)) if clamp else (lambda i: i)
        if clamp or m >= 0:
            finish(m, ob[par])
        if clamp or m + 1 >= 0:
            values(idx(m + 1), *sc[1 - par], ob[1 - par])
        if clamp or m + 2 >= 0:
            scores(idx(m + 2), *ops[par], *sc[par])
        prepare(idx(m + 3), *ops[1 - par])

    for m in (-3, -2, -1):
        step(m, m % 2, False)

    def body(j, carry):
        step(2 * j, 0, True)
        step(2 * j + 1, 1, True)
        return carry

    lax.fori_loop(0, nc // 2, body, 0)


def _hgrn(hg, hconst, hw, sf0, sb0, li, states_only=False):
    b, l, _ = hg.shape
    nc = l // HG_CHUNK
    psum, amask, lomask, bd = hconst
    col = lambda j: pl.BlockSpec((None, l, WIDTH_B), lambda bi: (bi, 0, j))
    full = lambda a: _const_spec(a.shape, lambda bi: (0,) * a.ndim)
    st_spec = pl.BlockSpec((None, WIDTH_B, WIDTH_B), lambda bi: (bi, 0, 0))
    state_scratch = pltpu.VMEM((nc, WIDTH_B, WIDTH_B), BF16)
    sums_scratch = pltpu.VMEM((nc, HG_CHUNK, WIDTH_B), F32)
    q_operands = pltpu.VMEM((HG_LEVELS + 3, HG_CHUNK, WIDTH_B), F32)
    k_operands = pltpu.VMEM((HG_LEVELS, HG_CHUNK, WIDTH_B), F32)
    chunk_f32 = pltpu.VMEM((HG_CHUNK, WIDTH_B), F32)
    out_specs = [pl.BlockSpec((None, l, WIDTH_B), lambda bi: (bi, 0, 0)), st_spec, st_spec]
    out_shape = [jax.ShapeDtypeStruct((b, l, WIDTH_B), BF16),
                 jax.ShapeDtypeStruct((b, WIDTH_B, WIDTH_B), F32),
                 jax.ShapeDtypeStruct((b, WIDTH_B, WIDTH_B), F32)]
    if states_only:
        out_specs, out_shape = out_specs[1:], out_shape[1:]
    outs = pl.pallas_call(
        functools.partial(_hgrn_kernel, nc, states_only),
        grid=(b,),
        in_specs=[col(0), col(1), col(2), col(3), col(4), full(psum), full(amask),
                  full(lomask), full(bd), pl.BlockSpec((None, 1, WIDTH_B), lambda bi: (li, 0, 0)),
                  st_spec, st_spec],
        out_specs=tuple(out_specs),
        out_shape=tuple(out_shape),
        scratch_shapes=[state_scratch, state_scratch, sums_scratch, sums_scratch,
                        q_operands, k_operands, q_operands, k_operands,
                        chunk_f32, chunk_f32, chunk_f32, chunk_f32, chunk_f32, chunk_f32],
        compiler_params=_params(("parallel",)),
        name="hgrn_states_l%d" % l if states_only else "hgrn_l%d" % l,
    )(hg, hg, hg, hg, hg, psum, amask, lomask, bd, hw, sf0, sb0)
    return (None,) + tuple(outs) if states_only else tuple(outs)


def _dft_constants(l):
    k = np.arange(FN_GROUP_DIM)
    ang = 2.0 * np.pi * ((k[:, None] * k[None, :]) % FN_GROUP_DIM) / FN_GROUP_DIM
    eye = np.eye(FN_GROUPS)
    cs = np.concatenate([np.kron(eye, np.cos(ang)), np.kron(eye, np.sin(ang))], axis=1)
    n = np.arange(l)
    angl = 2.0 * np.pi * ((n[:, None] * n[None, :]) % l) / l
    return tuple(jnp.asarray(a, F32).astype(BF16) for a in (cs, np.cos(angl), np.sin(angl)))


def _fnet_kernel(scale, u_ref, cs_ref, cl_ref, sl_ref, wf_ref, o_ref):
    t = _dot(u_ref[...], cs_ref[...])
    uc = t[:, :WIDTH_C].astype(BF16)
    us = t[:, WIDTH_C:].astype(BF16)
    y = (_dot(cl_ref[...], uc) - _dot(sl_ref[...], us)) * scale
    o_ref[...] = _dot(y.astype(BF16), wf_ref[...]).astype(BF16)


def _fnet(uf, dft, wf_bf, li):
    b, l, _ = uf.shape
    cs, cl, sl = dft
    scale = 1.0 / math.sqrt(l * FN_GROUP_DIM)
    return pl.pallas_call(
        functools.partial(_fnet_kernel, scale),
        grid=(b,),
        in_specs=[
            pl.BlockSpec((None, l, WIDTH_C), lambda bi: (bi, 0, 0)),
            _const_spec(cs.shape, lambda bi: (0, 0)),
            _const_spec(cl.shape, lambda bi: (0, 0)),
            _const_spec(sl.shape, lambda bi: (0, 0)),
            _const_spec((None, WIDTH_C, WIDTH_C), lambda bi: (li, 0, 0)),
        ],
        out_specs=pl.BlockSpec((None, l, WIDTH_C), lambda bi: (bi, 0, 0)),
        out_shape=jax.ShapeDtypeStruct((b, l, WIDTH_C), BF16),
        compiler_params=_params(("parallel",)),
        name="fnet_l%d" % l,
    )(uf, cs, cl, sl, wf_bf)


def _mix_ffn_kernel(tm, tpb, final, att_ref, attp_ref, attn_ref, rec_ref, recp_ref, recn_ref, four_ref,
                    fourp_ref, fourn_ref, x_ref, xp_ref, xn_ref, mod_ref, nw_ref, wo_ref, wup_ref, cw_ref,
                    cb_ref, wdn_ref, fw_ref, o_ref, a_ref):
    t = pl.program_id(0) % tpb
    ext = tm + 2 * HALO
    m = mod_ref[...]

    def with_halo(prev_ref, main_ref, next_ref):
        return jnp.concatenate([prev_ref[...], main_ref[...], next_ref[...]], axis=0)

    mix = jnp.concatenate([with_halo(attp_ref, att_ref, attn_ref), with_halo(recp_ref, rec_ref, recn_ref),
                           with_halo(fourp_ref, four_ref, fourn_ref)], axis=1)
    x1e = with_halo(xp_ref, x_ref, xn_ref) + m[2:3] * _dot(mix, wo_ref[...])
    h = (x1e * _rms_scale(x1e) * nw_ref[...]) * (1.0 + m[4:5]) + m[3:4]
    row = lax.broadcasted_iota(jnp.int32, (ext, 1), 0)
    outside = ((row < HALO) & (t == 0)) | ((row >= HALO + tm) & (t == tpb - 1))
    hext = jnp.where(outside, 0.0, h).astype(BF16)

    def conv(u, c0):
        w = cw_ref[:, c0:c0 + FFN_COLS]
        up = pltpu.roll(u, 1, 0)
        dn = pltpu.roll(u, ext - 1, 0)
        r = up * w[0:1] + u * w[1:2] + dn * w[2:3] + cb_ref[:, c0:c0 + FFN_COLS]
        return r[HALO:HALO + tm]

    for j in range(D_FF // FFN_COLS):
        c0 = j * FFN_COLS
        gate = conv(_dot(hext, wup_ref[:, c0:c0 + FFN_COLS]), c0)
        val = conv(_dot(hext, wup_ref[:, D_FF + c0:D_FF + c0 + FFN_COLS]), D_FF + c0)
        a_ref[:, c0:c0 + FFN_COLS] = (gate * jax.nn.sigmoid(gate) * val).astype(BF16)
    x2 = x1e[HALO:HALO + tm] + m[5:6] * _dot(a_ref[...], wdn_ref[...])
    if final:
        x2 = x2 * _rms_scale(x2) * fw_ref[...]
    o_ref[...] = x2


def _mix_ffn(att, rec, four, x2d, mod4, nw2, wo_bf, wup_bf, conv_w, conv_b, wdn_bf, final_w, li, seg_len,
             mod_row, tm, final):
    rows = x2d.shape[0]
    tpb = seg_len // tm
    hb = tm // HALO
    nhb = rows // HALO
    mod_map = (lambda i: (li, i // tpb, 0, 0)) if mod_row is None else (lambda i: (li, mod_row, 0, 0))

    def row_specs(w):
        return [pl.BlockSpec((tm, w), lambda i: (i, 0)),
                pl.BlockSpec((HALO, w), lambda i: (jnp.maximum(i * hb - 1, 0), 0)),
                pl.BlockSpec((HALO, w), lambda i: (jnp.minimum((i + 1) * hb, nhb - 1), 0))]

    return pl.pallas_call(
        functools.partial(_mix_ffn_kernel, tm, tpb, final),
        grid=(rows // tm,),
        in_specs=row_specs(WIDTH_A) + row_specs(WIDTH_B) + row_specs(WIDTH_C) + row_specs(D_MODEL) + [
            pl.BlockSpec((None, None, 6, D_MODEL), mod_map),
            pl.BlockSpec((None, 1, D_MODEL), lambda i: (li, 0, 0)),
            _const_spec((None, D_MODEL, D_MODEL), lambda i: (li, 0, 0)),
            _const_spec((None, D_MODEL, 2 * D_FF), lambda i: (li, 0, 0)),
            pl.BlockSpec((None, 3, 2 * D_FF), lambda i: (li, 0, 0)),
            pl.BlockSpec((None, 1, 2 * D_FF), lambda i: (li, 0, 0)),
            _const_spec((None, D_FF, D_MODEL), lambda i: (li, 0, 0)),
            pl.BlockSpec((1, D_MODEL), lambda i: (0, 0)),
        ],
        out_specs=pl.BlockSpec((tm, D_MODEL), lambda i: (i, 0)),
        out_shape=jax.ShapeDtypeStruct((rows, D_MODEL), F32),
        scratch_shapes=[pltpu.VMEM((tm, D_FF), BF16)],
        compiler_params=_params(("parallel",)),
        name="mix_convffn_final" if final else "mix_convffn",
    )(att, att, att, rec, rec, rec, four, four, four, x2d, x2d, x2d, mod4, nw2, wo_bf, wup_bf, conv_w, conv_b,
      wdn_bf, final_w)


def _rope_tables(l):
    rows = l // GRID_W
    pos_r = np.repeat(np.arange(rows), GRID_W).astype(np.float32)
    pos_c = np.tile(np.arange(GRID_W), rows).astype(np.float32)
    half = DH_A // 2
    inv_freq = (ROPE_BASE ** (-np.arange(0, half, 2, dtype=np.float32) / half)).astype(np.float32)
    ang = np.concatenate([pos_r[:, None] * inv_freq] * 2 + [pos_c[:, None] * inv_freq] * 2, axis=-1)
    cos = np.cos(ang.astype(np.float32)).astype(np.float32)
    sin = np.sin(ang.astype(np.float32)).astype(np.float32)
    cos = np.tile(cos, (1, 2))
    sin = np.tile(sin, (1, 2))
    first = (np.arange(DV_A) % 32) < 16
    sin_a = np.where(first, -sin, 0.0).astype(np.float32)
    sin_b = np.where(first, 0.0, sin).astype(np.float32)
    return jnp.asarray(cos), jnp.asarray(sin_a), jnp.asarray(sin_b)


def kernel(x, c, ctx, c_ctx, w_ada, b_ada, norm1_w, norm2_w, w_in, lam_qk, subln_w, lb_param, hgrn_norm_w,
           w_fnet, w_out, w_up, conv_w, conv_b, w_down, final_norm_w):
    b, l, d = x.shape
    lc = ctx.shape[1]
    depth = w_ada.shape[0]
    assert d == D_MODEL and l % GRID_W == 0 and b <= 8

    lb_all = jnp.cumsum(jax.nn.softmax(lb_param.astype(F32), axis=0), axis=0)
    lb_all = lb_all - lb_all[0:1]
    lbc = jnp.stack([jnp.log(lb_all[:, 0]), jnp.log1p(-lb_all[:, 0]),
                     jnp.log(lb_all[:, 1]), jnp.log1p(-lb_all[:, 1])], axis=1)
    lq = lam_qk.astype(F32)
    lam_dyn = jnp.exp(jnp.sum(lq[:, 0] * lq[:, 1], axis=-1)) - jnp.exp(jnp.sum(lq[:, 2] * lq[:, 3], axis=-1))
    rope_tabs = _rope_tables(l)
    hconst = _hgrn_constants()
    dft_l = _dft_constants(l)
    dft_c = _dft_constants(lc)
    w_in_bf = w_in.astype(BF16)
    w_out_bf = w_out.astype(BF16)
    w_up_bf = w_up.astype(BF16)
    w_dn_bf = w_down.astype(BF16)
    w_fn_bf = w_fnet.astype(BF16)
    nw1 = norm1_w.reshape(depth, 1, d)
    nw2 = norm2_w.reshape(depth, 1, d)
    sub_w = subln_w.reshape(depth, 1, DV_A)
    hw = jnp.tile(hgrn_norm_w, (1, N_HEADS_B)).reshape(depth, 1, WIDTH_B)
    cb = conv_b.reshape(depth, 1, 2 * D_FF)
    fw = final_norm_w.reshape(1, d)

    cstack = jnp.concatenate([c, c_ctx[None, :], jnp.zeros((16 - b - 1, d), F32)], axis=0)
    mod4 = _modulation(cstack, w_ada, b_ada).reshape(depth, 16, 6, d)
    ctx_row = b

    zeros_state = jnp.zeros((b, WIDTH_B, WIDTH_B), F32)
    xl = x.reshape(b * l, d)
    xc = ctx.reshape(b * lc, d)
    tm_l = 1024
    tm_f = 1024
    tm_c = 256
    for li in range(depth):
        last = li == depth - 1
        lam_init = 0.8 - 0.6 * math.exp(-0.3 * li)
        lam = (lam_dyn[li] + lam_init).reshape(1, 1)

        qa, ka, va, hg, uf = _inproj(xl, mod4, nw1, w_in_bf, lbc, rope_tabs, li, l, None, tm_l)
        qc, kc, vc, hgc, ufc = _inproj(xc, mod4, nw1, w_in_bf, lbc, None, li, lc, ctx_row,
                                       math.gcd(b * lc, tm_l))
        r3 = lambda a, n: a.reshape(b, n, a.shape[-1])

        att = _attention(lam, r3(qa, l), [r3(kc, lc), r3(ka, l)], [r3(vc, lc), r3(va, l)], sub_w, li,
                         1.0 - lam_init, 1, l, 256)
        rec_c, sf, sb = _hgrn(r3(hgc, lc), hconst, hw, zeros_state, zeros_state, li, states_only=last)
        rec, _, _ = _hgrn(r3(hg, l), hconst, hw, sf, sb, li)
        four = _fnet(r3(uf, l), dft_l, w_fn_bf, li)

        xl = _mix_ffn(att.reshape(b * l, -1), rec.reshape(b * l, -1), four.reshape(b * l, -1), xl, mod4, nw2,
                      w_out_bf, w_up_bf, conv_w, cb, w_dn_bf, fw, li, l, None, tm_f, last)

        if not last:
            att_c = _attention(lam, r3(qc, lc), [r3(kc, lc)], [r3(vc, lc)], sub_w, li, 1.0 - lam_init,
                               N_HEADS_A, lc, 128)
            four_c = _fnet(r3(ufc, lc), dft_c, w_fn_bf, li)
            xc = _mix_ffn(att_c.reshape(b * lc, -1), rec_c.reshape(b * lc, -1), four_c.reshape(b * lc, -1), xc,
                          mod4, nw2, w_out_bf, w_up_bf, conv_w, cb, w_dn_bf, fw, li, lc, ctx_row, tm_c, False)
    return xl.reshape(b, l, d)
```

```python
import functools
import math

import numpy as np
import jax
import jax.numpy as jnp
from jax import lax
from jax.experimental import pallas as pl
from jax.experimental.pallas import tpu as pltpu

F32 = jnp.float32
BF16 = jnp.bfloat16

D_MODEL = 1024
GRID_W = 64
N_HEADS_A = 4
DH_A = 64
DV_A = 2 * DH_A
WIDTH_A = N_HEADS_A * DV_A
N_HEADS_B = 4
DK_B = 64
DV_B = 64
WIDTH_B = N_HEADS_B * DV_B
FN_GROUPS = 4
FN_GROUP_DIM = 64
WIDTH_C = FN_GROUPS * FN_GROUP_DIM
PROJ_WIDTH = 3 * WIDTH_A + 5 * WIDTH_B + WIDTH_C
D_FF = 2816
ROPE_BASE = 10000.0
EPS = 1e-6
LOG2E = math.log2(math.e)
exp2 = jnp.exp2
HG_UNIT = LOG2E

V7X_LANES = 128
V7X_BF16_SUBLANES = 16
V7X_VMEM_BYTES = 64 * 1024 * 1024
VMEM_LIMIT = V7X_VMEM_BYTES - 8 * 1024 * 1024

MOD_COLS = 1536
INPROJ_ROWS = 1024
INPROJ_SUB = 256
ATTN_SUB = 256
FFN_ROWS = 1024
FFN_COLS = 256
HALO = V7X_BF16_SUBLANES
HG_CHUNK = 64
HG_LEVELS = 6
HG_WIDTHS = tuple(HG_CHUNK >> (l + 1) for l in range(HG_LEVELS))


def _params(sem):
    return pltpu.CompilerParams(dimension_semantics=sem, vmem_limit_bytes=VMEM_LIMIT)


def _const_spec(shape, index_map):
    return pl.BlockSpec(shape, index_map, pipeline_mode=pl.Buffered(1))


def _dot(a, b):
    return jnp.dot(a, b, preferred_element_type=F32)


def _dot_nt(a, b):
    return lax.dot_general(a, b, (((1,), (1,)), ((), ())), preferred_element_type=F32)


def _dot_tn(a, b):
    return lax.dot_general(a, b, (((0,), (0,)), ((), ())), preferred_element_type=F32)


def _split3(x):
    h = x.astype(BF16)
    r = x - h.astype(F32)
    m = r.astype(BF16)
    return h, m, (r - m.astype(F32)).astype(BF16)


def _split2(x):
    h = x.astype(BF16)
    return h, (x - h.astype(F32)).astype(BF16)


def _rms_scale(x):
    return lax.rsqrt(jnp.mean(x * x, axis=-1, keepdims=True) + EPS)


def _mod_kernel(c_ref, w_ref, b_ref, o_ref):
    c = c_ref[...]
    a = (c * jax.nn.sigmoid(c)).astype(BF16)
    o_ref[0] = _dot(a, w_ref[0].astype(BF16)) + b_ref[0]


def _modulation(cstack, w_ada, b_ada):
    depth, _, n = w_ada.shape
    tn = MOD_COLS
    rows = cstack.shape[0]
    return pl.pallas_call(
        _mod_kernel,
        grid=(depth, n // tn),
        in_specs=[
            pl.BlockSpec((rows, D_MODEL), lambda l, j: (0, 0)),
            pl.BlockSpec((1, D_MODEL, tn), lambda l, j: (l, 0, j)),
            pl.BlockSpec((1, 1, tn), lambda l, j: (l, 0, j)),
        ],
        out_specs=pl.BlockSpec((1, rows, tn), lambda l, j: (l, 0, j)),
        out_shape=jax.ShapeDtypeStruct((depth, rows, n), F32),
        compiler_params=_params(("arbitrary", "arbitrary")),
        name="modulation",
    )(cstack, w_ada, b_ada.reshape(depth, 1, n))


def _log_forget(z, log_lb, log1m_lb):
    ls = jnp.minimum(z, 0.0) - jnp.log(1.0 + jnp.exp(-jnp.abs(z)))
    b = log1m_lb + ls
    return jnp.maximum(log_lb, b) + jnp.log(1.0 + jnp.exp(-jnp.abs(log_lb - b)))


def _inproj_kernel(rope, x_ref, mod_ref, nw_ref, w_ref, lbc_ref, *rest):
    if rope:
        cos_ref, sa_ref, sb_ref, qa_ref, ka_ref, va_ref, hg_ref, uf_ref = rest
    else:
        qa_ref, ka_ref, va_ref, hg_ref, uf_ref = rest
    m = mod_ref[...]
    lbc = lbc_ref[...]
    c0 = 3 * WIDTH_A
    for r0 in range(0, x_ref.shape[0], INPROJ_SUB):
        rs = slice(r0, r0 + INPROJ_SUB)
        x = x_ref[rs, :]
        y = x * _rms_scale(x) * nw_ref[...]
        h = (y * (1.0 + m[1:2]) + m[0:1]).astype(BF16)

        def proj(col, n):
            return _dot(h, w_ref[:, col:col + n])

        for base, ref, scale in ((0, qa_ref, DH_A ** -0.5 * LOG2E), (WIDTH_A, ka_ref, None)):
            t = proj(base, WIDTH_A)
            for hd in range(N_HEADS_A):
                th = t[:, hd * DV_A:(hd + 1) * DV_A]
                if rope:
                    th = (th * cos_ref[rs, :]
                          + pltpu.roll(th, V7X_LANES - 16, 1) * sa_ref[rs, :]
                          + pltpu.roll(th, 16, 1) * sb_ref[rs, :])
                if scale is not None:
                    th = th * scale
                ref[rs, hd * DV_A:(hd + 1) * DV_A] = th.astype(BF16)
        va_ref[rs, :] = proj(2 * WIDTH_A, WIDTH_A).astype(BF16)
        hg_ref[rs, 0:WIDTH_B] = proj(c0, WIDTH_B)
        for dirn in range(2):
            z = proj(c0 + WIDTH_B * (1 + dirn), WIDTH_B)
            hg_ref[rs, WIDTH_B * (1 + dirn):WIDTH_B * (2 + dirn)] = _log_forget(
                z, lbc[2 * dirn:2 * dirn + 1], lbc[2 * dirn + 1:2 * dirn + 2])
        hg_ref[rs, 3 * WIDTH_B:5 * WIDTH_B] = proj(c0 + 3 * WIDTH_B, 2 * WIDTH_B)
        uf_ref[rs, :] = proj(c0 + 5 * WIDTH_B, WIDTH_C).astype(BF16)


def _inproj(x2d, mod4, nw, w_bf, lbc, rope_tabs, li, seg_len, mod_row, tm):
    rows = x2d.shape[0]
    tpb = seg_len // tm
    rope = rope_tabs is not None
    mod_map = (lambda i: (li, i // tpb, 0, 0)) if mod_row is None else (lambda i: (li, mod_row, 0, 0))
    in_specs = [
        pl.BlockSpec((tm, D_MODEL), lambda i: (i, 0)),
        pl.BlockSpec((None, None, 6, D_MODEL), mod_map),
        pl.BlockSpec((None, 1, D_MODEL), lambda i: (li, 0, 0)),
        _const_spec((None, D_MODEL, PROJ_WIDTH), lambda i: (li, 0, 0)),
        pl.BlockSpec((None, 4, WIDTH_B), lambda i: (li, 0, 0)),
    ]
    args = [x2d, mod4, nw, w_bf, lbc]
    if rope:
        in_specs += [pl.BlockSpec((tm, DV_A), lambda i: (i % tpb, 0))] * 3
        args += list(rope_tabs)
    out_shape = (
        jax.ShapeDtypeStruct((rows, WIDTH_A), BF16),
        jax.ShapeDtypeStruct((rows, WIDTH_A), BF16),
        jax.ShapeDtypeStruct((rows, WIDTH_A), BF16),
        jax.ShapeDtypeStruct((rows, 5 * WIDTH_B), F32),
        jax.ShapeDtypeStruct((rows, WIDTH_C), BF16),
    )
    out_specs = (
        pl.BlockSpec((tm, WIDTH_A), lambda i: (i, 0)),
        pl.BlockSpec((tm, WIDTH_A), lambda i: (i, 0)),
        pl.BlockSpec((tm, WIDTH_A), lambda i: (i, 0)),
        pl.BlockSpec((tm, 5 * WIDTH_B), lambda i: (i, 0)),
        pl.BlockSpec((tm, WIDTH_C), lambda i: (i, 0)),
    )
    return pl.pallas_call(
        functools.partial(_inproj_kernel, rope),
        grid=(rows // tm,),
        in_specs=in_specs,
        out_specs=out_specs,
        out_shape=out_shape,
        compiler_params=_params(("parallel",)),
        name="inproj_rope" if rope else "inproj",
    )(*args)


def _attn_kernel(nseg, hps, tq, sub, out_scale, lam_ref, q_ref, *rest):
    k_refs = rest[:nseg]
    v_refs = rest[nseg:2 * nseg]
    w_ref, o_ref = rest[2 * nseg:]
    lam = lam_ref[0, 0]
    wn = w_ref[...] * out_scale
    lane = lax.broadcasted_iota(jnp.int32, (sub, DV_A), 1)
    for hh in range(hps):
        hl = slice(hh * DV_A, (hh + 1) * DV_A)
        vexts = []
        for v_ref in v_refs:
            v = v_ref[:, hl]
            vl = lax.broadcasted_iota(jnp.int32, v.shape, 1)
            vexts.append(jnp.concatenate([v, jnp.where(vl == 0, 1.0, 0.0).astype(BF16)], axis=1))
        for r0 in range(0, tq, sub):
            q = q_ref[r0:r0 + sub, hl].astype(F32)
            qs = jnp.concatenate([jnp.where(lane < DH_A, q, 0.0), jnp.where(lane >= DH_A, q, 0.0)],
                                 axis=0).astype(BF16)
            scores = [_dot_nt(qs, k_ref[:, hl]) for k_ref in k_refs]
            mx = jnp.max(scores[0], axis=-1, keepdims=True)
            for s in scores[1:]:
                mx = jnp.maximum(mx, jnp.max(s, axis=-1, keepdims=True))
            acc = sum(_dot(exp2(s - mx).astype(BF16), vext) for s, vext in zip(scores, vexts))
            r = acc[:, :DV_A] / acc[:, DV_A:DV_A + 1]
            o = r[:sub] - lam * r[sub:]
            o_ref[r0:r0 + sub, hl] = (o * _rms_scale(o) * wn).astype(BF16)


def _attention(lam, q, ks, vs, subln_w, li, out_scale, hps, tq, sub):
    b, lq, _ = q.shape
    nseg = len(ks)
    w = hps * DV_A
    in_specs = [
        pl.BlockSpec(memory_space=pltpu.SMEM),
        pl.BlockSpec((None, tq, w), lambda bi, h, i: (bi, i, h)),
    ]
    for arr in list(ks) + list(vs):
        in_specs.append(pl.BlockSpec((None, arr.shape[1], w), lambda bi, h, i: (bi, 0, h)))
    in_specs.append(pl.BlockSpec((None, 1, DV_A), lambda bi, h, i: (li, 0, 0)))
    return pl.pallas_call(
        functools.partial(_attn_kernel, nseg, hps, tq, sub, out_scale),
        grid=(b, N_HEADS_A // hps, lq // tq),
        in_specs=in_specs,
        out_specs=pl.BlockSpec((None, tq, w), lambda bi, h, i: (bi, i, h)),
        out_shape=jax.ShapeDtypeStruct((b, lq, WIDTH_A), BF16),
        compiler_params=_params(("parallel", "parallel", "arbitrary")),
        name="diff_attention_%dseg" % nseg,
    )(lam, q, *ks, *vs, subln_w)


def _hgrn_constants():
    c = HG_CHUNK
    t = np.arange(c)[:, None]
    r = np.arange(c)[None, :]
    amask = []
    lomask = []
    for w in HG_WIDTHS:
        hi = (t % (2 * w)) >= w
        same_pair = (t // (2 * w)) == (r // (2 * w))
        amask.append(same_pair & (hi != ((r % (2 * w)) >= w)))
        lomask.append(np.broadcast_to(~hi, (c, c)))
    psum = np.zeros((2 * c, 2 * c), np.float32)
    psum[:c, :c] = r <= t
    psum[c:, c:] = r >= t
    tile = lambda a: np.tile(np.asarray(a, np.float32), (1, N_HEADS_B))
    amask = np.stack([tile(a) for a in amask])
    lomask = np.stack([tile(a) for a in lomask])
    hd = np.arange(WIDTH_B) // DK_B
    bd = (hd[:, None] == hd[None, :]).astype(np.float32)
    return (jnp.asarray(psum, BF16), jnp.asarray(amask, F32), jnp.asarray(lomask, F32), jnp.asarray(bd, F32))


def _hgrn_kernel(nc, states_only, q_ref, gf_ref, gb_ref, v_ref, og_ref, psum_ref, amask_ref, lomask_ref,
                 bd_ref, hw_ref, sf0_ref, sb0_ref, *outs_and_scratch):
    if states_only:
        rec_ref = None
        sf_ref, sb_ref = outs_and_scratch[:2]
        scratch = outs_and_scratch[2:]
    else:
        rec_ref, sf_ref, sb_ref = outs_and_scratch[:3]
        scratch = outs_and_scratch[3:]
    (sfp_ref, sbn_ref, cf_ref, cbs_ref, qa_ref, ka_ref, qb_ref, kb_ref, aa_ref, pa_ref, ab_ref, pb_ref, oa_ref,
     ob_ref) = scratch
    c = HG_CHUNK
    bd = bd_ref[...]
    bd_bf = bd.astype(BF16)
    lane_head = lax.broadcasted_iota(jnp.int32, (c, WIDTH_B), 1) // DK_B

    def stack_heads(a):
        return jnp.concatenate([jnp.where(lane_head == h, a, 0.0).astype(BF16) for h in range(N_HEADS_B)],
                               axis=0)

    def rows(ref, n):
        if isinstance(n, int):
            return ref[n * c:(n + 1) * c, :]
        return ref[pl.ds(pl.multiple_of(n * c, c), c), :]

    sf_ref[...] = sf0_ref[...]
    sb_ref[...] = sb0_ref[...]

    def decayed_keys(i, kt_ref, dec_ref):
        nf = i
        nb = nc - 1 - i
        gf = rows(gf_ref, nf)
        gb = rows(gb_ref, nb)
        g = jnp.concatenate([gf, gb], axis=0)
        sums = sum(_dot(psum_ref[...], part) for part in _split3(g))
        cf = sums[:c]
        cbs = sums[c:]
        cf_ref[nf] = cf * HG_UNIT
        cbs_ref[nb] = cbs * HG_UNIT
        tot_f = cf[c - 1:c]
        tot_b = cbs[0:1]
        kt_ref[0] = (1.0 - jnp.exp(gf)) * jnp.exp(tot_f - cf)
        kt_ref[1] = (1.0 - jnp.exp(gb)) * jnp.exp(tot_b - cbs)
        dec_ref[0:1, :] = jnp.exp(tot_f)
        dec_ref[8:9, :] = jnp.exp(tot_b)

    def apply_step(i, kt_ref, dec_ref):
        nf = i
        nb = nc - 1 - i
        upd_f = _dot_tn(rows(v_ref, nf).astype(BF16), kt_ref[0].astype(BF16)) * bd
        upd_b = _dot_tn(rows(v_ref, nb).astype(BF16), kt_ref[1].astype(BF16)) * bd
        stf = sf_ref[...]
        sfp_ref[nf] = stf.astype(BF16)
        sf_ref[...] = stf * dec_ref[0:1, :] + upd_f
        stb = sb_ref[...]
        sbn_ref[nb] = stb.astype(BF16)
        sb_ref[...] = stb * dec_ref[8:9, :] + upd_b

    pre = ((qa_ref, oa_ref), (qb_ref, ob_ref))
    decayed_keys(0, *pre[0])

    def pre_body(j, carry):
        for par in (0, 1):
            i = 2 * j + par
            apply_step(i, *pre[par])
            decayed_keys(jnp.minimum(i + 1, nc - 1), *pre[1 - par])
        return carry

    lax.fori_loop(0, nc // 2, pre_body, 0)
    if states_only:
        return

    def prepare(n, qop_ref, kop_ref):
        q = rows(q_ref, n)
        gf = rows(gf_ref, n) * HG_UNIT
        gb = rows(gb_ref, n) * HG_UNIT
        cf = cf_ref[n]
        cbs = cbs_ref[n]
        kf = 1.0 - exp2(gf)
        kb = 1.0 - exp2(gb)
        for lvl, w in enumerate(HG_WIDTHS):
            lo = lomask_ref[lvl] > 0.5
            if 2 * w >= 8:
                anchor = lambda a, r: jnp.broadcast_to(
                    a.reshape(c // (2 * w), 2 * w, WIDTH_B)[:, r:r + 1, :],
                    (c // (2 * w), 2 * w, WIDTH_B)).reshape(c, WIDTH_B)
                d_f = cf - anchor(cf, w - 1)
                d_b = cbs - anchor(cbs, w)
                eq = exp2(jnp.where(lo, d_b, d_f))
                ek = exp2(-jnp.where(lo, d_f, d_b))
            elif w == 2:
                r4 = lax.broadcasted_iota(jnp.int32, (c, WIDTH_B), 0) & 3
                gf_up, gf_dn = pltpu.roll(gf, c - 1, 0), pltpu.roll(gf, 1, 0)
                gb_up, gb_dn = pltpu.roll(gb, c - 1, 0), pltpu.roll(gb, 1, 0)
                eq = exp2(jnp.where(r4 == 0, gb + gb_up, jnp.where(r4 == 1, gb,
                              jnp.where(r4 == 2, gf, gf + gf_dn))))
                ek = exp2(jnp.where(r4 == 0, gf_up, jnp.where(r4 == 3, gb_dn, 0.0)))
            else:
                eq = exp2(jnp.where(lo, gb, gf))
                ek = None
            klev = jnp.where(lo, kf, kb)
            qop_ref[lvl] = q * eq
            kop_ref[lvl] = klev if ek is None else klev * ek
        qop_ref[HG_LEVELS] = q * exp2(cf)
        qop_ref[HG_LEVELS + 1] = q * exp2(cbs)
        qop_ref[HG_LEVELS + 2] = q * (kf + kb)

    def scores(n, qop_ref, kop_ref, a_ref, part_ref):
        a_all = jnp.zeros((c, N_HEADS_B * c), F32)
        for lvl in range(HG_LEVELS):
            a_all = a_all + _dot_nt(qop_ref[lvl].astype(BF16), stack_heads(kop_ref[lvl])) * amask_ref[lvl]
        a_ref[...] = a_all
        part = _dot(qop_ref[HG_LEVELS + 2].astype(BF16), bd_bf) * rows(v_ref, n)
        part = part + _dot_nt(qop_ref[HG_LEVELS].astype(BF16), sfp_ref[n])
        part_ref[...] = part + _dot_nt(qop_ref[HG_LEVELS + 1].astype(BF16), sbn_ref[n])

    def values(n, a_ref, part_ref, o_ref):
        o_ref[...] = (_dot(a_ref[...].astype(BF16), stack_heads(rows(v_ref, n)))
                      + part_ref[...])

    def finish(n, o_ref):
        o = o_ref[...]
        sq_h, sq_l = _split2(o * o)
        ms = _dot(jnp.concatenate([sq_h, sq_l], axis=0), bd_bf)
        ms = (ms[:c] + ms[c:]) * (1.0 / DV_B)
        og = rows(og_ref, n)
        rec = o * lax.rsqrt(ms + EPS) * hw_ref[...] * (og * jax.nn.sigmoid(og))
        rec_ref[pl.ds(pl.multiple_of(n * c, c), c), :] = rec.astype(BF16)

    ops = ((qa_ref, ka_ref), (qb_ref, kb_ref))
    sc = ((aa_ref, pa_ref), (ab_ref, pb_ref))
    ob = (oa_ref, ob_ref)
    last = nc - 1

    def step(m, par, clamp):
        idx = (lambda i: jnp.minimum(i, last)) if clamp else (lambda i: i)
        if clamp or m >= 0:
            finish(m, ob[par])
        if clamp or m + 1 >= 0:
            values(idx(m + 1), *sc[1 - par], ob[1 - par])
        if clamp or m + 2 >= 0:
            scores(idx(m + 2), *ops[par], *sc[par])
        prepare(idx(m + 3), *ops[1 - par])

    for m in (-3, -2, -1):
        step(m, m % 2, False)

    def body(j, carry):
        step(2 * j, 0, True)
        step(2 * j + 1, 1, True)
        return carry

    lax.fori_loop(0, nc // 2, body, 0)


def _hgrn(hg, hconst, hw, sf0, sb0, li, states_only=False):
    b, l, _ = hg.shape
    nc = l // HG_CHUNK
    psum, amask, lomask, bd = hconst
    col = lambda j: pl.BlockSpec((None, l, WIDTH_B), lambda bi: (bi, 0, j))
    full = lambda a: _const_spec(a.shape, lambda bi: (0,) * a.ndim)
    st_spec = pl.BlockSpec((None, WIDTH_B, WIDTH_B), lambda bi: (bi, 0, 0))
    state_scratch = pltpu.VMEM((nc, WIDTH_B, WIDTH_B), BF16)
    sums_scratch = pltpu.VMEM((nc, HG_CHUNK, WIDTH_B), F32)
    q_operands = pltpu.VMEM((HG_LEVELS + 3, HG_CHUNK, WIDTH_B), F32)
    k_operands = pltpu.VMEM((HG_LEVELS, HG_CHUNK, WIDTH_B), F32)
    chunk_f32 = pltpu.VMEM((HG_CHUNK, WIDTH_B), F32)
    out_specs = [pl.BlockSpec((None, l, WIDTH_B), lambda bi: (bi, 0, 0)), st_spec, st_spec]
    out_shape = [jax.ShapeDtypeStruct((b, l, WIDTH_B), BF16),
                 jax.ShapeDtypeStruct((b, WIDTH_B, WIDTH_B), F32),
                 jax.ShapeDtypeStruct((b, WIDTH_B, WIDTH_B), F32)]
    if states_only:
        out_specs, out_shape = out_specs[1:], out_shape[1:]
    outs = pl.pallas_call(
        functools.partial(_hgrn_kernel, nc, states_only),
        grid=(b,),
        in_specs=[col(0), col(1), col(2), col(3), col(4), full(psum), full(amask),
                  full(lomask), full(bd), pl.BlockSpec((None, 1, WIDTH_B), lambda bi: (li, 0, 0)),
                  st_spec, st_spec],
        out_specs=tuple(out_specs),
        out_shape=tuple(out_shape),
        scratch_shapes=[state_scratch, state_scratch, sums_scratch, sums_scratch,
                        q_operands, k_operands, q_operands, k_operands,
                        chunk_f32, chunk_f32, chunk_f32, chunk_f32, chunk_f32, chunk_f32],
        compiler_params=_params(("parallel",)),
        name="hgrn_states_l%d" % l if states_only else "hgrn_l%d" % l,
    )(hg, hg, hg, hg, hg, psum, amask, lomask, bd, hw, sf0, sb0)
    return (None,) + tuple(outs) if states_only else tuple(outs)


def _dft_constants(l):
    k = np.arange(FN_GROUP_DIM)
    ang = 2.0 * np.pi * ((k[:, None] * k[None, :]) % FN_GROUP_DIM) / FN_GROUP_DIM
    eye = np.eye(FN_GROUPS)
    cs = np.concatenate([np.kron(eye, np.cos(ang)), np.kron(eye, np.sin(ang))], axis=1)
    n = np.arange(l)
    angl = 2.0 * np.pi * ((n[:, None] * n[None, :]) % l) / l
    return tuple(jnp.asarray(a, F32).astype(BF16) for a in (cs, np.cos(angl), np.sin(angl)))


def _fnet_kernel(scale, u_ref, cs_ref, cl_ref, sl_ref, wf_ref, o_ref):
    t = _dot(u_ref[...], cs_ref[...])
    uc = t[:, :WIDTH_C].astype(BF16)
    us = t[:, WIDTH_C:].astype(BF16)
    y = (_dot(cl_ref[...], uc) - _dot(sl_ref[...], us)) * scale
    o_ref[...] = _dot(y.astype(BF16), wf_ref[...]).astype(BF16)


def _fnet(uf, dft, wf_bf, li):
    b, l, _ = uf.shape
    cs, cl, sl = dft
    scale = 1.0 / math.sqrt(l * FN_GROUP_DIM)
    return pl.pallas_call(
        functools.partial(_fnet_kernel, scale),
        grid=(b,),
        in_specs=[
            pl.BlockSpec((None, l, WIDTH_C), lambda bi: (bi, 0, 0)),
            _const_spec(cs.shape, lambda bi: (0, 0)),
            _const_spec(cl.shape, lambda bi: (0, 0)),
            _const_spec(sl.shape, lambda bi: (0, 0)),
            _const_spec((None, WIDTH_C, WIDTH_C), lambda bi: (li, 0, 0)),
        ],
        out_specs=pl.BlockSpec((None, l, WIDTH_C), lambda bi: (bi, 0, 0)),
        out_shape=jax.ShapeDtypeStruct((b, l, WIDTH_C), BF16),
        compiler_params=_params(("parallel",)),
        name="fnet_l%d" % l,
    )(uf, cs, cl, sl, wf_bf)


def _mix_ffn_kernel(tm, tpb, final, att_ref, attp_ref, attn_ref, rec_ref, recp_ref, recn_ref, four_ref,
                    fourp_ref, fourn_ref, x_ref, xp_ref, xn_ref, mod_ref, nw_ref, wo_ref, wup_ref, cw_ref,
                    cb_ref, wdn_ref, fw_ref, o_ref, a_ref):
    t = pl.program_id(0) % tpb
    ext = tm + 2 * HALO
    m = mod_ref[...]

    def with_halo(prev_ref, main_ref, next_ref):
        return jnp.concatenate([prev_ref[...], main_ref[...], next_ref[...]], axis=0)

    mix = jnp.concatenate([with_halo(attp_ref, att_ref, attn_ref), with_halo(recp_ref, rec_ref, recn_ref),
                           with_halo(fourp_ref, four_ref, fourn_ref)], axis=1)
    x1e = with_halo(xp_ref, x_ref, xn_ref) + m[2:3] * _dot(mix, wo_ref[...])
    h = (x1e * _rms_scale(x1e) * nw_ref[...]) * (1.0 + m[4:5]) + m[3:4]
    row = lax.broadcasted_iota(jnp.int32, (ext, 1), 0)
    outside = ((row < HALO) & (t == 0)) | ((row >= HALO + tm) & (t == tpb - 1))
    hext = jnp.where(outside, 0.0, h).astype(BF16)

    def conv(u, c0):
        w = cw_ref[:, c0:c0 + FFN_COLS]
        up = pltpu.roll(u, 1, 0)
        dn = pltpu.roll(u, ext - 1, 0)
        r = up * w[0:1] + u * w[1:2] + dn * w[2:3] + cb_ref[:, c0:c0 + FFN_COLS]
        return r[HALO:HALO + tm]

    for j in range(D_FF // FFN_COLS):
        c0 = j * FFN_COLS
        gate = conv(_dot(hext, wup_ref[:, c0:c0 + FFN_COLS]), c0)
        val = conv(_dot(hext, wup_ref[:, D_FF + c0:D_FF + c0 + FFN_COLS]), D_FF + c0)
        a_ref[:, c0:c0 + FFN_COLS] = (gate * jax.nn.sigmoid(gate) * val).astype(BF16)
    x2 = x1e[HALO:HALO + tm] + m[5:6] * _dot(a_ref[...], wdn_ref[...])
    if final:
        x2 = x2 * _rms_scale(x2) * fw_ref[...]
    o_ref[...] = x2


def _mix_ffn(att, rec, four, x2d, mod4, nw2, wo_bf, wup_bf, conv_w, conv_b, wdn_bf, final_w, li, seg_len,
             mod_row, tm, final):
    rows = x2d.shape[0]
    tpb = seg_len // tm
    hb = tm // HALO
    nhb = rows // HALO
    mod_map = (lambda i: (li, i // tpb, 0, 0)) if mod_row is None else (lambda i: (li, mod_row, 0, 0))

    def row_specs(w):
        return [pl.BlockSpec((tm, w), lambda i: (i, 0)),
                pl.BlockSpec((HALO, w), lambda i: (jnp.maximum(i * hb - 1, 0), 0)),
                pl.BlockSpec((HALO, w), lambda i: (jnp.minimum((i + 1) * hb, nhb - 1), 0))]

    return pl.pallas_call(
        functools.partial(_mix_ffn_kernel, tm, tpb, final),
        grid=(rows // tm,),
        in_specs=row_specs(WIDTH_A) + row_specs(WIDTH_B) + row_specs(WIDTH_C) + row_specs(D_MODEL) + [
            pl.BlockSpec((None, None, 6, D_MODEL), mod_map),
            pl.BlockSpec((None, 1, D_MODEL), lambda i: (li, 0, 0)),
            _const_spec((None, D_MODEL, D_MODEL), lambda i: (li, 0, 0)),
            _const_spec((None, D_MODEL, 2 * D_FF), lambda i: (li, 0, 0)),
            pl.BlockSpec((None, 3, 2 * D_FF), lambda i: (li, 0, 0)),
            pl.BlockSpec((None, 1, 2 * D_FF), lambda i: (li, 0, 0)),
            _const_spec((None, D_FF, D_MODEL), lambda i: (li, 0, 0)),
            pl.BlockSpec((1, D_MODEL), lambda i: (0, 0)),
        ],
        out_specs=pl.BlockSpec((tm, D_MODEL), lambda i: (i, 0)),
        out_shape=jax.ShapeDtypeStruct((rows, D_MODEL), F32),
        scratch_shapes=[pltpu.VMEM((tm, D_FF), BF16)],
        compiler_params=_params(("parallel",)),
        name="mix_convffn_final" if final else "mix_convffn",
    )(att, att, att, rec, rec, rec, four, four, four, x2d, x2d, x2d, mod4, nw2, wo_bf, wup_bf, conv_w, conv_b,
      wdn_bf, final_w)


def _rope_tables(l):
    rows = l // GRID_W
    pos_r = np.repeat(np.arange(rows), GRID_W).astype(np.float32)
    pos_c = np.tile(np.arange(GRID_W), rows).astype(np.float32)
    half = DH_A // 2
    inv_freq = (ROPE_BASE ** (-np.arange(0, half, 2, dtype=np.float32) / half)).astype(np.float32)
    ang = np.concatenate([pos_r[:, None] * inv_freq] * 2 + [pos_c[:, None] * inv_freq] * 2, axis=-1)
    cos = np.cos(ang.astype(np.float32)).astype(np.float32)
    sin = np.sin(ang.astype(np.float32)).astype(np.float32)
    cos = np.tile(cos, (1, 2))
    sin = np.tile(sin, (1, 2))
    first = (np.arange(DV_A) % 32) < 16
    sin_a = np.where(first, -sin, 0.0).astype(np.float32)
    sin_b = np.where(first, 0.0, sin).astype(np.float32)
    return jnp.asarray(cos), jnp.asarray(sin_a), jnp.asarray(sin_b)


def kernel(x, c, ctx, c_ctx, w_ada, b_ada, norm1_w, norm2_w, w_in, lam_qk, subln_w, lb_param, hgrn_norm_w,
           w_fnet, w_out, w_up, conv_w, conv_b, w_down, final_norm_w):
    b, l, d = x.shape
    lc = ctx.shape[1]
    depth = w_ada.shape[0]
    mod_rows = 16
    assert d == D_MODEL and b < mod_rows
    assert l % INPROJ_ROWS == 0 and l % FFN_ROWS == 0 and l % ATTN_SUB == 0 and lc % (2 * HG_CHUNK) == 0
    tm_l = INPROJ_ROWS
    tm_f = FFN_ROWS
    tm_c = lc

    lb_all = jnp.cumsum(jax.nn.softmax(lb_param.astype(F32), axis=0), axis=0)
    lb_all = lb_all - lb_all[0:1]
    lbc = jnp.stack([jnp.log(lb_all[:, 0]), jnp.log1p(-lb_all[:, 0]),
                     jnp.log(lb_all[:, 1]), jnp.log1p(-lb_all[:, 1])], axis=1)
    lq = lam_qk.astype(F32)
    lam_dyn = jnp.exp(jnp.sum(lq[:, 0] * lq[:, 1], axis=-1)) - jnp.exp(jnp.sum(lq[:, 2] * lq[:, 3], axis=-1))
    rope_tabs = _rope_tables(l)
    hconst = _hgrn_constants()
    dft_l = _dft_constants(l)
    dft_c = _dft_constants(lc)
    w_in_bf = w_in.astype(BF16)
    w_out_bf = w_out.astype(BF16)
    w_up_bf = w_up.astype(BF16)
    w_dn_bf = w_down.astype(BF16)
    w_fn_bf = w_fnet.astype(BF16)
    nw1 = norm1_w.reshape(depth, 1, d)
    nw2 = norm2_w.reshape(depth, 1, d)
    sub_w = subln_w.reshape(depth, 1, DV_A)
    hw = jnp.tile(hgrn_norm_w, (1, N_HEADS_B)).reshape(depth, 1, WIDTH_B)
    cb = conv_b.reshape(depth, 1, 2 * D_FF)
    fw = final_norm_w.reshape(1, d)

    cstack = jnp.concatenate([c, c_ctx[None, :], jnp.zeros((mod_rows - b - 1, d), F32)], axis=0)
    mod4 = _modulation(cstack, w_ada, b_ada).reshape(depth, mod_rows, 6, d)
    ctx_row = b

    zeros_state = jnp.zeros((b, WIDTH_B, WIDTH_B), F32)
    xl = x.reshape(b * l, d)
    xc = ctx.reshape(b * lc, d)
    for li in range(depth):
        last = li == depth - 1
        lam_init = 0.8 - 0.6 * math.exp(-0.3 * li)
        lam = (lam_dyn[li] + lam_init).reshape(1, 1)

        qa, ka, va, hg, uf = _inproj(xl, mod4, nw1, w_in_bf, lbc, rope_tabs, li, l, None, tm_l)
        qc, kc, vc, hgc, ufc = _inproj(xc, mod4, nw1, w_in_bf, lbc, None, li, lc, ctx_row,
                                       math.gcd(b * lc, tm_l))
        r3 = lambda a, n: a.reshape(b, n, a.shape[-1])

        att = _attention(lam, r3(qa, l), [r3(kc, lc), r3(ka, l)], [r3(vc, lc), r3(va, l)], sub_w, li,
                         1.0 - lam_init, 1, l, ATTN_SUB)
        rec_c, sf, sb = _hgrn(r3(hgc, lc), hconst, hw, zeros_state, zeros_state, li, states_only=last)
        rec, _, _ = _hgrn(r3(hg, l), hconst, hw, sf, sb, li)
        four = _fnet(r3(uf, l), dft_l, w_fn_bf, li)

        xl = _mix_ffn(att.reshape(b * l, -1), rec.reshape(b * l, -1), four.reshape(b * l, -1), xl, mod4, nw2,
                      w_out_bf, w_up_bf, conv_w, cb, w_dn_bf, fw, li, l, None, tm_f, last)

        if not last:
            att_c = _attention(lam, r3(qc, lc), [r3(kc, lc)], [r3(vc, lc)], sub_w, li, 1.0 - lam_init,
                               N_HEADS_A, lc, min(ATTN_SUB, lc))
            four_c = _fnet(r3(ufc, lc), dft_c, w_fn_bf, li)
            xc = _mix_ffn(att_c.reshape(b * lc, -1), rec_c.reshape(b * lc, -1), four_c.reshape(b * lc, -1), xc,
                          mod4, nw2, w_out_bf, w_up_bf, conv_w, cb, w_dn_bf, fw, li, lc, ctx_row, tm_c, False)
    return xl.reshape(b, l, d)
```

```python
import functools
import math

import numpy as np
import jax
import jax.numpy as jnp
from jax import lax
from jax.experimental import pallas as pl
from jax.experimental.pallas import tpu as pltpu

F32 = jnp.float32
BF16 = jnp.bfloat16

D_MODEL = 1024
GRID_W = 64
N_HEADS_A = 4
DH_A = 64
DV_A = 2 * DH_A
WIDTH_A = N_HEADS_A * DV_A
N_HEADS_B = 4
DK_B = 64
DV_B = 64
WIDTH_B = N_HEADS_B * DV_B
FN_GROUPS = 4
FN_GROUP_DIM = 64
WIDTH_C = FN_GROUPS * FN_GROUP_DIM
PROJ_WIDTH = 3 * WIDTH_A + 5 * WIDTH_B + WIDTH_C
D_FF = 2816
ROPE_BASE = 10000.0
EPS = 1e-6
LOG2E = math.log2(math.e)
exp2 = jnp.exp2
HG_UNIT = LOG2E

V7X_LANES = 128
V7X_BF16_SUBLANES = 16
V7X_VMEM_BYTES = 64 * 1024 * 1024
VMEM_LIMIT = V7X_VMEM_BYTES - 8 * 1024 * 1024

MOD_COLS = 1536
INPROJ_ROWS = 1024
INPROJ_SUB = 256
ATTN_SUB = 256
FFN_ROWS = 1024
FFN_COLS = 256
HALO = V7X_BF16_SUBLANES
HG_CHUNK = 64
HG_LEVELS = 6
HG_WIDTHS = tuple(HG_CHUNK >> (l + 1) for l in range(HG_LEVELS))
HG_STEPS_PER_TRIP = 8


def _params(sem):
    return pltpu.CompilerParams(dimension_semantics=sem, vmem_limit_bytes=VMEM_LIMIT)


def _const_spec(shape, index_map):
    return pl.BlockSpec(shape, index_map, pipeline_mode=pl.Buffered(1))


def _dot(a, b):
    return jnp.dot(a, b, preferred_element_type=F32)


def _dot_nt(a, b):
    return lax.dot_general(a, b, (((1,), (1,)), ((), ())), preferred_element_type=F32)


def _dot_tn(a, b):
    return lax.dot_general(a, b, (((0,), (0,)), ((), ())), preferred_element_type=F32)


def _split3(x):
    h = x.astype(BF16)
    r = x - h.astype(F32)
    m = r.astype(BF16)
    return h, m, (r - m.astype(F32)).astype(BF16)


def _split2(x):
    h = x.astype(BF16)
    return h, (x - h.astype(F32)).astype(BF16)


def _rms_scale(x):
    return lax.rsqrt(jnp.mean(x * x, axis=-1, keepdims=True) + EPS)


def _mod_kernel(c_ref, w_ref, b_ref, o_ref):
    c = c_ref[...]
    a = (c * jax.nn.sigmoid(c)).astype(BF16)
    o_ref[0] = _dot(a, w_ref[0].astype(BF16)) + b_ref[0]


def _modulation(cstack, w_ada, b_ada):
    depth, _, n = w_ada.shape
    tn = MOD_COLS
    rows = cstack.shape[0]
    return pl.pallas_call(
        _mod_kernel,
        grid=(depth, n // tn),
        in_specs=[
            pl.BlockSpec((rows, D_MODEL), lambda l, j: (0, 0)),
            pl.BlockSpec((1, D_MODEL, tn), lambda l, j: (l, 0, j)),
            pl.BlockSpec((1, 1, tn), lambda l, j: (l, 0, j)),
        ],
        out_specs=pl.BlockSpec((1, rows, tn), lambda l, j: (l, 0, j)),
        out_shape=jax.ShapeDtypeStruct((depth, rows, n), F32),
        compiler_params=_params(("arbitrary", "arbitrary")),
        name="modulation",
    )(cstack, w_ada, b_ada.reshape(depth, 1, n))


def _log_forget(z, log_lb, log1m_lb):
    ls = jnp.minimum(z, 0.0) - jnp.log(1.0 + jnp.exp(-jnp.abs(z)))
    b = log1m_lb + ls
    return jnp.maximum(log_lb, b) + jnp.log(1.0 + jnp.exp(-jnp.abs(log_lb - b)))


def _inproj_kernel(rope, x_ref, mod_ref, nw_ref, w_ref, lbc_ref, *rest):
    if rope:
        cos_ref, sa_ref, sb_ref, qa_ref, ka_ref, va_ref, hg_ref, uf_ref = rest
    else:
        qa_ref, ka_ref, va_ref, hg_ref, uf_ref = rest
    m = mod_ref[...]
    lbc = lbc_ref[...]
    c0 = 3 * WIDTH_A
    for r0 in range(0, x_ref.shape[0], INPROJ_SUB):
        rs = slice(r0, r0 + INPROJ_SUB)
        x = x_ref[rs, :]
        y = x * _rms_scale(x) * nw_ref[...]
        h = (y * (1.0 + m[1:2]) + m[0:1]).astype(BF16)

        def proj(col, n):
            return _dot(h, w_ref[:, col:col + n])

        for base, ref, scale in ((0, qa_ref, DH_A ** -0.5 * LOG2E), (WIDTH_A, ka_ref, None)):
            t = proj(base, WIDTH_A)
            for hd in range(N_HEADS_A):
                th = t[:, hd * DV_A:(hd + 1) * DV_A]
                if rope:
                    th = (th * cos_ref[rs, :]
                          + pltpu.roll(th, V7X_LANES - 16, 1) * sa_ref[rs, :]
                          + pltpu.roll(th, 16, 1) * sb_ref[rs, :])
                if scale is not None:
                    th = th * scale
                ref[rs, hd * DV_A:(hd + 1) * DV_A] = th.astype(BF16)
        va_ref[rs, :] = proj(2 * WIDTH_A, WIDTH_A).astype(BF16)
        hg_ref[rs, 0:WIDTH_B] = proj(c0, WIDTH_B)
        for dirn in range(2):
            z = proj(c0 + WIDTH_B * (1 + dirn), WIDTH_B)
            hg_ref[rs, WIDTH_B * (1 + dirn):WIDTH_B * (2 + dirn)] = _log_forget(
                z, lbc[2 * dirn:2 * dirn + 1], lbc[2 * dirn + 1:2 * dirn + 2])
        hg_ref[rs, 3 * WIDTH_B:5 * WIDTH_B] = proj(c0 + 3 * WIDTH_B, 2 * WIDTH_B)
        uf_ref[rs, :] = proj(c0 + 5 * WIDTH_B, WIDTH_C).astype(BF16)


def _inproj(x2d, mod4, nw, w_bf, lbc, rope_tabs, li, seg_len, mod_row, tm):
    rows = x2d.shape[0]
    tpb = seg_len // tm
    rope = rope_tabs is not None
    mod_map = (lambda i: (li, i // tpb, 0, 0)) if mod_row is None else (lambda i: (li, mod_row, 0, 0))
    in_specs = [
        pl.BlockSpec((tm, D_MODEL), lambda i: (i, 0)),
        pl.BlockSpec((None, None, 6, D_MODEL), mod_map),
        pl.BlockSpec((None, 1, D_MODEL), lambda i: (li, 0, 0)),
        _const_spec((None, D_MODEL, PROJ_WIDTH), lambda i: (li, 0, 0)),
        pl.BlockSpec((None, 4, WIDTH_B), lambda i: (li, 0, 0)),
    ]
    args = [x2d, mod4, nw, w_bf, lbc]
    if rope:
        in_specs += [pl.BlockSpec((tm, DV_A), lambda i: (i % tpb, 0))] * 3
        args += list(rope_tabs)
    out_shape = (
        jax.ShapeDtypeStruct((rows, WIDTH_A), BF16),
        jax.ShapeDtypeStruct((rows, WIDTH_A), BF16),
        jax.ShapeDtypeStruct((rows, WIDTH_A), BF16),
        jax.ShapeDtypeStruct((rows, 5 * WIDTH_B), F32),
        jax.ShapeDtypeStruct((rows, WIDTH_C), BF16),
    )
    out_specs = (
        pl.BlockSpec((tm, WIDTH_A), lambda i: (i, 0)),
        pl.BlockSpec((tm, WIDTH_A), lambda i: (i, 0)),
        pl.BlockSpec((tm, WIDTH_A), lambda i: (i, 0)),
        pl.BlockSpec((tm, 5 * WIDTH_B), lambda i: (i, 0)),
        pl.BlockSpec((tm, WIDTH_C), lambda i: (i, 0)),
    )
    return pl.pallas_call(
        functools.partial(_inproj_kernel, rope),
        grid=(rows // tm,),
        in_specs=in_specs,
        out_specs=out_specs,
        out_shape=out_shape,
        compiler_params=_params(("parallel",)),
        name="inproj_rope" if rope else "inproj",
    )(*args)


def _attn_kernel(nseg, hps, tq, sub, out_scale, lam_ref, q_ref, *rest):
    k_refs = rest[:nseg]
    v_refs = rest[nseg:2 * nseg]
    w_ref, o_ref = rest[2 * nseg:]
    lam = lam_ref[0, 0]
    wn = w_ref[...] * out_scale
    lane = lax.broadcasted_iota(jnp.int32, (sub, DV_A), 1)
    for hh in range(hps):
        hl = slice(hh * DV_A, (hh + 1) * DV_A)
        vexts = []
        for v_ref in v_refs:
            v = v_ref[:, hl]
            vl = lax.broadcasted_iota(jnp.int32, v.shape, 1)
            vexts.append(jnp.concatenate([v, jnp.where(vl == 0, 1.0, 0.0).astype(BF16)], axis=1))
        for r0 in range(0, tq, sub):
            q = q_ref[r0:r0 + sub, hl].astype(F32)
            qs = jnp.concatenate([jnp.where(lane < DH_A, q, 0.0), jnp.where(lane >= DH_A, q, 0.0)],
                                 axis=0).astype(BF16)
            scores = [_dot_nt(qs, k_ref[:, hl]) for k_ref in k_refs]
            mx = jnp.max(scores[0], axis=-1, keepdims=True)
            for s in scores[1:]:
                mx = jnp.maximum(mx, jnp.max(s, axis=-1, keepdims=True))
            acc = sum(_dot(exp2(s - mx).astype(BF16), vext) for s, vext in zip(scores, vexts))
            r = acc[:, :DV_A] / acc[:, DV_A:DV_A + 1]
            o = r[:sub] - lam * r[sub:]
            o_ref[r0:r0 + sub, hl] = (o * _rms_scale(o) * wn).astype(BF16)


def _attention(lam, q, ks, vs, subln_w, li, out_scale, hps, tq, sub):
    b, lq, _ = q.shape
    nseg = len(ks)
    w = hps * DV_A
    in_specs = [
        pl.BlockSpec(memory_space=pltpu.SMEM),
        pl.BlockSpec((None, tq, w), lambda bi, h, i: (bi, i, h)),
    ]
    for arr in list(ks) + list(vs):
        in_specs.append(pl.BlockSpec((None, arr.shape[1], w), lambda bi, h, i: (bi, 0, h)))
    in_specs.append(pl.BlockSpec((None, 1, DV_A), lambda bi, h, i: (li, 0, 0)))
    return pl.pallas_call(
        functools.partial(_attn_kernel, nseg, hps, tq, sub, out_scale),
        grid=(b, N_HEADS_A // hps, lq // tq),
        in_specs=in_specs,
        out_specs=pl.BlockSpec((None, tq, w), lambda bi, h, i: (bi, i, h)),
        out_shape=jax.ShapeDtypeStruct((b, lq, WIDTH_A), BF16),
        compiler_params=_params(("parallel", "parallel", "arbitrary")),
        name="diff_attention_%dseg" % nseg,
    )(lam, q, *ks, *vs, subln_w)


def _hgrn_constants():
    c = HG_CHUNK
    t = np.arange(c)[:, None]
    r = np.arange(c)[None, :]
    amask = []
    lomask = []
    for w in HG_WIDTHS:
        hi = (t % (2 * w)) >= w
        same_pair = (t // (2 * w)) == (r // (2 * w))
        amask.append(same_pair & (hi != ((r % (2 * w)) >= w)))
        lomask.append(np.broadcast_to(~hi, (c, c)))
    psum = np.zeros((2 * c, 2 * c), np.float32)
    psum[:c, :c] = r <= t
    psum[c:, c:] = r >= t
    tile = lambda a: np.tile(np.asarray(a, np.float32), (1, N_HEADS_B))
    amask = np.stack([tile(a) for a in amask])
    lomask = np.stack([tile(a) for a in lomask])
    hd = np.arange(WIDTH_B) // DK_B
    bd = (hd[:, None] == hd[None, :]).astype(np.float32)
    return (jnp.asarray(psum, BF16), jnp.asarray(amask, F32), jnp.asarray(lomask, F32), jnp.asarray(bd, F32))


def _hgrn_kernel(nc, states_only, q_ref, gf_ref, gb_ref, v_ref, og_ref, psum_ref, amask_ref, lomask_ref,
                 bd_ref, hw_ref, sf0_ref, sb0_ref, *outs_and_scratch):
    if states_only:
        rec_ref = None
        sf_ref, sb_ref = outs_and_scratch[:2]
        scratch = outs_and_scratch[2:]
    else:
        rec_ref, sf_ref, sb_ref = outs_and_scratch[:3]
        scratch = outs_and_scratch[3:]
    (sfp_ref, sbn_ref, cf_ref, cbs_ref, qa_ref, ka_ref, qb_ref, kb_ref, aa_ref, pa_ref, ab_ref, pb_ref, oa_ref,
     ob_ref) = scratch
    c = HG_CHUNK
    bd = bd_ref[...]
    bd_bf = bd.astype(BF16)
    lane_head = lax.broadcasted_iota(jnp.int32, (c, WIDTH_B), 1) // DK_B

    def stack_heads(a):
        return jnp.concatenate([jnp.where(lane_head == h, a, 0.0).astype(BF16) for h in range(N_HEADS_B)],
                               axis=0)

    def rows(ref, n):
        if isinstance(n, int):
            return ref[n * c:(n + 1) * c, :]
        return ref[pl.ds(pl.multiple_of(n * c, c), c), :]

    sf_ref[...] = sf0_ref[...]
    sb_ref[...] = sb0_ref[...]

    def decayed_keys(i, kt_ref, dec_ref):
        nf = i
        nb = nc - 1 - i
        gf = rows(gf_ref, nf)
        gb = rows(gb_ref, nb)
        g = jnp.concatenate([gf, gb], axis=0)
        sums = sum(_dot(psum_ref[...], part) for part in _split3(g))
        cf = sums[:c]
        cbs = sums[c:]
        cf_ref[nf] = cf * HG_UNIT
        cbs_ref[nb] = cbs * HG_UNIT
        tot_f = cf[c - 1:c]
        tot_b = cbs[0:1]
        kt_ref[0] = (1.0 - jnp.exp(gf)) * jnp.exp(tot_f - cf)
        kt_ref[1] = (1.0 - jnp.exp(gb)) * jnp.exp(tot_b - cbs)
        dec_ref[0:1, :] = jnp.exp(tot_f)
        dec_ref[8:9, :] = jnp.exp(tot_b)

    def apply_step(i, kt_ref, dec_ref):
        nf = i
        nb = nc - 1 - i
        upd_f = _dot_tn(rows(v_ref, nf).astype(BF16), kt_ref[0].astype(BF16)) * bd
        upd_b = _dot_tn(rows(v_ref, nb).astype(BF16), kt_ref[1].astype(BF16)) * bd
        stf = sf_ref[...]
        sfp_ref[nf] = stf.astype(BF16)
        sf_ref[...] = stf * dec_ref[0:1, :] + upd_f
        stb = sb_ref[...]
        sbn_ref[nb] = stb.astype(BF16)
        sb_ref[...] = stb * dec_ref[8:9, :] + upd_b

    pre = ((qa_ref, oa_ref), (qb_ref, ob_ref))
    decayed_keys(0, *pre[0])

    per_trip = min(HG_STEPS_PER_TRIP, nc)

    def pre_body(j, carry):
        for u in range(per_trip):
            i = per_trip * j + u
            apply_step(i, *pre[u % 2])
            decayed_keys(jnp.minimum(i + 1, nc - 1), *pre[1 - u % 2])
        return carry

    lax.fori_loop(0, nc // per_trip, pre_body, 0)
    if states_only:
        return

    def prepare(n, qop_ref, kop_ref):
        q = rows(q_ref, n)
        gf = rows(gf_ref, n) * HG_UNIT
        gb = rows(gb_ref, n) * HG_UNIT
        cf = cf_ref[n]
        cbs = cbs_ref[n]
        kf = 1.0 - exp2(gf)
        kb = 1.0 - exp2(gb)
        for lvl, w in enumerate(HG_WIDTHS):
            lo = lomask_ref[lvl] > 0.5
            if 2 * w >= 8:
                anchor = lambda a, r: jnp.broadcast_to(
                    a.reshape(c // (2 * w), 2 * w, WIDTH_B)[:, r:r + 1, :],
                    (c // (2 * w), 2 * w, WIDTH_B)).reshape(c, WIDTH_B)
                d_f = cf - anchor(cf, w - 1)
                d_b = cbs - anchor(cbs, w)
                eq = exp2(jnp.where(lo, d_b, d_f))
                ek = exp2(-jnp.where(lo, d_f, d_b))
            elif w == 2:
                r4 = lax.broadcasted_iota(jnp.int32, (c, WIDTH_B), 0) & 3
                gf_up, gf_dn = pltpu.roll(gf, c - 1, 0), pltpu.roll(gf, 1, 0)
                gb_up, gb_dn = pltpu.roll(gb, c - 1, 0), pltpu.roll(gb, 1, 0)
                eq = exp2(jnp.where(r4 == 0, gb + gb_up, jnp.where(r4 == 1, gb,
                              jnp.where(r4 == 2, gf, gf + gf_dn))))
                ek = exp2(jnp.where(r4 == 0, gf_up, jnp.where(r4 == 3, gb_dn, 0.0)))
            else:
                eq = exp2(jnp.where(lo, gb, gf))
                ek = None
            klev = jnp.where(lo, kf, kb)
            qop_ref[lvl] = q * eq
            kop_ref[lvl] = klev if ek is None else klev * ek
        qop_ref[HG_LEVELS] = q * exp2(cf)
        qop_ref[HG_LEVELS + 1] = q * exp2(cbs)
        qop_ref[HG_LEVELS + 2] = q * (kf + kb)

    def scores(n, qop_ref, kop_ref, a_ref, part_ref):
        a_all = jnp.zeros((c, N_HEADS_B * c), F32)
        for lvl in range(HG_LEVELS):
            a_all = a_all + _dot_nt(qop_ref[lvl].astype(BF16), stack_heads(kop_ref[lvl])) * amask_ref[lvl]
        a_ref[...] = a_all
        part = _dot(qop_ref[HG_LEVELS + 2].astype(BF16), bd_bf) * rows(v_ref, n)
        part = part + _dot_nt(qop_ref[HG_LEVELS].astype(BF16), sfp_ref[n])
        part_ref[...] = part + _dot_nt(qop_ref[HG_LEVELS + 1].astype(BF16), sbn_ref[n])

    def values(n, a_ref, part_ref, o_ref):
        o_ref[...] = (_dot(a_ref[...].astype(BF16), stack_heads(rows(v_ref, n)))
                      + part_ref[...])

    def finish(n, o_ref):
        o = o_ref[...]
        sq_h, sq_l = _split2(o * o)
        ms = _dot(jnp.concatenate([sq_h, sq_l], axis=0), bd_bf)
        ms = (ms[:c] + ms[c:]) * (1.0 / DV_B)
        og = rows(og_ref, n)
        rec = o * lax.rsqrt(ms + EPS) * hw_ref[...] * (og * jax.nn.sigmoid(og))
        rec_ref[pl.ds(pl.multiple_of(n * c, c), c), :] = rec.astype(BF16)

    ops = ((qa_ref, ka_ref), (qb_ref, kb_ref))
    sc = ((aa_ref, pa_ref), (ab_ref, pb_ref))
    ob = (oa_ref, ob_ref)
    last = nc - 1

    def step(m, par, clamp):
        idx = (lambda i: jnp.minimum(i, last)) if clamp else (lambda i: i)
        if clamp or m >= 0:
            finish(m, ob[par])
        if clamp or m + 1 >= 0:
            values(idx(m + 1), *sc[1 - par], ob[1 - par])
        if clamp or m + 2 >= 0:
            scores(idx(m + 2), *ops[par], *sc[par])
        prepare(idx(m + 3), *ops[1 - par])

    for m in (-3, -2, -1):
        step(m, m % 2, False)

    per_trip = min(HG_STEPS_PER_TRIP, nc)

    def body(j, carry):
        for u in range(per_trip):
            step(per_trip * j + u, u % 2, True)
        return carry

    lax.fori_loop(0, nc // per_trip, body, 0)


def _hgrn(hg, hconst, hw, sf0, sb0, li, states_only=False):
    b, l, _ = hg.shape
    nc = l // HG_CHUNK
    psum, amask, lomask, bd = hconst
    col = lambda j: pl.BlockSpec((None, l, WIDTH_B), lambda bi: (bi, 0, j))
    full = lambda a: _const_spec(a.shape, lambda bi: (0,) * a.ndim)
    st_spec = pl.BlockSpec((None, WIDTH_B, WIDTH_B), lambda bi: (bi, 0, 0))
    state_scratch = pltpu.VMEM((nc, WIDTH_B, WIDTH_B), BF16)
    sums_scratch = pltpu.VMEM((nc, HG_CHUNK, WIDTH_B), F32)
    q_operands = pltpu.VMEM((HG_LEVELS + 3, HG_CHUNK, WIDTH_B), F32)
    k_operands = pltpu.VMEM((HG_LEVELS, HG_CHUNK, WIDTH_B), F32)
    chunk_f32 = pltpu.VMEM((HG_CHUNK, WIDTH_B), F32)
    out_specs = [pl.BlockSpec((None, l, WIDTH_B), lambda bi: (bi, 0, 0)), st_spec, st_spec]
    out_shape = [jax.ShapeDtypeStruct((b, l, WIDTH_B), BF16),
                 jax.ShapeDtypeStruct((b, WIDTH_B, WIDTH_B), F32),
                 jax.ShapeDtypeStruct((b, WIDTH_B, WIDTH_B), F32)]
    if states_only:
        out_specs, out_shape = out_specs[1:], out_shape[1:]
    outs = pl.pallas_call(
        functools.partial(_hgrn_kernel, nc, states_only),
        grid=(b,),
        in_specs=[col(0), col(1), col(2), col(3), col(4), full(psum), full(amask),
                  full(lomask), full(bd), pl.BlockSpec((None, 1, WIDTH_B), lambda bi: (li, 0, 0)),
                  st_spec, st_spec],
        out_specs=tuple(out_specs),
        out_shape=tuple(out_shape),
        scratch_shapes=[state_scratch, state_scratch, sums_scratch, sums_scratch,
                        q_operands, k_operands, q_operands, k_operands,
                        chunk_f32, chunk_f32, chunk_f32, chunk_f32, chunk_f32, chunk_f32],
        compiler_params=_params(("parallel",)),
        name="hgrn_states_l%d" % l if states_only else "hgrn_l%d" % l,
    )(hg, hg, hg, hg, hg, psum, amask, lomask, bd, hw, sf0, sb0)
    return (None,) + tuple(outs) if states_only else tuple(outs)


def _dft_constants(l):
    k = np.arange(FN_GROUP_DIM)
    ang = 2.0 * np.pi * ((k[:, None] * k[None, :]) % FN_GROUP_DIM) / FN_GROUP_DIM
    eye = np.eye(FN_GROUPS)
    cs = np.concatenate([np.kron(eye, np.cos(ang)), np.kron(eye, np.sin(ang))], axis=1)
    n = np.arange(l)
    angl = 2.0 * np.pi * ((n[:, None] * n[None, :]) % l) / l
    return tuple(jnp.asarray(a, F32).astype(BF16) for a in (cs, np.cos(angl), np.sin(angl)))


def _fnet_kernel(scale, u_ref, cs_ref, cl_ref, sl_ref, wf_ref, o_ref):
    t = _dot(u_ref[...], cs_ref[...])
    uc = t[:, :WIDTH_C].astype(BF16)
    us = t[:, WIDTH_C:].astype(BF16)
    y = (_dot(cl_ref[...], uc) - _dot(sl_ref[...], us)) * scale
    o_ref[...] = _dot(y.astype(BF16), wf_ref[...]).astype(BF16)


def _fnet(uf, dft, wf_bf, li):
    b, l, _ = uf.shape
    cs, cl, sl = dft
    scale = 1.0 / math.sqrt(l * FN_GROUP_DIM)
    return pl.pallas_call(
        functools.partial(_fnet_kernel, scale),
        grid=(b,),
        in_specs=[
            pl.BlockSpec((None, l, WIDTH_C), lambda bi: (bi, 0, 0)),
            _const_spec(cs.shape, lambda bi: (0, 0)),
            _const_spec(cl.shape, lambda bi: (0, 0)),
            _const_spec(sl.shape, lambda bi: (0, 0)),
            _const_spec((None, WIDTH_C, WIDTH_C), lambda bi: (li, 0, 0)),
        ],
        out_specs=pl.BlockSpec((None, l, WIDTH_C), lambda bi: (bi, 0, 0)),
        out_shape=jax.ShapeDtypeStruct((b, l, WIDTH_C), BF16),
        compiler_params=_params(("parallel",)),
        name="fnet_l%d" % l,
    )(uf, cs, cl, sl, wf_bf)


def _mix_ffn_kernel(tm, tpb, final, att_ref, attp_ref, attn_ref, rec_ref, recp_ref, recn_ref, four_ref,
                    fourp_ref, fourn_ref, x_ref, xp_ref, xn_ref, mod_ref, nw_ref, wo_ref, wup_ref, cw_ref,
                    cb_ref, wdn_ref, fw_ref, o_ref, a_ref):
    t = pl.program_id(0) % tpb
    ext = tm + 2 * HALO
    m = mod_ref[...]

    def with_halo(prev_ref, main_ref, next_ref):
        return jnp.concatenate([prev_ref[...], main_ref[...], next_ref[...]], axis=0)

    mix = jnp.concatenate([with_halo(attp_ref, att_ref, attn_ref), with_halo(recp_ref, rec_ref, recn_ref),
                           with_halo(fourp_ref, four_ref, fourn_ref)], axis=1)
    x1e = with_halo(xp_ref, x_ref, xn_ref) + m[2:3] * _dot(mix, wo_ref[...])
    h = (x1e * _rms_scale(x1e) * nw_ref[...]) * (1.0 + m[4:5]) + m[3:4]
    row = lax.broadcasted_iota(jnp.int32, (ext, 1), 0)
    outside = ((row < HALO) & (t == 0)) | ((row >= HALO + tm) & (t == tpb - 1))
    hext = jnp.where(outside, 0.0, h).astype(BF16)

    def conv(u, c0):
        w = cw_ref[:, c0:c0 + FFN_COLS]
        up = pltpu.roll(u, 1, 0)
        dn = pltpu.roll(u, ext - 1, 0)
        r = up * w[0:1] + u * w[1:2] + dn * w[2:3] + cb_ref[:, c0:c0 + FFN_COLS]
        return r[HALO:HALO + tm]

    for j in range(D_FF // FFN_COLS):
        c0 = j * FFN_COLS
        gate = conv(_dot(hext, wup_ref[:, c0:c0 + FFN_COLS]), c0)
        val = conv(_dot(hext, wup_ref[:, D_FF + c0:D_FF + c0 + FFN_COLS]), D_FF + c0)
        a_ref[:, c0:c0 + FFN_COLS] = (gate * jax.nn.sigmoid(gate) * val).astype(BF16)
    x2 = x1e[HALO:HALO + tm] + m[5:6] * _dot(a_ref[...], wdn_ref[...])
    if final:
        x2 = x2 * _rms_scale(x2) * fw_ref[...]
    o_ref[...] = x2


def _mix_ffn(att, rec, four, x2d, mod4, nw2, wo_bf, wup_bf, conv_w, conv_b, wdn_bf, final_w, li, seg_len,
             mod_row, tm, final):
    rows = x2d.shape[0]
    tpb = seg_len // tm
    hb = tm // HALO
    nhb = rows // HALO
    mod_map = (lambda i: (li, i // tpb, 0, 0)) if mod_row is None else (lambda i: (li, mod_row, 0, 0))

    def row_specs(w):
        return [pl.BlockSpec((tm, w), lambda i: (i, 0)),
                pl.BlockSpec((HALO, w), lambda i: (jnp.maximum(i * hb - 1, 0), 0)),
                pl.BlockSpec((HALO, w), lambda i: (jnp.minimum((i + 1) * hb, nhb - 1), 0))]

    return pl.pallas_call(
        functools.partial(_mix_ffn_kernel, tm, tpb, final),
        grid=(rows // tm,),
        in_specs=row_specs(WIDTH_A) + row_specs(WIDTH_B) + row_specs(WIDTH_C) + row_specs(D_MODEL) + [
            pl.BlockSpec((None, None, 6, D_MODEL), mod_map),
            pl.BlockSpec((None, 1, D_MODEL), lambda i: (li, 0, 0)),
            _const_spec((None, D_MODEL, D_MODEL), lambda i: (li, 0, 0)),
            _const_spec((None, D_MODEL, 2 * D_FF), lambda i: (li, 0, 0)),
            pl.BlockSpec((None, 3, 2 * D_FF), lambda i: (li, 0, 0)),
            pl.BlockSpec((None, 1, 2 * D_FF), lambda i: (li, 0, 0)),
            _const_spec((None, D_FF, D_MODEL), lambda i: (li, 0, 0)),
            pl.BlockSpec((1, D_MODEL), lambda i: (0, 0)),
        ],
        out_specs=pl.BlockSpec((tm, D_MODEL), lambda i: (i, 0)),
        out_shape=jax.ShapeDtypeStruct((rows, D_MODEL), F32),
        scratch_shapes=[pltpu.VMEM((tm, D_FF), BF16)],
        compiler_params=_params(("parallel",)),
        name="mix_convffn_final" if final else "mix_convffn",
    )(att, att, att, rec, rec, rec, four, four, four, x2d, x2d, x2d, mod4, nw2, wo_bf, wup_bf, conv_w, conv_b,
      wdn_bf, final_w)


def _rope_tables(l):
    rows = l // GRID_W
    pos_r = np.repeat(np.arange(rows), GRID_W).astype(np.float32)
    pos_c = np.tile(np.arange(GRID_W), rows).astype(np.float32)
    half = DH_A // 2
    inv_freq = (ROPE_BASE ** (-np.arange(0, half, 2, dtype=np.float32) / half)).astype(np.float32)
    ang = np.concatenate([pos_r[:, None] * inv_freq] * 2 + [pos_c[:, None] * inv_freq] * 2, axis=-1)
    cos = np.cos(ang.astype(np.float32)).astype(np.float32)
    sin = np.sin(ang.astype(np.float32)).astype(np.float32)
    cos = np.tile(cos, (1, 2))
    sin = np.tile(sin, (1, 2))
    first = (np.arange(DV_A) % 32) < 16
    sin_a = np.where(first, -sin, 0.0).astype(np.float32)
    sin_b = np.where(first, 0.0, sin).astype(np.float32)
    return jnp.asarray(cos), jnp.asarray(sin_a), jnp.asarray(sin_b)


def kernel(x, c, ctx, c_ctx, w_ada, b_ada, norm1_w, norm2_w, w_in, lam_qk, subln_w, lb_param, hgrn_norm_w,
           w_fnet, w_out, w_up, conv_w, conv_b, w_down, final_norm_w):
    b, l, d = x.shape
    lc = ctx.shape[1]
    depth = w_ada.shape[0]
    mod_rows = 16
    assert d == D_MODEL and b < mod_rows
    assert l % INPROJ_ROWS == 0 and l % FFN_ROWS == 0 and l % ATTN_SUB == 0
    for seq in (l, lc):
        nchunks = seq // HG_CHUNK
        assert seq % (2 * HG_CHUNK) == 0 and nchunks % min(HG_STEPS_PER_TRIP, nchunks) == 0
    tm_l = INPROJ_ROWS
    tm_f = FFN_ROWS
    tm_c = lc

    lb_all = jnp.cumsum(jax.nn.softmax(lb_param.astype(F32), axis=0), axis=0)
    lb_all = lb_all - lb_all[0:1]
    lbc = jnp.stack([jnp.log(lb_all[:, 0]), jnp.log1p(-lb_all[:, 0]),
                     jnp.log(lb_all[:, 1]), jnp.log1p(-lb_all[:, 1])], axis=1)
    lq = lam_qk.astype(F32)
    lam_dyn = jnp.exp(jnp.sum(lq[:, 0] * lq[:, 1], axis=-1)) - jnp.exp(jnp.sum(lq[:, 2] * lq[:, 3], axis=-1))
    rope_tabs = _rope_tables(l)
    hconst = _hgrn_constants()
    dft_l = _dft_constants(l)
    dft_c = _dft_constants(lc)
    w_in_bf = w_in.astype(BF16)
    w_out_bf = w_out.astype(BF16)
    w_up_bf = w_up.astype(BF16)
    w_dn_bf = w_down.astype(BF16)
    w_fn_bf = w_fnet.astype(BF16)
    nw1 = norm1_w.reshape(depth, 1, d)
    nw2 = norm2_w.reshape(depth, 1, d)
    sub_w = subln_w.reshape(depth, 1, DV_A)
    hw = jnp.tile(hgrn_norm_w, (1, N_HEADS_B)).reshape(depth, 1, WIDTH_B)
    cb = conv_b.reshape(depth, 1, 2 * D_FF)
    fw = final_norm_w.reshape(1, d)

    cstack = jnp.concatenate([c, c_ctx[None, :], jnp.zeros((mod_rows - b - 1, d), F32)], axis=0)
    mod4 = _modulation(cstack, w_ada, b_ada).reshape(depth, mod_rows, 6, d)
    ctx_row = b

    zeros_state = jnp.zeros((b, WIDTH_B, WIDTH_B), F32)
    xl = x.reshape(b * l, d)
    xc = ctx.reshape(b * lc, d)
    for li in range(depth):
        last = li == depth - 1
        lam_init = 0.8 - 0.6 * math.exp(-0.3 * li)
        lam = (lam_dyn[li] + lam_init).reshape(1, 1)

        qa, ka, va, hg, uf = _inproj(xl, mod4, nw1, w_in_bf, lbc, rope_tabs, li, l, None, tm_l)
        qc, kc, vc, hgc, ufc = _inproj(xc, mod4, nw1, w_in_bf, lbc, None, li, lc, ctx_row,
                                       math.gcd(b * lc, tm_l))
        r3 = lambda a, n: a.reshape(b, n, a.shape[-1])

        att = _attention(lam, r3(qa, l), [r3(kc, lc), r3(ka, l)], [r3(vc, lc), r3(va, l)], sub_w, li,
                         1.0 - lam_init, 1, l, ATTN_SUB)
        rec_c, sf, sb = _hgrn(r3(hgc, lc), hconst, hw, zeros_state, zeros_state, li, states_only=last)
        rec, _, _ = _hgrn(r3(hg, l), hconst, hw, sf, sb, li)
        four = _fnet(r3(uf, l), dft_l, w_fn_bf, li)

        xl = _mix_ffn(att.reshape(b * l, -1), rec.reshape(b * l, -1), four.reshape(b * l, -1), xl, mod4, nw2,
                      w_out_bf, w_up_bf, conv_w, cb, w_dn_bf, fw, li, l, None, tm_f, last)

        if not last:
            att_c = _attention(lam, r3(qc, lc), [r3(kc, lc)], [r3(vc, lc)], sub_w, li, 1.0 - lam_init,
                               N_HEADS_A, lc, min(ATTN_SUB, lc) // 2)
            four_c = _fnet(r3(ufc, lc), dft_c, w_fn_bf, li)
            xc = _mix_ffn(att_c.reshape(b * lc, -1), rec_c.reshape(b * lc, -1), four_c.reshape(b * lc, -1), xc,
                          mod4, nw2, w_out_bf, w_up_bf, conv_w, cb, w_dn_bf, fw, li, lc, ctx_row, tm_c, False)
    return xl.reshape(b, l, d)
```

```python
import functools
import math

import numpy as np
import jax
import jax.numpy as jnp
from jax import lax
from jax.experimental import pallas as pl
from jax.experimental.pallas import tpu as pltpu

F32 = jnp.float32
BF16 = jnp.bfloat16

D_MODEL = 1024
GRID_W = 64
N_HEADS_A = 4
DH_A = 64
DV_A = 2 * DH_A
WIDTH_A = N_HEADS_A * DV_A
N_HEADS_B = 4
DK_B = 64
DV_B = 64
WIDTH_B = N_HEADS_B * DV_B
FN_GROUPS = 4
FN_GROUP_DIM = 64
WIDTH_C = FN_GROUPS * FN_GROUP_DIM
PROJ_WIDTH = 3 * WIDTH_A + 5 * WIDTH_B + WIDTH_C
D_FF = 2816
ROPE_BASE = 10000.0
EPS = 1e-6
LOG2E = math.log2(math.e)
exp2 = jnp.exp2
HG_UNIT = LOG2E

V7X_LANES = 128
V7X_BF16_SUBLANES = 16
V7X_VMEM_BYTES = 64 * 1024 * 1024
VMEM_LIMIT = V7X_VMEM_BYTES - 8 * 1024 * 1024

MOD_COLS = 1536
INPROJ_ROWS = 1024
INPROJ_SUB = 256
ATTN_SUB = 256
ATTN_AHEAD = 2
FFN_ROWS = 1024
FFN_COLS = 256
HALO = V7X_BF16_SUBLANES
HG_CHUNK = 64
HG_LEVELS = 6
HG_WIDTHS = tuple(HG_CHUNK >> (l + 1) for l in range(HG_LEVELS))
HG_STEPS_PER_TRIP = 8


def _params(sem):
    return pltpu.CompilerParams(dimension_semantics=sem, vmem_limit_bytes=VMEM_LIMIT)


def _const_spec(shape, index_map):
    return pl.BlockSpec(shape, index_map, pipeline_mode=pl.Buffered(1))


def _dot(a, b):
    return jnp.dot(a, b, preferred_element_type=F32)


def _dot_nt(a, b):
    return lax.dot_general(a, b, (((1,), (1,)), ((), ())), preferred_element_type=F32)


def _dot_tn(a, b):
    return lax.dot_general(a, b, (((0,), (0,)), ((), ())), preferred_element_type=F32)


def _split3(x):
    h = x.astype(BF16)
    r = x - h.astype(F32)
    m = r.astype(BF16)
    return h, m, (r - m.astype(F32)).astype(BF16)


def _split2(x):
    h = x.astype(BF16)
    return h, (x - h.astype(F32)).astype(BF16)


def _rms_scale(x):
    return lax.rsqrt(jnp.mean(x * x, axis=-1, keepdims=True) + EPS)


def _mod_kernel(c_ref, w_ref, b_ref, o_ref):
    c = c_ref[...]
    a = (c * jax.nn.sigmoid(c)).astype(BF16)
    o_ref[0] = _dot(a, w_ref[0].astype(BF16)) + b_ref[0]


def _modulation(cstack, w_ada, b_ada):
    depth, _, n = w_ada.shape
    tn = MOD_COLS
    rows = cstack.shape[0]
    return pl.pallas_call(
        _mod_kernel,
        grid=(depth, n // tn),
        in_specs=[
            pl.BlockSpec((rows, D_MODEL), lambda l, j: (0, 0)),
            pl.BlockSpec((1, D_MODEL, tn), lambda l, j: (l, 0, j)),
            pl.BlockSpec((1, 1, tn), lambda l, j: (l, 0, j)),
        ],
        out_specs=pl.BlockSpec((1, rows, tn), lambda l, j: (l, 0, j)),
        out_shape=jax.ShapeDtypeStruct((depth, rows, n), F32),
        compiler_params=_params(("arbitrary", "arbitrary")),
        name="modulation",
    )(cstack, w_ada, b_ada.reshape(depth, 1, n))


def _log_forget(z, log_lb, log1m_lb):
    ls = jnp.minimum(z, 0.0) - jnp.log(1.0 + jnp.exp(-jnp.abs(z)))
    b = log1m_lb + ls
    return jnp.maximum(log_lb, b) + jnp.log(1.0 + jnp.exp(-jnp.abs(log_lb - b)))


def _inproj_kernel(rope, x_ref, mod_ref, nw_ref, w_ref, lbc_ref, *rest):
    if rope:
        cos_ref, sa_ref, sb_ref, qa_ref, ka_ref, va_ref, hg_ref, uf_ref = rest
    else:
        qa_ref, ka_ref, va_ref, hg_ref, uf_ref = rest
    m = mod_ref[...]
    lbc = lbc_ref[...]
    c0 = 3 * WIDTH_A
    for r0 in range(0, x_ref.shape[0], INPROJ_SUB):
        rs = slice(r0, r0 + INPROJ_SUB)
        x = x_ref[rs, :]
        y = x * _rms_scale(x) * nw_ref[...]
        h = (y * (1.0 + m[1:2]) + m[0:1]).astype(BF16)

        def proj(col, n):
            return _dot(h, w_ref[:, col:col + n])

        for base, ref, scale in ((0, qa_ref, DH_A ** -0.5 * LOG2E), (WIDTH_A, ka_ref, None)):
            t = proj(base, WIDTH_A)
            for hd in range(N_HEADS_A):
                th = t[:, hd * DV_A:(hd + 1) * DV_A]
                if rope:
                    th = (th * cos_ref[rs, :]
                          + pltpu.roll(th, V7X_LANES - 16, 1) * sa_ref[rs, :]
                          + pltpu.roll(th, 16, 1) * sb_ref[rs, :])
                if scale is not None:
                    th = th * scale
                ref[rs, hd * DV_A:(hd + 1) * DV_A] = th.astype(BF16)
        va_ref[rs, :] = proj(2 * WIDTH_A, WIDTH_A).astype(BF16)
        hg_ref[rs, 0:WIDTH_B] = proj(c0, WIDTH_B)
        for dirn in range(2):
            z = proj(c0 + WIDTH_B * (1 + dirn), WIDTH_B)
            hg_ref[rs, WIDTH_B * (1 + dirn):WIDTH_B * (2 + dirn)] = _log_forget(
                z, lbc[2 * dirn:2 * dirn + 1], lbc[2 * dirn + 1:2 * dirn + 2])
        hg_ref[rs, 3 * WIDTH_B:5 * WIDTH_B] = proj(c0 + 3 * WIDTH_B, 2 * WIDTH_B)
        uf_ref[rs, :] = proj(c0 + 5 * WIDTH_B, WIDTH_C).astype(BF16)


def _inproj(x2d, mod4, nw, w_bf, lbc, rope_tabs, li, seg_len, mod_row, tm):
    rows = x2d.shape[0]
    tpb = seg_len // tm
    rope = rope_tabs is not None
    mod_map = (lambda i: (li, i // tpb, 0, 0)) if mod_row is None else (lambda i: (li, mod_row, 0, 0))
    in_specs = [
        pl.BlockSpec((tm, D_MODEL), lambda i: (i, 0)),
        pl.BlockSpec((None, None, 6, D_MODEL), mod_map),
        pl.BlockSpec((None, 1, D_MODEL), lambda i: (li, 0, 0)),
        _const_spec((None, D_MODEL, PROJ_WIDTH), lambda i: (li, 0, 0)),
        pl.BlockSpec((None, 4, WIDTH_B), lambda i: (li, 0, 0)),
    ]
    args = [x2d, mod4, nw, w_bf, lbc]
    if rope:
        in_specs += [pl.BlockSpec((tm, DV_A), lambda i: (i % tpb, 0))] * 3
        args += list(rope_tabs)
    out_shape = (
        jax.ShapeDtypeStruct((rows, WIDTH_A), BF16),
        jax.ShapeDtypeStruct((rows, WIDTH_A), BF16),
        jax.ShapeDtypeStruct((rows, WIDTH_A), BF16),
        jax.ShapeDtypeStruct((rows, 5 * WIDTH_B), F32),
        jax.ShapeDtypeStruct((rows, WIDTH_C), BF16),
    )
    out_specs = (
        pl.BlockSpec((tm, WIDTH_A), lambda i: (i, 0)),
        pl.BlockSpec((tm, WIDTH_A), lambda i: (i, 0)),
        pl.BlockSpec((tm, WIDTH_A), lambda i: (i, 0)),
        pl.BlockSpec((tm, 5 * WIDTH_B), lambda i: (i, 0)),
        pl.BlockSpec((tm, WIDTH_C), lambda i: (i, 0)),
    )
    return pl.pallas_call(
        functools.partial(_inproj_kernel, rope),
        grid=(rows // tm,),
        in_specs=in_specs,
        out_specs=out_specs,
        out_shape=out_shape,
        compiler_params=_params(("parallel",)),
        name="inproj_rope" if rope else "inproj",
    )(*args)


def _attn_kernel(nseg, hps, tq, sub, out_scale, lam_ref, q_ref, *rest):
    k_refs = rest[:nseg]
    v_refs = rest[nseg:2 * nseg]
    w_ref, o_ref = rest[2 * nseg:]
    lam = lam_ref[0, 0]
    wn = w_ref[...] * out_scale
    lane = lax.broadcasted_iota(jnp.int32, (sub, DV_A), 1)
    for hh in range(hps):
        hl = slice(hh * DV_A, (hh + 1) * DV_A)
        vexts = []
        for v_ref in v_refs:
            v = v_ref[:, hl]
            vl = lax.broadcasted_iota(jnp.int32, v.shape, 1)
            vexts.append(jnp.concatenate([v, jnp.where(vl == 0, 1.0, 0.0).astype(BF16)], axis=1))
        def block_scores(r0):
            q = q_ref[r0:r0 + sub, hl].astype(F32)
            qs = jnp.concatenate([jnp.where(lane < DH_A, q, 0.0), jnp.where(lane >= DH_A, q, 0.0)],
                                 axis=0).astype(BF16)
            return [_dot_nt(qs, k_ref[:, hl]) for k_ref in k_refs]

        starts = list(range(0, tq, sub))
        ahead = [block_scores(r) for r in starts[:ATTN_AHEAD]]
        for bi, r0 in enumerate(starts):
            scores = ahead.pop(0)
            if bi + ATTN_AHEAD < len(starts):
                ahead.append(block_scores(starts[bi + ATTN_AHEAD]))
            mx = jnp.max(scores[0], axis=-1, keepdims=True)
            for s in scores[1:]:
                mx = jnp.maximum(mx, jnp.max(s, axis=-1, keepdims=True))
            acc = sum(_dot(exp2(s - mx).astype(BF16), vext) for s, vext in zip(scores, vexts))
            r = acc[:, :DV_A] / acc[:, DV_A:DV_A + 1]
            o = r[:sub] - lam * r[sub:]
            o_ref[r0:r0 + sub, hl] = (o * _rms_scale(o) * wn).astype(BF16)


def _attention(lam, q, ks, vs, subln_w, li, out_scale, hps, tq, sub):
    b, lq, _ = q.shape
    nseg = len(ks)
    w = hps * DV_A
    in_specs = [
        pl.BlockSpec(memory_space=pltpu.SMEM),
        pl.BlockSpec((None, tq, w), lambda bi, h, i: (bi, i, h)),
    ]
    for arr in list(ks) + list(vs):
        in_specs.append(pl.BlockSpec((None, arr.shape[1], w), lambda bi, h, i: (bi, 0, h)))
    in_specs.append(pl.BlockSpec((None, 1, DV_A), lambda bi, h, i: (li, 0, 0)))
    return pl.pallas_call(
        functools.partial(_attn_kernel, nseg, hps, tq, sub, out_scale),
        grid=(b, N_HEADS_A // hps, lq // tq),
        in_specs=in_specs,
        out_specs=pl.BlockSpec((None, tq, w), lambda bi, h, i: (bi, i, h)),
        out_shape=jax.ShapeDtypeStruct((b, lq, WIDTH_A), BF16),
        compiler_params=_params(("parallel", "parallel", "arbitrary")),
        name="diff_attention_%dseg" % nseg,
    )(lam, q, *ks, *vs, subln_w)


def _hgrn_constants():
    c = HG_CHUNK
    t = np.arange(c)[:, None]
    r = np.arange(c)[None, :]
    amask = []
    lomask = []
    for w in HG_WIDTHS:
        hi = (t % (2 * w)) >= w
        same_pair = (t // (2 * w)) == (r // (2 * w))
        amask.append(same_pair & (hi != ((r % (2 * w)) >= w)))
        lomask.append(np.broadcast_to(~hi, (c, c)))
    psum = np.zeros((2 * c, 2 * c), np.float32)
    psum[:c, :c] = r <= t
    psum[c:, c:] = r >= t
    tile = lambda a: np.tile(np.asarray(a, np.float32), (1, N_HEADS_B))
    amask = np.stack([tile(a) for a in amask])
    lomask = np.stack([tile(a) for a in lomask])
    hd = np.arange(WIDTH_B) // DK_B
    bd = (hd[:, None] == hd[None, :]).astype(np.float32)
    return (jnp.asarray(psum, BF16), jnp.asarray(amask, F32), jnp.asarray(lomask, F32), jnp.asarray(bd, F32))


def _hgrn_kernel(nc, states_only, q_ref, gf_ref, gb_ref, v_ref, og_ref, psum_ref, amask_ref, lomask_ref,
                 bd_ref, hw_ref, sf0_ref, sb0_ref, *outs_and_scratch):
    if states_only:
        rec_ref = None
        sf_ref, sb_ref = outs_and_scratch[:2]
        scratch = outs_and_scratch[2:]
    else:
        rec_ref, sf_ref, sb_ref = outs_and_scratch[:3]
        scratch = outs_and_scratch[3:]
    (sfp_ref, sbn_ref, cf_ref, cbs_ref, qa_ref, ka_ref, qb_ref, kb_ref, aa_ref, pa_ref, ab_ref, pb_ref, oa_ref,
     ob_ref) = scratch
    c = HG_CHUNK
    bd = bd_ref[...]
    bd_bf = bd.astype(BF16)
    lane_head = lax.broadcasted_iota(jnp.int32, (c, WIDTH_B), 1) // DK_B

    def stack_heads(a):
        return jnp.concatenate([jnp.where(lane_head == h, a, 0.0).astype(BF16) for h in range(N_HEADS_B)],
                               axis=0)

    def rows(ref, n):
        if isinstance(n, int):
            return ref[n * c:(n + 1) * c, :]
        return ref[pl.ds(pl.multiple_of(n * c, c), c), :]

    sf_ref[...] = sf0_ref[...]
    sb_ref[...] = sb0_ref[...]

    def decayed_keys(i, kt_ref, dec_ref):
        nf = i
        nb = nc - 1 - i
        gf = rows(gf_ref, nf)
        gb = rows(gb_ref, nb)
        g = jnp.concatenate([gf, gb], axis=0)
        sums = sum(_dot(psum_ref[...], part) for part in _split3(g))
        cf = sums[:c]
        cbs = sums[c:]
        cf_ref[nf] = cf * HG_UNIT
        cbs_ref[nb] = cbs * HG_UNIT
        tot_f = cf[c - 1:c]
        tot_b = cbs[0:1]
        kt_ref[0] = (1.0 - jnp.exp(gf)) * jnp.exp(tot_f - cf)
        kt_ref[1] = (1.0 - jnp.exp(gb)) * jnp.exp(tot_b - cbs)
        dec_ref[0:1, :] = jnp.exp(tot_f)
        dec_ref[8:9, :] = jnp.exp(tot_b)

    def apply_step(i, kt_ref, dec_ref):
        nf = i
        nb = nc - 1 - i
        upd_f = _dot_tn(rows(v_ref, nf).astype(BF16), kt_ref[0].astype(BF16)) * bd
        upd_b = _dot_tn(rows(v_ref, nb).astype(BF16), kt_ref[1].astype(BF16)) * bd
        stf = sf_ref[...]
        sfp_ref[nf] = stf.astype(BF16)
        sf_ref[...] = stf * dec_ref[0:1, :] + upd_f
        stb = sb_ref[...]
        sbn_ref[nb] = stb.astype(BF16)
        sb_ref[...] = stb * dec_ref[8:9, :] + upd_b

    pre = ((qa_ref, oa_ref), (qb_ref, ob_ref))
    decayed_keys(0, *pre[0])

    per_trip = min(HG_STEPS_PER_TRIP, nc)

    def pre_body(j, carry):
        for u in range(per_trip):
            i = per_trip * j + u
            apply_step(i, *pre[u % 2])
            decayed_keys(jnp.minimum(i + 1, nc - 1), *pre[1 - u % 2])
        return carry

    lax.fori_loop(0, nc // per_trip, pre_body, 0)
    if states_only:
        return

    def prepare(n, qop_ref, kop_ref):
        q = rows(q_ref, n)
        gf = rows(gf_ref, n) * HG_UNIT
        gb = rows(gb_ref, n) * HG_UNIT
        cf = cf_ref[n]
        cbs = cbs_ref[n]
        kf = 1.0 - exp2(gf)
        kb = 1.0 - exp2(gb)
        for lvl, w in enumerate(HG_WIDTHS):
            lo = lomask_ref[lvl] > 0.5
            if 2 * w >= 8:
                anchor = lambda a, r: jnp.broadcast_to(
                    a.reshape(c // (2 * w), 2 * w, WIDTH_B)[:, r:r + 1, :],
                    (c // (2 * w), 2 * w, WIDTH_B)).reshape(c, WIDTH_B)
                d_f = cf - anchor(cf, w - 1)
                d_b = cbs - anchor(cbs, w)
                eq = exp2(jnp.where(lo, d_b, d_f))
                ek = exp2(-jnp.where(lo, d_f, d_b))
            elif w == 2:
                r4 = lax.broadcasted_iota(jnp.int32, (c, WIDTH_B), 0) & 3
                gf_up, gf_dn = pltpu.roll(gf, c - 1, 0), pltpu.roll(gf, 1, 0)
                gb_up, gb_dn = pltpu.roll(gb, c - 1, 0), pltpu.roll(gb, 1, 0)
                eq = exp2(jnp.where(r4 == 0, gb + gb_up, jnp.where(r4 == 1, gb,
                              jnp.where(r4 == 2, gf, gf + gf_dn))))
                ek = exp2(jnp.where(r4 == 0, gf_up, jnp.where(r4 == 3, gb_dn, 0.0)))
            else:
                eq = exp2(jnp.where(lo, gb, gf))
                ek = None
            klev = jnp.where(lo, kf, kb)
            qop_ref[lvl] = q * eq
            kop_ref[lvl] = klev if ek is None else klev * ek
        qop_ref[HG_LEVELS] = q * exp2(cf)
        qop_ref[HG_LEVELS + 1] = q * exp2(cbs)
        qop_ref[HG_LEVELS + 2] = q * (kf + kb)

    def scores(n, qop_ref, kop_ref, a_ref, part_ref):
        a_all = jnp.zeros((c, N_HEADS_B * c), F32)
        for lvl in range(HG_LEVELS):
            a_all = a_all + _dot_nt(qop_ref[lvl].astype(BF16), stack_heads(kop_ref[lvl])) * amask_ref[lvl]
        a_ref[...] = a_all
        part = _dot(qop_ref[HG_LEVELS + 2].astype(BF16), bd_bf) * rows(v_ref, n)
        part = part + _dot_nt(qop_ref[HG_LEVELS].astype(BF16), sfp_ref[n])
        part_ref[...] = part + _dot_nt(qop_ref[HG_LEVELS + 1].astype(BF16), sbn_ref[n])

    def values(n, a_ref, part_ref, o_ref):
        o_ref[...] = (_dot(a_ref[...].astype(BF16), stack_heads(rows(v_ref, n)))
                      + part_ref[...])

    def finish(n, o_ref):
        o = o_ref[...]
        sq_h, sq_l = _split2(o * o)
        ms = _dot(jnp.concatenate([sq_h, sq_l], axis=0), bd_bf)
        ms = (ms[:c] + ms[c:]) * (1.0 / DV_B)
        og = rows(og_ref, n)
        rec = o * lax.rsqrt(ms + EPS) * hw_ref[...] * (og * jax.nn.sigmoid(og))
        rec_ref[pl.ds(pl.multiple_of(n * c, c), c), :] = rec.astype(BF16)

    ops = ((qa_ref, ka_ref), (qb_ref, kb_ref))
    sc = ((aa_ref, pa_ref), (ab_ref, pb_ref))
    ob = (oa_ref, ob_ref)
    last = nc - 1

    def step(m, par, clamp):
        idx = (lambda i: jnp.minimum(i, last)) if clamp else (lambda i: i)
        if clamp or m >= 0:
            finish(m, ob[par])
        if clamp or m + 1 >= 0:
            values(idx(m + 1), *sc[1 - par], ob[1 - par])
        if clamp or m + 2 >= 0:
            scores(idx(m + 2), *ops[par], *sc[par])
        prepare(idx(m + 3), *ops[1 - par])

    for m in (-3, -2, -1):
        step(m, m % 2, False)

    per_trip = min(HG_STEPS_PER_TRIP, nc)

    def body(j, carry):
        for u in range(per_trip):
            step(per_trip * j + u, u % 2, True)
        return carry

    lax.fori_loop(0, nc // per_trip, body, 0)


def _hgrn(hg, hconst, hw, sf0, sb0, li, states_only=False):
    b, l, _ = hg.shape
    nc = l // HG_CHUNK
    psum, amask, lomask, bd = hconst
    col = lambda j: pl.BlockSpec((None, l, WIDTH_B), lambda bi: (bi, 0, j))
    full = lambda a: _const_spec(a.shape, lambda bi: (0,) * a.ndim)
    st_spec = pl.BlockSpec((None, WIDTH_B, WIDTH_B), lambda bi: (bi, 0, 0))
    state_scratch = pltpu.VMEM((nc, WIDTH_B, WIDTH_B), BF16)
    sums_scratch = pltpu.VMEM((nc, HG_CHUNK, WIDTH_B), F32)
    q_operands = pltpu.VMEM((HG_LEVELS + 3, HG_CHUNK, WIDTH_B), F32)
    k_operands = pltpu.VMEM((HG_LEVELS, HG_CHUNK, WIDTH_B), F32)
    chunk_f32 = pltpu.VMEM((HG_CHUNK, WIDTH_B), F32)
    out_specs = [pl.BlockSpec((None, l, WIDTH_B), lambda bi: (bi, 0, 0)), st_spec, st_spec]
    out_shape = [jax.ShapeDtypeStruct((b, l, WIDTH_B), BF16),
                 jax.ShapeDtypeStruct((b, WIDTH_B, WIDTH_B), F32),
                 jax.ShapeDtypeStruct((b, WIDTH_B, WIDTH_B), F32)]
    if states_only:
        out_specs, out_shape = out_specs[1:], out_shape[1:]
    outs = pl.pallas_call(
        functools.partial(_hgrn_kernel, nc, states_only),
        grid=(b,),
        in_specs=[col(0), col(1), col(2), col(3), col(4), full(psum), full(amask),
                  full(lomask), full(bd), pl.BlockSpec((None, 1, WIDTH_B), lambda bi: (li, 0, 0)),
                  st_spec, st_spec],
        out_specs=tuple(out_specs),
        out_shape=tuple(out_shape),
        scratch_shapes=[state_scratch, state_scratch, sums_scratch, sums_scratch,
                        q_operands, k_operands, q_operands, k_operands,
                        chunk_f32, chunk_f32, chunk_f32, chunk_f32, chunk_f32, chunk_f32],
        compiler_params=_params(("parallel",)),
        name="hgrn_states_l%d" % l if states_only else "hgrn_l%d" % l,
    )(hg, hg, hg, hg, hg, psum, amask, lomask, bd, hw, sf0, sb0)
    return (None,) + tuple(outs) if states_only else tuple(outs)


def _dft_constants(l):
    k = np.arange(FN_GROUP_DIM)
    ang = 2.0 * np.pi * ((k[:, None] * k[None, :]) % FN_GROUP_DIM) / FN_GROUP_DIM
    eye = np.eye(FN_GROUPS)
    cs = np.concatenate([np.kron(eye, np.cos(ang)), np.kron(eye, np.sin(ang))], axis=1)
    n = np.arange(l)
    angl = 2.0 * np.pi * ((n[:, None] * n[None, :]) % l) / l
    return tuple(jnp.asarray(a, F32).astype(BF16) for a in (cs, np.cos(angl), np.sin(angl)))


def _fnet_kernel(scale, u_ref, cs_ref, cl_ref, sl_ref, wf_ref, o_ref):
    t = _dot(u_ref[...], cs_ref[...])
    uc = t[:, :WIDTH_C].astype(BF16)
    us = t[:, WIDTH_C:].astype(BF16)
    y = (_dot(cl_ref[...], uc) - _dot(sl_ref[...], us)) * scale
    o_ref[...] = _dot(y.astype(BF16), wf_ref[...]).astype(BF16)


def _fnet(uf, dft, wf_bf, li):
    b, l, _ = uf.shape
    cs, cl, sl = dft
    scale = 1.0 / math.sqrt(l * FN_GROUP_DIM)
    return pl.pallas_call(
        functools.partial(_fnet_kernel, scale),
        grid=(b,),
        in_specs=[
            pl.BlockSpec((None, l, WIDTH_C), lambda bi: (bi, 0, 0)),
            _const_spec(cs.shape, lambda bi: (0, 0)),
            _const_spec(cl.shape, lambda bi: (0, 0)),
            _const_spec(sl.shape, lambda bi: (0, 0)),
            _const_spec((None, WIDTH_C, WIDTH_C), lambda bi: (li, 0, 0)),
        ],
        out_specs=pl.BlockSpec((None, l, WIDTH_C), lambda bi: (bi, 0, 0)),
        out_shape=jax.ShapeDtypeStruct((b, l, WIDTH_C), BF16),
        compiler_params=_params(("parallel",)),
        name="fnet_l%d" % l,
    )(uf, cs, cl, sl, wf_bf)


def _mix_ffn_kernel(tm, tpb, final, att_ref, attp_ref, attn_ref, rec_ref, recp_ref, recn_ref, four_ref,
                    fourp_ref, fourn_ref, x_ref, xp_ref, xn_ref, mod_ref, nw_ref, wo_ref, wup_ref, cw_ref,
                    cb_ref, wdn_ref, fw_ref, o_ref, a_ref):
    t = pl.program_id(0) % tpb
    ext = tm + 2 * HALO
    m = mod_ref[...]

    def with_halo(prev_ref, main_ref, next_ref):
        return jnp.concatenate([prev_ref[...], main_ref[...], next_ref[...]], axis=0)

    mix = jnp.concatenate([with_halo(attp_ref, att_ref, attn_ref), with_halo(recp_ref, rec_ref, recn_ref),
                           with_halo(fourp_ref, four_ref, fourn_ref)], axis=1)
    x1e = with_halo(xp_ref, x_ref, xn_ref) + m[2:3] * _dot(mix, wo_ref[...])
    h = (x1e * _rms_scale(x1e) * nw_ref[...]) * (1.0 + m[4:5]) + m[3:4]
    row = lax.broadcasted_iota(jnp.int32, (ext, 1), 0)
    outside = ((row < HALO) & (t == 0)) | ((row >= HALO + tm) & (t == tpb - 1))
    hext = jnp.where(outside, 0.0, h).astype(BF16)

    def conv(u, c0):
        w = cw_ref[:, c0:c0 + FFN_COLS]
        up = pltpu.roll(u, 1, 0)
        dn = pltpu.roll(u, ext - 1, 0)
        r = up * w[0:1] + u * w[1:2] + dn * w[2:3] + cb_ref[:, c0:c0 + FFN_COLS]
        return r[HALO:HALO + tm]

    for j in range(D_FF // FFN_COLS):
        c0 = j * FFN_COLS
        gate = conv(_dot(hext, wup_ref[:, c0:c0 + FFN_COLS]), c0)
        val = conv(_dot(hext, wup_ref[:, D_FF + c0:D_FF + c0 + FFN_COLS]), D_FF + c0)
        a_ref[:, c0:c0 + FFN_COLS] = (gate * jax.nn.sigmoid(gate) * val).astype(BF16)
    x2 = x1e[HALO:HALO + tm] + m[5:6] * _dot(a_ref[...], wdn_ref[...])
    if final:
        x2 = x2 * _rms_scale(x2) * fw_ref[...]
    o_ref[...] = x2


def _mix_ffn(att, rec, four, x2d, mod4, nw2, wo_bf, wup_bf, conv_w, conv_b, wdn_bf, final_w, li, seg_len,
             mod_row, tm, final):
    rows = x2d.shape[0]
    tpb = seg_len // tm
    hb = tm // HALO
    nhb = rows // HALO
    mod_map = (lambda i: (li, i // tpb, 0, 0)) if mod_row is None else (lambda i: (li, mod_row, 0, 0))

    def row_specs(w):
        return [pl.BlockSpec((tm, w), lambda i: (i, 0)),
                pl.BlockSpec((HALO, w), lambda i: (jnp.maximum(i * hb - 1, 0), 0)),
                pl.BlockSpec((HALO, w), lambda i: (jnp.minimum((i + 1) * hb, nhb - 1), 0))]

    return pl.pallas_call(
        functools.partial(_mix_ffn_kernel, tm, tpb, final),
        grid=(rows // tm,),
        in_specs=row_specs(WIDTH_A) + row_specs(WIDTH_B) + row_specs(WIDTH_C) + row_specs(D_MODEL) + [
            pl.BlockSpec((None, None, 6, D_MODEL), mod_map),
            pl.BlockSpec((None, 1, D_MODEL), lambda i: (li, 0, 0)),
            _const_spec((None, D_MODEL, D_MODEL), lambda i: (li, 0, 0)),
            _const_spec((None, D_MODEL, 2 * D_FF), lambda i: (li, 0, 0)),
            pl.BlockSpec((None, 3, 2 * D_FF), lambda i: (li, 0, 0)),
            pl.BlockSpec((None, 1, 2 * D_FF), lambda i: (li, 0, 0)),
            _const_spec((None, D_FF, D_MODEL), lambda i: (li, 0, 0)),
            pl.BlockSpec((1, D_MODEL), lambda i: (0, 0)),
        ],
        out_specs=pl.BlockSpec((tm, D_MODEL), lambda i: (i, 0)),
        out_shape=jax.ShapeDtypeStruct((rows, D_MODEL), F32),
        scratch_shapes=[pltpu.VMEM((tm, D_FF), BF16)],
        compiler_params=_params(("parallel",)),
        name="mix_convffn_final" if final else "mix_convffn",
    )(att, att, att, rec, rec, rec, four, four, four, x2d, x2d, x2d, mod4, nw2, wo_bf, wup_bf, conv_w, conv_b,
      wdn_bf, final_w)


def _rope_tables(l):
    rows = l // GRID_W
    pos_r = np.repeat(np.arange(rows), GRID_W).astype(np.float32)
    pos_c = np.tile(np.arange(GRID_W), rows).astype(np.float32)
    half = DH_A // 2
    inv_freq = (ROPE_BASE ** (-np.arange(0, half, 2, dtype=np.float32) / half)).astype(np.float32)
    ang = np.concatenate([pos_r[:, None] * inv_freq] * 2 + [pos_c[:, None] * inv_freq] * 2, axis=-1)
    cos = np.cos(ang.astype(np.float32)).astype(np.float32)
    sin = np.sin(ang.astype(np.float32)).astype(np.float32)
    cos = np.tile(cos, (1, 2))
    sin = np.tile(sin, (1, 2))
    first = (np.arange(DV_A) % 32) < 16
    sin_a = np.where(first, -sin, 0.0).astype(np.float32)
    sin_b = np.where(first, 0.0, sin).astype(np.float32)
    return jnp.asarray(cos), jnp.asarray(sin_a), jnp.asarray(sin_b)


def kernel(x, c, ctx, c_ctx, w_ada, b_ada, norm1_w, norm2_w, w_in, lam_qk, subln_w, lb_param, hgrn_norm_w,
           w_fnet, w_out, w_up, conv_w, conv_b, w_down, final_norm_w):
    b, l, d = x.shape
    lc = ctx.shape[1]
    depth = w_ada.shape[0]
    mod_rows = 16
    assert d == D_MODEL and b < mod_rows
    assert l % INPROJ_ROWS == 0 and l % FFN_ROWS == 0 and l % ATTN_SUB == 0
    for seq in (l, lc):
        nchunks = seq // HG_CHUNK
        assert seq % (2 * HG_CHUNK) == 0 and nchunks % min(HG_STEPS_PER_TRIP, nchunks) == 0
    tm_l = INPROJ_ROWS
    tm_f = FFN_ROWS
    tm_c = lc

    lb_all = jnp.cumsum(jax.nn.softmax(lb_param.astype(F32), axis=0), axis=0)
    lb_all = lb_all - lb_all[0:1]
    lbc = jnp.stack([jnp.log(lb_all[:, 0]), jnp.log1p(-lb_all[:, 0]),
                     jnp.log(lb_all[:, 1]), jnp.log1p(-lb_all[:, 1])], axis=1)
    lq = lam_qk.astype(F32)
    lam_dyn = jnp.exp(jnp.sum(lq[:, 0] * lq[:, 1], axis=-1)) - jnp.exp(jnp.sum(lq[:, 2] * lq[:, 3], axis=-1))
    rope_tabs = _rope_tables(l)
    hconst = _hgrn_constants()
    dft_l = _dft_constants(l)
    dft_c = _dft_constants(lc)
    w_in_bf = w_in.astype(BF16)
    w_out_bf = w_out.astype(BF16)
    w_up_bf = w_up.astype(BF16)
    w_dn_bf = w_down.astype(BF16)
    w_fn_bf = w_fnet.astype(BF16)
    nw1 = norm1_w.reshape(depth, 1, d)
    nw2 = norm2_w.reshape(depth, 1, d)
    sub_w = subln_w.reshape(depth, 1, DV_A)
    hw = jnp.tile(hgrn_norm_w, (1, N_HEADS_B)).reshape(depth, 1, WIDTH_B)
    cb = conv_b.reshape(depth, 1, 2 * D_FF)
    fw = final_norm_w.reshape(1, d)

    cstack = jnp.concatenate([c, c_ctx[None, :], jnp.zeros((mod_rows - b - 1, d), F32)], axis=0)
    mod4 = _modulation(cstack, w_ada, b_ada).reshape(depth, mod_rows, 6, d)
    ctx_row = b

    zeros_state = jnp.zeros((b, WIDTH_B, WIDTH_B), F32)
    xl = x.reshape(b * l, d)
    xc = ctx.reshape(b * lc, d)
    for li in range(depth):
        last = li == depth - 1
        lam_init = 0.8 - 0.6 * math.exp(-0.3 * li)
        lam = (lam_dyn[li] + lam_init).reshape(1, 1)

        qa, ka, va, hg, uf = _inproj(xl, mod4, nw1, w_in_bf, lbc, rope_tabs, li, l, None, tm_l)
        qc, kc, vc, hgc, ufc = _inproj(xc, mod4, nw1, w_in_bf, lbc, None, li, lc, ctx_row,
                                       math.gcd(b * lc, tm_l))
        r3 = lambda a, n: a.reshape(b, n, a.shape[-1])

        att = _attention(lam, r3(qa, l), [r3(kc, lc), r3(ka, l)], [r3(vc, lc), r3(va, l)], sub_w, li,
                         1.0 - lam_init, 1, l, ATTN_SUB)
        rec_c, sf, sb = _hgrn(r3(hgc, lc), hconst, hw, zeros_state, zeros_state, li, states_only=last)
        rec, _, _ = _hgrn(r3(hg, l), hconst, hw, sf, sb, li)
        four = _fnet(r3(uf, l), dft_l, w_fn_bf, li)

        xl = _mix_ffn(att.reshape(b * l, -1), rec.reshape(b * l, -1), four.reshape(b * l, -1), xl, mod4, nw2,
                      w_out_bf, w_up_bf, conv_w, cb, w_dn_bf, fw, li, l, None, tm_f, last)

        if not last:
            att_c = _attention(lam, r3(qc, lc), [r3(kc, lc)], [r3(vc, lc)], sub_w, li, 1.0 - lam_init,
                               N_HEADS_A, lc, min(ATTN_SUB, lc) // 2)
            four_c = _fnet(r3(ufc, lc), dft_c, w_fn_bf, li)
            xc = _mix_ffn(att_c.reshape(b * lc, -1), rec_c.reshape(b * lc, -1), four_c.reshape(b * lc, -1), xc,
                          mod4, nw2, w_out_bf, w_up_bf, conv_w, cb, w_dn_bf, fw, li, lc, ctx_row, tm_c, False)
    return xl.reshape(b, l, d)
```

```python
import functools
import math

import numpy as np
import jax
import jax.numpy as jnp
from jax import lax
from jax.experimental import pallas as pl
from jax.experimental.pallas import tpu as pltpu

F32 = jnp.float32
BF16 = jnp.bfloat16

D_MODEL = 1024
GRID_W = 64
N_HEADS_A = 4
DH_A = 64
DV_A = 2 * DH_A
WIDTH_A = N_HEADS_A * DV_A
N_HEADS_B = 4
DK_B = 64
DV_B = 64
WIDTH_B = N_HEADS_B * DV_B
FN_GROUPS = 4
FN_GROUP_DIM = 64
WIDTH_C = FN_GROUPS * FN_GROUP_DIM
PROJ_WIDTH = 3 * WIDTH_A + 5 * WIDTH_B + WIDTH_C
D_FF = 2816
ROPE_BASE = 10000.0
EPS = 1e-6
LOG2E = math.log2(math.e)
exp2 = jnp.exp2
HG_UNIT = LOG2E

V7X_LANES = 128
V7X_BF16_SUBLANES = 16
V7X_VMEM_BYTES = 64 * 1024 * 1024
VMEM_LIMIT = V7X_VMEM_BYTES - 8 * 1024 * 1024

MOD_COLS = 1536
INPROJ_ROWS = 1024
INPROJ_SUB = 256
ATTN_SUB = 256
ATTN_AHEAD = 2
FFN_ROWS = 1024
FFN_COLS = 256
HALO = V7X_BF16_SUBLANES
HG_CHUNK = 64
HG_LEVELS = 6
HG_WIDTHS = tuple(HG_CHUNK >> (l + 1) for l in range(HG_LEVELS))
HG_STEPS_PER_TRIP = 8


def _params(sem):
    return pltpu.CompilerParams(dimension_semantics=sem, vmem_limit_bytes=VMEM_LIMIT)


def _const_spec(shape, index_map):
    return pl.BlockSpec(shape, index_map, pipeline_mode=pl.Buffered(1))


def _dot(a, b):
    return jnp.dot(a, b, preferred_element_type=F32)


def _dot_nt(a, b):
    return lax.dot_general(a, b, (((1,), (1,)), ((), ())), preferred_element_type=F32)


def _dot_tn(a, b):
    return lax.dot_general(a, b, (((0,), (0,)), ((), ())), preferred_element_type=F32)


def _split3(x):
    h = x.astype(BF16)
    r = x - h.astype(F32)
    m = r.astype(BF16)
    return h, m, (r - m.astype(F32)).astype(BF16)


def _split2(x):
    h = x.astype(BF16)
    return h, (x - h.astype(F32)).astype(BF16)


def _rms_scale(x):
    return lax.rsqrt(jnp.mean(x * x, axis=-1, keepdims=True) + EPS)


def _mod_kernel(c_ref, w_ref, b_ref, o_ref):
    c = c_ref[...]
    a = (c * jax.nn.sigmoid(c)).astype(BF16)
    o_ref[0] = _dot(a, w_ref[0].astype(BF16)) + b_ref[0]


def _modulation(cstack, w_ada, b_ada):
    depth, _, n = w_ada.shape
    tn = MOD_COLS
    rows = cstack.shape[0]
    return pl.pallas_call(
        _mod_kernel,
        grid=(depth, n // tn),
        in_specs=[
            pl.BlockSpec((rows, D_MODEL), lambda l, j: (0, 0)),
            pl.BlockSpec((1, D_MODEL, tn), lambda l, j: (l, 0, j)),
            pl.BlockSpec((1, 1, tn), lambda l, j: (l, 0, j)),
        ],
        out_specs=pl.BlockSpec((1, rows, tn), lambda l, j: (l, 0, j)),
        out_shape=jax.ShapeDtypeStruct((depth, rows, n), F32),
        compiler_params=_params(("arbitrary", "arbitrary")),
        name="modulation",
    )(cstack, w_ada, b_ada.reshape(depth, 1, n))


def _log_forget(z, log_lb, log1m_lb):
    ls = jnp.minimum(z, 0.0) - jnp.log(1.0 + jnp.exp(-jnp.abs(z)))
    b = log1m_lb + ls
    return jnp.maximum(log_lb, b) + jnp.log(1.0 + jnp.exp(-jnp.abs(log_lb - b)))


def _inproj_kernel(rope, x_ref, mod_ref, nw_ref, w_ref, lbc_ref, *rest):
    if rope:
        cos_ref, sa_ref, sb_ref, qa_ref, ka_ref, va_ref, hg_ref, uf_ref = rest
    else:
        qa_ref, ka_ref, va_ref, hg_ref, uf_ref = rest
    m = mod_ref[...]
    lbc = lbc_ref[...]
    c0 = 3 * WIDTH_A
    for r0 in range(0, x_ref.shape[0], INPROJ_SUB):
        rs = slice(r0, r0 + INPROJ_SUB)
        x = x_ref[rs, :]
        y = x * _rms_scale(x) * nw_ref[...]
        h = (y * (1.0 + m[1:2]) + m[0:1]).astype(BF16)

        def proj(col, n):
            return _dot(h, w_ref[:, col:col + n])

        for base, ref, scale in ((0, qa_ref, DH_A ** -0.5 * LOG2E), (WIDTH_A, ka_ref, None)):
            t = proj(base, WIDTH_A)
            for hd in range(N_HEADS_A):
                th = t[:, hd * DV_A:(hd + 1) * DV_A]
                if rope:
                    th = (th * cos_ref[rs, :]
                          + pltpu.roll(th, V7X_LANES - 16, 1) * sa_ref[rs, :]
                          + pltpu.roll(th, 16, 1) * sb_ref[rs, :])
                if scale is not None:
                    th = th * scale
                ref[rs, hd * DV_A:(hd + 1) * DV_A] = th.astype(BF16)
        va_ref[rs, :] = proj(2 * WIDTH_A, WIDTH_A).astype(BF16)
        hg_ref[rs, 0:WIDTH_B] = proj(c0, WIDTH_B)
        for dirn in range(2):
            z = proj(c0 + WIDTH_B * (1 + dirn), WIDTH_B)
            hg_ref[rs, WIDTH_B * (1 + dirn):WIDTH_B * (2 + dirn)] = _log_forget(
                z, lbc[2 * dirn:2 * dirn + 1], lbc[2 * dirn + 1:2 * dirn + 2])
        hg_ref[rs, 3 * WIDTH_B:5 * WIDTH_B] = proj(c0 + 3 * WIDTH_B, 2 * WIDTH_B)
        uf_ref[rs, :] = proj(c0 + 5 * WIDTH_B, WIDTH_C).astype(BF16)


def _inproj(x2d, mod4, nw, w_bf, lbc, rope_tabs, li, seg_len, mod_row, tm):
    rows = x2d.shape[0]
    tpb = seg_len // tm
    rope = rope_tabs is not None
    mod_map = (lambda i: (li, i // tpb, 0, 0)) if mod_row is None else (lambda i: (li, mod_row, 0, 0))
    in_specs = [
        pl.BlockSpec((tm, D_MODEL), lambda i: (i, 0)),
        pl.BlockSpec((None, None, 6, D_MODEL), mod_map),
        pl.BlockSpec((None, 1, D_MODEL), lambda i: (li, 0, 0)),
        _const_spec((None, D_MODEL, PROJ_WIDTH), lambda i: (li, 0, 0)),
        pl.BlockSpec((None, 4, WIDTH_B), lambda i: (li, 0, 0)),
    ]
    args = [x2d, mod4, nw, w_bf, lbc]
    if rope:
        in_specs += [pl.BlockSpec((tm, DV_A), lambda i: (i % tpb, 0))] * 3
        args += list(rope_tabs)
    out_shape = (
        jax.ShapeDtypeStruct((rows, WIDTH_A), BF16),
        jax.ShapeDtypeStruct((rows, WIDTH_A), BF16),
        jax.ShapeDtypeStruct((rows, WIDTH_A), BF16),
        jax.ShapeDtypeStruct((rows, 5 * WIDTH_B), F32),
        jax.ShapeDtypeStruct((rows, WIDTH_C), BF16),
    )
    out_specs = (
        pl.BlockSpec((tm, WIDTH_A), lambda i: (i, 0)),
        pl.BlockSpec((tm, WIDTH_A), lambda i: (i, 0)),
        pl.BlockSpec((tm, WIDTH_A), lambda i: (i, 0)),
        pl.BlockSpec((tm, 5 * WIDTH_B), lambda i: (i, 0)),
        pl.BlockSpec((tm, WIDTH_C), lambda i: (i, 0)),
    )
    return pl.pallas_call(
        functools.partial(_inproj_kernel, rope),
        grid=(rows // tm,),
        in_specs=in_specs,
        out_specs=out_specs,
        out_shape=out_shape,
        compiler_params=_params(("parallel",)),
        name="inproj_rope" if rope else "inproj",
    )(*args)


def _attn_kernel(nseg, hps, tq, sub, out_scale, lam_ref, q_ref, *rest):
    k_refs = rest[:nseg]
    v_refs = rest[nseg:2 * nseg]
    w_ref, o_ref = rest[2 * nseg:]
    lam = lam_ref[0, 0]
    wn = w_ref[...] * out_scale
    lane = lax.broadcasted_iota(jnp.int32, (sub, DV_A), 1)
    for hh in range(hps):
        hl = slice(hh * DV_A, (hh + 1) * DV_A)
        vexts = []
        for v_ref in v_refs:
            v = v_ref[:, hl]
            vl = lax.broadcasted_iota(jnp.int32, v.shape, 1)
            vexts.append(jnp.concatenate([v, jnp.where(vl == 0, 1.0, 0.0).astype(BF16)], axis=1))
        def block_scores(r0):
            q = q_ref[r0:r0 + sub, hl].astype(F32)
            qs = jnp.concatenate([jnp.where(lane < DH_A, q, 0.0), jnp.where(lane >= DH_A, q, 0.0)],
                                 axis=0).astype(BF16)
            return [_dot_nt(qs, k_ref[:, hl]) for k_ref in k_refs]

        starts = list(range(0, tq, sub))
        ahead = [block_scores(r) for r in starts[:ATTN_AHEAD]]
        for bi, r0 in enumerate(starts):
            scores = ahead.pop(0)
            if bi + ATTN_AHEAD < len(starts):
                ahead.append(block_scores(starts[bi + ATTN_AHEAD]))
            mx = jnp.max(scores[0], axis=-1, keepdims=True)
            for s in scores[1:]:
                mx = jnp.maximum(mx, jnp.max(s, axis=-1, keepdims=True))
            acc = sum(_dot(exp2(s - mx).astype(BF16), vext) for s, vext in zip(scores, vexts))
            r = acc[:, :DV_A] / acc[:, DV_A:DV_A + 1]
            o = r[:sub] - lam * r[sub:]
            o_ref[r0:r0 + sub, hl] = (o * _rms_scale(o) * wn).astype(BF16)


def _attention(lam, q, ks, vs, subln_w, li, out_scale, hps, tq, sub):
    b, lq, _ = q.shape
    nseg = len(ks)
    w = hps * DV_A
    in_specs = [
        pl.BlockSpec(memory_space=pltpu.SMEM),
        pl.BlockSpec((None, tq, w), lambda bi, h, i: (bi, i, h)),
    ]
    for arr in list(ks) + list(vs):
        in_specs.append(pl.BlockSpec((None, arr.shape[1], w), lambda bi, h, i: (bi, 0, h)))
    in_specs.append(pl.BlockSpec((None, 1, DV_A), lambda bi, h, i: (li, 0, 0)))
    return pl.pallas_call(
        functools.partial(_attn_kernel, nseg, hps, tq, sub, out_scale),
        grid=(b, N_HEADS_A // hps, lq // tq),
        in_specs=in_specs,
        out_specs=pl.BlockSpec((None, tq, w), lambda bi, h, i: (bi, i, h)),
        out_shape=jax.ShapeDtypeStruct((b, lq, WIDTH_A), BF16),
        compiler_params=_params(("parallel", "parallel", "arbitrary")),
        name="diff_attention_%dseg" % nseg,
    )(lam, q, *ks, *vs, subln_w)


def _hgrn_constants():
    c = HG_CHUNK
    t = np.arange(c)[:, None]
    r = np.arange(c)[None, :]
    amask = []
    lomask = []
    for w in HG_WIDTHS:
        hi = (t % (2 * w)) >= w
        same_pair = (t // (2 * w)) == (r // (2 * w))
        amask.append(same_pair & (hi != ((r % (2 * w)) >= w)))
        lomask.append(np.broadcast_to(~hi, (c, c)))
    psum = np.zeros((2 * c, 2 * c), np.float32)
    psum[:c, :c] = r <= t
    psum[c:, c:] = r >= t
    tile = lambda a: np.tile(np.asarray(a, np.float32), (1, N_HEADS_B))
    amask = np.stack([tile(a) for a in amask])
    lomask = np.stack([tile(a) for a in lomask])
    hd = np.arange(WIDTH_B) // DK_B
    bd = (hd[:, None] == hd[None, :]).astype(np.float32)
    return (jnp.asarray(psum, BF16), jnp.asarray(amask, F32), jnp.asarray(lomask, F32), jnp.asarray(bd, F32))


def _hgrn_kernel(nc, states_only, q_ref, gf_ref, gb_ref, v_ref, og_ref, psum_ref, amask_ref, lomask_ref,
                 bd_ref, hw_ref, sf0_ref, sb0_ref, *outs_and_scratch):
    if states_only:
        rec_ref = None
        sf_ref, sb_ref = outs_and_scratch[:2]
        scratch = outs_and_scratch[2:]
    else:
        rec_ref, sf_ref, sb_ref = outs_and_scratch[:3]
        scratch = outs_and_scratch[3:]
    (sfp_ref, sbn_ref, cf_ref, cbs_ref, qa_ref, ka_ref, qb_ref, kb_ref, aa_ref, pa_ref, ab_ref, pb_ref, oa_ref,
     ob_ref) = scratch
    c = HG_CHUNK
    bd = bd_ref[...]
    bd_bf = bd.astype(BF16)
    lane_head = lax.broadcasted_iota(jnp.int32, (c, WIDTH_B), 1) // DK_B

    def stack_heads(a):
        return jnp.concatenate([jnp.where(lane_head == h, a, 0.0).astype(BF16) for h in range(N_HEADS_B)],
                               axis=0)

    def rows(ref, n):
        if isinstance(n, int):
            return ref[n * c:(n + 1) * c, :]
        return ref[pl.ds(pl.multiple_of(n * c, c), c), :]

    sf_ref[...] = sf0_ref[...]
    sb_ref[...] = sb0_ref[...]

    def decayed_keys(i, kt_ref, dec_ref):
        nf = i
        nb = nc - 1 - i
        gf = rows(gf_ref, nf)
        gb = rows(gb_ref, nb)
        g = jnp.concatenate([gf, gb], axis=0)
        sums = sum(_dot(psum_ref[...], part) for part in _split3(g))
        cf = sums[:c]
        cbs = sums[c:]
        cf_ref[nf] = cf * HG_UNIT
        cbs_ref[nb] = cbs * HG_UNIT
        tot_f = cf[c - 1:c]
        tot_b = cbs[0:1]
        kt_ref[0] = (1.0 - jnp.exp(gf)) * jnp.exp(tot_f - cf)
        kt_ref[1] = (1.0 - jnp.exp(gb)) * jnp.exp(tot_b - cbs)
        dec_ref[0:1, :] = jnp.exp(tot_f)
        dec_ref[8:9, :] = jnp.exp(tot_b)

    def apply_step(i, kt_ref, dec_ref):
        nf = i
        nb = nc - 1 - i
        upd_f = _dot_tn(rows(v_ref, nf).astype(BF16), kt_ref[0].astype(BF16)) * bd
        upd_b = _dot_tn(rows(v_ref, nb).astype(BF16), kt_ref[1].astype(BF16)) * bd
        stf = sf_ref[...]
        sfp_ref[nf] = stf.astype(BF16)
        sf_ref[...] = stf * dec_ref[0:1, :] + upd_f
        stb = sb_ref[...]
        sbn_ref[nb] = stb.astype(BF16)
        sb_ref[...] = stb * dec_ref[8:9, :] + upd_b

    pre = ((qa_ref, oa_ref), (qb_ref, ob_ref))
    decayed_keys(0, *pre[0])

    per_trip = min(HG_STEPS_PER_TRIP, nc)

    def pre_body(j, carry):
        for u in range(per_trip):
            i = per_trip * j + u
            apply_step(i, *pre[u % 2])
            decayed_keys(jnp.minimum(i + 1, nc - 1), *pre[1 - u % 2])
        return carry

    lax.fori_loop(0, nc // per_trip, pre_body, 0)
    if states_only:
        return

    def prepare(n, qop_ref, kop_ref):
        q = rows(q_ref, n)
        gf = rows(gf_ref, n) * HG_UNIT
        gb = rows(gb_ref, n) * HG_UNIT
        cf = cf_ref[n]
        cbs = cbs_ref[n]
        kf = 1.0 - exp2(gf)
        kb = 1.0 - exp2(gb)
        for lvl, w in enumerate(HG_WIDTHS):
            lo = lomask_ref[lvl] > 0.5
            if 2 * w >= 8:
                anchor = lambda a, r: jnp.broadcast_to(
                    a.reshape(c // (2 * w), 2 * w, WIDTH_B)[:, r:r + 1, :],
                    (c // (2 * w), 2 * w, WIDTH_B)).reshape(c, WIDTH_B)
                d_f = cf - anchor(cf, w - 1)
                d_b = cbs - anchor(cbs, w)
                eq = exp2(jnp.where(lo, d_b, d_f))
                ek = exp2(-jnp.where(lo, d_f, d_b))
            elif w == 2:
                r4 = lax.broadcasted_iota(jnp.int32, (c, WIDTH_B), 0) & 3
                gf_up, gf_dn = pltpu.roll(gf, c - 1, 0), pltpu.roll(gf, 1, 0)
                gb_up, gb_dn = pltpu.roll(gb, c - 1, 0), pltpu.roll(gb, 1, 0)
                eq = exp2(jnp.where(r4 == 0, gb + gb_up, jnp.where(r4 == 1, gb,
                              jnp.where(r4 == 2, gf, gf + gf_dn))))
                ek = exp2(jnp.where(r4 == 0, gf_up, jnp.where(r4 == 3, gb_dn, 0.0)))
            else:
                eq = exp2(jnp.where(lo, gb, gf))
                ek = None
            klev = jnp.where(lo, kf, kb)
            qop_ref[lvl] = q * eq
            kop_ref[lvl] = klev if ek is None else klev * ek
        qop_ref[HG_LEVELS] = q * exp2(cf)
        qop_ref[HG_LEVELS + 1] = q * exp2(cbs)
        qop_ref[HG_LEVELS + 2] = q * (kf + kb)

    def scores(n, qop_ref, kop_ref, a_ref, part_ref):
        a_all = jnp.zeros((c, N_HEADS_B * c), F32)
        for lvl in range(HG_LEVELS):
            a_all = a_all + _dot_nt(qop_ref[lvl].astype(BF16), stack_heads(kop_ref[lvl])) * amask_ref[lvl]
        a_ref[...] = a_all
        part = _dot(qop_ref[HG_LEVELS + 2].astype(BF16), bd_bf) * rows(v_ref, n)
        part = part + _dot_nt(qop_ref[HG_LEVELS].astype(BF16), sfp_ref[n])
        part_ref[...] = part + _dot_nt(qop_ref[HG_LEVELS + 1].astype(BF16), sbn_ref[n])

    def values(n, a_ref, part_ref, o_ref):
        o_ref[...] = (_dot(a_ref[...].astype(BF16), stack_heads(rows(v_ref, n)))
                      + part_ref[...])

    def finish(n, o_ref):
        o = o_ref[...]
        sq_h, sq_l = _split2(o * o)
        ms = _dot(jnp.concatenate([sq_h, sq_l], axis=0), bd_bf)
        ms = (ms[:c] + ms[c:]) * (1.0 / DV_B)
        og = rows(og_ref, n)
        rec = o * lax.rsqrt(ms + EPS) * hw_ref[...] * (og * jax.nn.sigmoid(og))
        rec_ref[pl.ds(pl.multiple_of(n * c, c), c), :] = rec.astype(BF16)

    ops = ((qa_ref, ka_ref), (qb_ref, kb_ref))
    sc = ((aa_ref, pa_ref), (ab_ref, pb_ref))
    ob = (oa_ref, ob_ref)
    last = nc - 1

    def step(m, par, clamp):
        idx = (lambda i: jnp.minimum(i, last)) if clamp else (lambda i: i)
        if clamp or m + 2 >= 0:
            scores(idx(m + 2), *ops[par], *sc[par])
        if clamp or m + 1 >= 0:
            values(idx(m + 1), *sc[1 - par], ob[1 - par])
        if clamp or m >= 0:
            finish(m, ob[par])
        prepare(idx(m + 3), *ops[1 - par])

    for m in (-3, -2, -1):
        step(m, m % 2, False)

    per_trip = min(HG_STEPS_PER_TRIP, nc)

    def body(j, carry):
        for u in range(per_trip):
            step(per_trip * j + u, u % 2, True)
        return carry

    lax.fori_loop(0, nc // per_trip, body, 0)


def _hgrn(hg, hconst, hw, sf0, sb0, li, states_only=False):
    b, l, _ = hg.shape
    nc = l // HG_CHUNK
    psum, amask, lomask, bd = hconst
    col = lambda j: pl.BlockSpec((None, l, WIDTH_B), lambda bi: (bi, 0, j))
    full = lambda a: _const_spec(a.shape, lambda bi: (0,) * a.ndim)
    st_spec = pl.BlockSpec((None, WIDTH_B, WIDTH_B), lambda bi: (bi, 0, 0))
    state_scratch = pltpu.VMEM((nc, WIDTH_B, WIDTH_B), BF16)
    sums_scratch = pltpu.VMEM((nc, HG_CHUNK, WIDTH_B), F32)
    q_operands = pltpu.VMEM((HG_LEVELS + 3, HG_CHUNK, WIDTH_B), F32)
    k_operands = pltpu.VMEM((HG_LEVELS, HG_CHUNK, WIDTH_B), F32)
    chunk_f32 = pltpu.VMEM((HG_CHUNK, WIDTH_B), F32)
    out_specs = [pl.BlockSpec((None, l, WIDTH_B), lambda bi: (bi, 0, 0)), st_spec, st_spec]
    out_shape = [jax.ShapeDtypeStruct((b, l, WIDTH_B), BF16),
                 jax.ShapeDtypeStruct((b, WIDTH_B, WIDTH_B), F32),
                 jax.ShapeDtypeStruct((b, WIDTH_B, WIDTH_B), F32)]
    if states_only:
        out_specs, out_shape = out_specs[1:], out_shape[1:]
    outs = pl.pallas_call(
        functools.partial(_hgrn_kernel, nc, states_only),
        grid=(b,),
        in_specs=[col(0), col(1), col(2), col(3), col(4), full(psum), full(amask),
                  full(lomask), full(bd), pl.BlockSpec((None, 1, WIDTH_B), lambda bi: (li, 0, 0)),
                  st_spec, st_spec],
        out_specs=tuple(out_specs),
        out_shape=tuple(out_shape),
        scratch_shapes=[state_scratch, state_scratch, sums_scratch, sums_scratch,
                        q_operands, k_operands, q_operands, k_operands,
                        chunk_f32, chunk_f32, chunk_f32, chunk_f32, chunk_f32, chunk_f32],
        compiler_params=_params(("parallel",)),
        name="hgrn_states_l%d" % l if states_only else "hgrn_l%d" % l,
    )(hg, hg, hg, hg, hg, psum, amask, lomask, bd, hw, sf0, sb0)
    return (None,) + tuple(outs) if states_only else tuple(outs)


def _dft_constants(l):
    k = np.arange(FN_GROUP_DIM)
    ang = 2.0 * np.pi * ((k[:, None] * k[None, :]) % FN_GROUP_DIM) / FN_GROUP_DIM
    eye = np.eye(FN_GROUPS)
    cs = np.concatenate([np.kron(eye, np.cos(ang)), np.kron(eye, np.sin(ang))], axis=1)
    n = np.arange(l)
    angl = 2.0 * np.pi * ((n[:, None] * n[None, :]) % l) / l
    return tuple(jnp.asarray(a, F32).astype(BF16) for a in (cs, np.cos(angl), np.sin(angl)))


def _fnet_kernel(scale, u_ref, cs_ref, cl_ref, sl_ref, wf_ref, o_ref):
    t = _dot(u_ref[...], cs_ref[...])
    uc = t[:, :WIDTH_C].astype(BF16)
    us = t[:, WIDTH_C:].astype(BF16)
    y = (_dot(cl_ref[...], uc) - _dot(sl_ref[...], us)) * scale
    o_ref[...] = _dot(y.astype(BF16), wf_ref[...]).astype(BF16)


def _fnet(uf, dft, wf_bf, li):
    b, l, _ = uf.shape
    cs, cl, sl = dft
    scale = 1.0 / math.sqrt(l * FN_GROUP_DIM)
    return pl.pallas_call(
        functools.partial(_fnet_kernel, scale),
        grid=(b,),
        in_specs=[
            pl.BlockSpec((None, l, WIDTH_C), lambda bi: (bi, 0, 0)),
            _const_spec(cs.shape, lambda bi: (0, 0)),
            _const_spec(cl.shape, lambda bi: (0, 0)),
            _const_spec(sl.shape, lambda bi: (0, 0)),
            _const_spec((None, WIDTH_C, WIDTH_C), lambda bi: (li, 0, 0)),
        ],
        out_specs=pl.BlockSpec((None, l, WIDTH_C), lambda bi: (bi, 0, 0)),
        out_shape=jax.ShapeDtypeStruct((b, l, WIDTH_C), BF16),
        compiler_params=_params(("parallel",)),
        name="fnet_l%d" % l,
    )(uf, cs, cl, sl, wf_bf)


def _mix_ffn_kernel(tm, tpb, final, att_ref, attp_ref, attn_ref, rec_ref, recp_ref, recn_ref, four_ref,
                    fourp_ref, fourn_ref, x_ref, xp_ref, xn_ref, mod_ref, nw_ref, wo_ref, wup_ref, cw_ref,
                    cb_ref, wdn_ref, fw_ref, o_ref, a_ref):
    t = pl.program_id(0) % tpb
    ext = tm + 2 * HALO
    m = mod_ref[...]

    def with_halo(prev_ref, main_ref, next_ref):
        return jnp.concatenate([prev_ref[...], main_ref[...], next_ref[...]], axis=0)

    mix = jnp.concatenate([with_halo(attp_ref, att_ref, attn_ref), with_halo(recp_ref, rec_ref, recn_ref),
                           with_halo(fourp_ref, four_ref, fourn_ref)], axis=1)
    x1e = with_halo(xp_ref, x_ref, xn_ref) + m[2:3] * _dot(mix, wo_ref[...])
    h = (x1e * _rms_scale(x1e) * nw_ref[...]) * (1.0 + m[4:5]) + m[3:4]
    row = lax.broadcasted_iota(jnp.int32, (ext, 1), 0)
    outside = ((row < HALO) & (t == 0)) | ((row >= HALO + tm) & (t == tpb - 1))
    hext = jnp.where(outside, 0.0, h).astype(BF16)

    def conv(u, c0):
        w = cw_ref[:, c0:c0 + FFN_COLS]
        up = pltpu.roll(u, 1, 0)
        dn = pltpu.roll(u, ext - 1, 0)
        r = up * w[0:1] + u * w[1:2] + dn * w[2:3] + cb_ref[:, c0:c0 + FFN_COLS]
        return r[HALO:HALO + tm]

    for j in range(D_FF // FFN_COLS):
        c0 = j * FFN_COLS
        gate = conv(_dot(hext, wup_ref[:, c0:c0 + FFN_COLS]), c0)
        val = conv(_dot(hext, wup_ref[:, D_FF + c0:D_FF + c0 + FFN_COLS]), D_FF + c0)
        a_ref[:, c0:c0 + FFN_COLS] = (gate * jax.nn.sigmoid(gate) * val).astype(BF16)
    x2 = x1e[HALO:HALO + tm] + m[5:6] * _dot(a_ref[...], wdn_ref[...])
    if final:
        x2 = x2 * _rms_scale(x2) * fw_ref[...]
    o_ref[...] = x2


def _mix_ffn(att, rec, four, x2d, mod4, nw2, wo_bf, wup_bf, conv_w, conv_b, wdn_bf, final_w, li, seg_len,
             mod_row, tm, final):
    rows = x2d.shape[0]
    tpb = seg_len // tm
    hb = tm // HALO
    nhb = rows // HALO
    mod_map = (lambda i: (li, i // tpb, 0, 0)) if mod_row is None else (lambda i: (li, mod_row, 0, 0))

    def row_specs(w):
        return [pl.BlockSpec((tm, w), lambda i: (i, 0)),
                pl.BlockSpec((HALO, w), lambda i: (jnp.maximum(i * hb - 1, 0), 0)),
                pl.BlockSpec((HALO, w), lambda i: (jnp.minimum((i + 1) * hb, nhb - 1), 0))]

    return pl.pallas_call(
        functools.partial(_mix_ffn_kernel, tm, tpb, final),
        grid=(rows // tm,),
        in_specs=row_specs(WIDTH_A) + row_specs(WIDTH_B) + row_specs(WIDTH_C) + row_specs(D_MODEL) + [
            pl.BlockSpec((None, None, 6, D_MODEL), mod_map),
            pl.BlockSpec((None, 1, D_MODEL), lambda i: (li, 0, 0)),
            _const_spec((None, D_MODEL, D_MODEL), lambda i: (li, 0, 0)),
            _const_spec((None, D_MODEL, 2 * D_FF), lambda i: (li, 0, 0)),
            pl.BlockSpec((None, 3, 2 * D_FF), lambda i: (li, 0, 0)),
            pl.BlockSpec((None, 1, 2 * D_FF), lambda i: (li, 0, 0)),
            _const_spec((None, D_FF, D_MODEL), lambda i: (li, 0, 0)),
            pl.BlockSpec((1, D_MODEL), lambda i: (0, 0)),
        ],
        out_specs=pl.BlockSpec((tm, D_MODEL), lambda i: (i, 0)),
        out_shape=jax.ShapeDtypeStruct((rows, D_MODEL), F32),
        scratch_shapes=[pltpu.VMEM((tm, D_FF), BF16)],
        compiler_params=_params(("parallel",)),
        name="mix_convffn_final" if final else "mix_convffn",
    )(att, att, att, rec, rec, rec, four, four, four, x2d, x2d, x2d, mod4, nw2, wo_bf, wup_bf, conv_w, conv_b,
      wdn_bf, final_w)


def _rope_tables(l):
    rows = l // GRID_W
    pos_r = np.repeat(np.arange(rows), GRID_W).astype(np.float32)
    pos_c = np.tile(np.arange(GRID_W), rows).astype(np.float32)
    half = DH_A // 2
    inv_freq = (ROPE_BASE ** (-np.arange(0, half, 2, dtype=np.float32) / half)).astype(np.float32)
    ang = np.concatenate([pos_r[:, None] * inv_freq] * 2 + [pos_c[:, None] * inv_freq] * 2, axis=-1)
    cos = np.cos(ang.astype(np.float32)).astype(np.float32)
    sin = np.sin(ang.astype(np.float32)).astype(np.float32)
    cos = np.tile(cos, (1, 2))
    sin = np.tile(sin, (1, 2))
    first = (np.arange(DV_A) % 32) < 16
    sin_a = np.where(first, -sin, 0.0).astype(np.float32)
    sin_b = np.where(first, 0.0, sin).astype(np.float32)
    return jnp.asarray(cos), jnp.asarray(sin_a), jnp.asarray(sin_b)


def kernel(x, c, ctx, c_ctx, w_ada, b_ada, norm1_w, norm2_w, w_in, lam_qk, subln_w, lb_param, hgrn_norm_w,
           w_fnet, w_out, w_up, conv_w, conv_b, w_down, final_norm_w):
    b, l, d = x.shape
    lc = ctx.shape[1]
    depth = w_ada.shape[0]
    mod_rows = 16
    assert d == D_MODEL and b < mod_rows
    assert l % INPROJ_ROWS == 0 and l % FFN_ROWS == 0 and l % ATTN_SUB == 0
    for seq in (l, lc):
        nchunks = seq // HG_CHUNK
        assert seq % (2 * HG_CHUNK) == 0 and nchunks % min(HG_STEPS_PER_TRIP, nchunks) == 0
    tm_l = INPROJ_ROWS
    tm_f = FFN_ROWS
    tm_c = lc

    lb_all = jnp.cumsum(jax.nn.softmax(lb_param.astype(F32), axis=0), axis=0)
    lb_all = lb_all - lb_all[0:1]
    lbc = jnp.stack([jnp.log(lb_all[:, 0]), jnp.log1p(-lb_all[:, 0]),
                     jnp.log(lb_all[:, 1]), jnp.log1p(-lb_all[:, 1])], axis=1)
    lq = lam_qk.astype(F32)
    lam_dyn = jnp.exp(jnp.sum(lq[:, 0] * lq[:, 1], axis=-1)) - jnp.exp(jnp.sum(lq[:, 2] * lq[:, 3], axis=-1))
    rope_tabs = _rope_tables(l)
    hconst = _hgrn_constants()
    dft_l = _dft_constants(l)
    dft_c = _dft_constants(lc)
    w_in_bf = w_in.astype(BF16)
    w_out_bf = w_out.astype(BF16)
    w_up_bf = w_up.astype(BF16)
    w_dn_bf = w_down.astype(BF16)
    w_fn_bf = w_fnet.astype(BF16)
    nw1 = norm1_w.reshape(depth, 1, d)
    nw2 = norm2_w.reshape(depth, 1, d)
    sub_w = subln_w.reshape(depth, 1, DV_A)
    hw = jnp.tile(hgrn_norm_w, (1, N_HEADS_B)).reshape(depth, 1, WIDTH_B)
    cb = conv_b.reshape(depth, 1, 2 * D_FF)
    fw = final_norm_w.reshape(1, d)

    cstack = jnp.concatenate([c, c_ctx[None, :], jnp.zeros((mod_rows - b - 1, d), F32)], axis=0)
    mod4 = _modulation(cstack, w_ada, b_ada).reshape(depth, mod_rows, 6, d)
    ctx_row = b

    zeros_state = jnp.zeros((b, WIDTH_B, WIDTH_B), F32)
    xl = x.reshape(b * l, d)
    xc = ctx.reshape(b * lc, d)
    for li in range(depth):
        last = li == depth - 1
        lam_init = 0.8 - 0.6 * math.exp(-0.3 * li)
        lam = (lam_dyn[li] + lam_init).reshape(1, 1)

        qa, ka, va, hg, uf = _inproj(xl, mod4, nw1, w_in_bf, lbc, rope_tabs, li, l, None, tm_l)
        qc, kc, vc, hgc, ufc = _inproj(xc, mod4, nw1, w_in_bf, lbc, None, li, lc, ctx_row,
                                       math.gcd(b * lc, tm_l))
        r3 = lambda a, n: a.reshape(b, n, a.shape[-1])

        att = _attention(lam, r3(qa, l), [r3(kc, lc), r3(ka, l)], [r3(vc, lc), r3(va, l)], sub_w, li,
                         1.0 - lam_init, 1, l, ATTN_SUB)
        rec_c, sf, sb = _hgrn(r3(hgc, lc), hconst, hw, zeros_state, zeros_state, li, states_only=last)
        rec, _, _ = _hgrn(r3(hg, l), hconst, hw, sf, sb, li)
        four = _fnet(r3(uf, l), dft_l, w_fn_bf, li)

        xl = _mix_ffn(att.reshape(b * l, -1), rec.reshape(b * l, -1), four.reshape(b * l, -1), xl, mod4, nw2,
                      w_out_bf, w_up_bf, conv_w, cb, w_dn_bf, fw, li, l, None, tm_f, last)

        if not last:
            att_c = _attention(lam, r3(qc, lc), [r3(kc, lc)], [r3(vc, lc)], sub_w, li, 1.0 - lam_init,
                               N_HEADS_A, lc, min(ATTN_SUB, lc) // 2)
            four_c = _fnet(r3(ufc, lc), dft_c, w_fn_bf, li)
            xc = _mix_ffn(att_c.reshape(b * lc, -1), rec_c.reshape(b * lc, -1), four_c.reshape(b * lc, -1), xc,
                          mod4, nw2, w_out_bf, w_up_bf, conv_w, cb, w_dn_bf, fw, li, lc, ctx_row, tm_c, False)
    return xl.reshape(b, l, d)
```

```python
import functools
import math

import numpy as np
import jax
import jax.numpy as jnp
from jax import lax
from jax.experimental import pallas as pl
from jax.experimental.pallas import tpu as pltpu

F32 = jnp.float32
BF16 = jnp.bfloat16

D_MODEL = 1024
GRID_W = 64
N_HEADS_A = 4
DH_A = 64
DV_A = 2 * DH_A
WIDTH_A = N_HEADS_A * DV_A
N_HEADS_B = 4
DK_B = 64
DV_B = 64
WIDTH_B = N_HEADS_B * DV_B
FN_GROUPS = 4
FN_GROUP_DIM = 64
WIDTH_C = FN_GROUPS * FN_GROUP_DIM
PROJ_WIDTH = 3 * WIDTH_A + 5 * WIDTH_B + WIDTH_C
D_FF = 2816
ROPE_BASE = 10000.0
EPS = 1e-6
LOG2E = math.log2(math.e)
exp2 = jnp.exp2
HG_UNIT = LOG2E

V7X_LANES = 128
V7X_BF16_SUBLANES = 16
V7X_VMEM_BYTES = 64 * 1024 * 1024
VMEM_LIMIT = V7X_VMEM_BYTES - 8 * 1024 * 1024

MOD_COLS = 1536
INPROJ_ROWS = 1024
INPROJ_SUB = 256
ATTN_SUB = 256
ATTN_AHEAD = 2
FFN_ROWS = 1024
FFN_COLS = 256
HALO = V7X_BF16_SUBLANES
HG_CHUNK = 64
HG_LEVELS = 6
HG_WIDTHS = tuple(HG_CHUNK >> (l + 1) for l in range(HG_LEVELS))
HG_STEPS_PER_TRIP = 8


def _params(sem):
    return pltpu.CompilerParams(dimension_semantics=sem, vmem_limit_bytes=VMEM_LIMIT)


def _const_spec(shape, index_map):
    return pl.BlockSpec(shape, index_map, pipeline_mode=pl.Buffered(1))


def _dot(a, b):
    return jnp.dot(a, b, preferred_element_type=F32)


def _dot_nt(a, b):
    return lax.dot_general(a, b, (((1,), (1,)), ((), ())), preferred_element_type=F32)


def _dot_tn(a, b):
    return lax.dot_general(a, b, (((0,), (0,)), ((), ())), preferred_element_type=F32)


def _split3(x):
    h = x.astype(BF16)
    r = x - h.astype(F32)
    m = r.astype(BF16)
    return h, m, (r - m.astype(F32)).astype(BF16)


def _split2(x):
    h = x.astype(BF16)
    return h, (x - h.astype(F32)).astype(BF16)


def _rms_scale(x):
    return lax.rsqrt(jnp.mean(x * x, axis=-1, keepdims=True) + EPS)


def _mod_kernel(c_ref, w_ref, b_ref, o_ref):
    c = c_ref[...]
    a = (c * jax.nn.sigmoid(c)).astype(BF16)
    o_ref[0] = _dot(a, w_ref[0].astype(BF16)) + b_ref[0]


def _modulation(cstack, w_ada, b_ada):
    depth, _, n = w_ada.shape
    tn = MOD_COLS
    rows = cstack.shape[0]
    return pl.pallas_call(
        _mod_kernel,
        grid=(depth, n // tn),
        in_specs=[
            pl.BlockSpec((rows, D_MODEL), lambda l, j: (0, 0)),
            pl.BlockSpec((1, D_MODEL, tn), lambda l, j: (l, 0, j)),
            pl.BlockSpec((1, 1, tn), lambda l, j: (l, 0, j)),
        ],
        out_specs=pl.BlockSpec((1, rows, tn), lambda l, j: (l, 0, j)),
        out_shape=jax.ShapeDtypeStruct((depth, rows, n), F32),
        compiler_params=_params(("arbitrary", "arbitrary")),
        name="modulation",
    )(cstack, w_ada, b_ada.reshape(depth, 1, n))


def _log_forget(z, log_lb, log1m_lb):
    ls = jnp.minimum(z, 0.0) - jnp.log(1.0 + jnp.exp(-jnp.abs(z)))
    b = log1m_lb + ls
    return jnp.maximum(log_lb, b) + jnp.log(1.0 + jnp.exp(-jnp.abs(log_lb - b)))


def _inproj_kernel(rope, x_ref, mod_ref, nw_ref, w_ref, lbc_ref, *rest):
    if rope:
        cos_ref, sa_ref, sb_ref, qa_ref, ka_ref, va_ref, hg_ref, uf_ref = rest
    else:
        qa_ref, ka_ref, va_ref, hg_ref, uf_ref = rest
    m = mod_ref[...]
    lbc = lbc_ref[...]
    c0 = 3 * WIDTH_A
    for r0 in range(0, x_ref.shape[0], INPROJ_SUB):
        rs = slice(r0, r0 + INPROJ_SUB)
        x = x_ref[rs, :]
        y = x * _rms_scale(x) * nw_ref[...]
        h = (y * (1.0 + m[1:2]) + m[0:1]).astype(BF16)

        def proj(col, n):
            return _dot(h, w_ref[:, col:col + n])

        for base, ref, scale in ((0, qa_ref, DH_A ** -0.5 * LOG2E), (WIDTH_A, ka_ref, None)):
            t = proj(base, WIDTH_A)
            for hd in range(N_HEADS_A):
                th = t[:, hd * DV_A:(hd + 1) * DV_A]
                if rope:
                    th = (th * cos_ref[rs, :]
                          + pltpu.roll(th, V7X_LANES - 16, 1) * sa_ref[rs, :]
                          + pltpu.roll(th, 16, 1) * sb_ref[rs, :])
                if scale is not None:
                    th = th * scale
                ref[rs, hd * DV_A:(hd + 1) * DV_A] = th.astype(BF16)
        va_ref[rs, :] = proj(2 * WIDTH_A, WIDTH_A).astype(BF16)
        hg_ref[rs, 0:WIDTH_B] = proj(c0, WIDTH_B)
        for dirn in range(2):
            z = proj(c0 + WIDTH_B * (1 + dirn), WIDTH_B)
            hg_ref[rs, WIDTH_B * (1 + dirn):WIDTH_B * (2 + dirn)] = _log_forget(
                z, lbc[2 * dirn:2 * dirn + 1], lbc[2 * dirn + 1:2 * dirn + 2])
        hg_ref[rs, 3 * WIDTH_B:5 * WIDTH_B] = proj(c0 + 3 * WIDTH_B, 2 * WIDTH_B)
        uf_ref[rs, :] = proj(c0 + 5 * WIDTH_B, WIDTH_C).astype(BF16)


def _inproj(x2d, mod4, nw, w_bf, lbc, rope_tabs, li, seg_len, mod_row, tm):
    rows = x2d.shape[0]
    tpb = seg_len // tm
    rope = rope_tabs is not None
    mod_map = (lambda i: (li, i // tpb, 0, 0)) if mod_row is None else (lambda i: (li, mod_row, 0, 0))
    in_specs = [
        pl.BlockSpec((tm, D_MODEL), lambda i: (i, 0)),
        pl.BlockSpec((None, None, 6, D_MODEL), mod_map),
        pl.BlockSpec((None, 1, D_MODEL), lambda i: (li, 0, 0)),
        _const_spec((None, D_MODEL, PROJ_WIDTH), lambda i: (li, 0, 0)),
        pl.BlockSpec((None, 4, WIDTH_B), lambda i: (li, 0, 0)),
    ]
    args = [x2d, mod4, nw, w_bf, lbc]
    if rope:
        in_specs += [pl.BlockSpec((tm, DV_A), lambda i: (i % tpb, 0))] * 3
        args += list(rope_tabs)
    out_shape = (
        jax.ShapeDtypeStruct((rows, WIDTH_A), BF16),
        jax.ShapeDtypeStruct((rows, WIDTH_A), BF16),
        jax.ShapeDtypeStruct((rows, WIDTH_A), BF16),
        jax.ShapeDtypeStruct((rows, 5 * WIDTH_B), F32),
        jax.ShapeDtypeStruct((rows, WIDTH_C), BF16),
    )
    out_specs = (
        pl.BlockSpec((tm, WIDTH_A), lambda i: (i, 0)),
        pl.BlockSpec((tm, WIDTH_A), lambda i: (i, 0)),
        pl.BlockSpec((tm, WIDTH_A), lambda i: (i, 0)),
        pl.BlockSpec((tm, 5 * WIDTH_B), lambda i: (i, 0)),
        pl.BlockSpec((tm, WIDTH_C), lambda i: (i, 0)),
    )
    return pl.pallas_call(
        functools.partial(_inproj_kernel, rope),
        grid=(rows // tm,),
        in_specs=in_specs,
        out_specs=out_specs,
        out_shape=out_shape,
        compiler_params=_params(("parallel",)),
        name="inproj_rope" if rope else "inproj",
    )(*args)


def _attn_kernel(nseg, hps, tq, sub, out_scale, lam_ref, q_ref, *rest):
    k_refs = rest[:nseg]
    v_refs = rest[nseg:2 * nseg]
    w_ref, o_ref = rest[2 * nseg:]
    lam = lam_ref[0, 0]
    wn = w_ref[...] * out_scale
    lane = lax.broadcasted_iota(jnp.int32, (sub, DV_A), 1)
    for hh in range(hps):
        hl = slice(hh * DV_A, (hh + 1) * DV_A)
        vexts = []
        for v_ref in v_refs:
            v = v_ref[:, hl]
            vl = lax.broadcasted_iota(jnp.int32, v.shape, 1)
            vexts.append(jnp.concatenate([v, jnp.where(vl == 0, 1.0, 0.0).astype(BF16)], axis=1))
        def block_scores(r0):
            q = q_ref[r0:r0 + sub, hl].astype(F32)
            qs = jnp.concatenate([jnp.where(lane < DH_A, q, 0.0), jnp.where(lane >= DH_A, q, 0.0)],
                                 axis=0).astype(BF16)
            return [_dot(qs, k_ref[hl, :]) for k_ref in k_refs]

        starts = list(range(0, tq, sub))
        ahead = [block_scores(r) for r in starts[:ATTN_AHEAD]]
        for bi, r0 in enumerate(starts):
            scores = ahead.pop(0)
            if bi + ATTN_AHEAD < len(starts):
                ahead.append(block_scores(starts[bi + ATTN_AHEAD]))
            mx = jnp.max(scores[0], axis=-1, keepdims=True)
            for s in scores[1:]:
                mx = jnp.maximum(mx, jnp.max(s, axis=-1, keepdims=True))
            acc = sum(_dot(exp2(s - mx).astype(BF16), vext) for s, vext in zip(scores, vexts))
            r = acc[:, :DV_A] / acc[:, DV_A:DV_A + 1]
            o = r[:sub] - lam * r[sub:]
            o_ref[r0:r0 + sub, hl] = (o * _rms_scale(o) * wn).astype(BF16)


def _attention(lam, q, ks, vs, subln_w, li, out_scale, hps, tq, sub):
    b, lq, _ = q.shape
    nseg = len(ks)
    w = hps * DV_A
    in_specs = [
        pl.BlockSpec(memory_space=pltpu.SMEM),
        pl.BlockSpec((None, tq, w), lambda bi, h, i: (bi, i, h)),
    ]
    ks = [jnp.swapaxes(k, 1, 2) for k in ks]
    for arr in ks:
        in_specs.append(pl.BlockSpec((None, w, arr.shape[2]), lambda bi, h, i: (bi, h, 0)))
    for arr in vs:
        in_specs.append(pl.BlockSpec((None, arr.shape[1], w), lambda bi, h, i: (bi, 0, h)))
    in_specs.append(pl.BlockSpec((None, 1, DV_A), lambda bi, h, i: (li, 0, 0)))
    return pl.pallas_call(
        functools.partial(_attn_kernel, nseg, hps, tq, sub, out_scale),
        grid=(b, N_HEADS_A // hps, lq // tq),
        in_specs=in_specs,
        out_specs=pl.BlockSpec((None, tq, w), lambda bi, h, i: (bi, i, h)),
        out_shape=jax.ShapeDtypeStruct((b, lq, WIDTH_A), BF16),
        compiler_params=_params(("parallel", "parallel", "arbitrary")),
        name="diff_attention_%dseg" % nseg,
    )(lam, q, *ks, *vs, subln_w)


def _hgrn_constants():
    c = HG_CHUNK
    t = np.arange(c)[:, None]
    r = np.arange(c)[None, :]
    amask = []
    lomask = []
    for w in HG_WIDTHS:
        hi = (t % (2 * w)) >= w
        same_pair = (t // (2 * w)) == (r // (2 * w))
        amask.append(same_pair & (hi != ((r % (2 * w)) >= w)))
        lomask.append(np.broadcast_to(~hi, (c, c)))
    psum = np.zeros((2 * c, 2 * c), np.float32)
    psum[:c, :c] = r <= t
    psum[c:, c:] = r >= t
    tile = lambda a: np.tile(np.asarray(a, np.float32), (1, N_HEADS_B))
    amask = np.stack([tile(a) for a in amask])
    lomask = np.stack([tile(a) for a in lomask])
    hd = np.arange(WIDTH_B) // DK_B
    bd = (hd[:, None] == hd[None, :]).astype(np.float32)
    return (jnp.asarray(psum, BF16), jnp.asarray(amask, F32), jnp.asarray(lomask, F32), jnp.asarray(bd, F32))


def _hgrn_kernel(nc, states_only, q_ref, gf_ref, gb_ref, v_ref, og_ref, psum_ref, amask_ref, lomask_ref,
                 bd_ref, hw_ref, sf0_ref, sb0_ref, *outs_and_scratch):
    if states_only:
        rec_ref = None
        sf_ref, sb_ref = outs_and_scratch[:2]
        scratch = outs_and_scratch[2:]
    else:
        rec_ref, sf_ref, sb_ref = outs_and_scratch[:3]
        scratch = outs_and_scratch[3:]
    (sfp_ref, sbn_ref, cf_ref, cbs_ref, qa_ref, ka_ref, qb_ref, kb_ref, aa_ref, pa_ref, ab_ref, pb_ref, oa_ref,
     ob_ref) = scratch
    c = HG_CHUNK
    bd = bd_ref[...]
    bd_bf = bd.astype(BF16)
    lane_head = lax.broadcasted_iota(jnp.int32, (c, WIDTH_B), 1) // DK_B

    def stack_heads(a):
        return jnp.concatenate([jnp.where(lane_head == h, a, 0.0).astype(BF16) for h in range(N_HEADS_B)],
                               axis=0)

    def rows(ref, n):
        if isinstance(n, int):
            return ref[n * c:(n + 1) * c, :]
        return ref[pl.ds(pl.multiple_of(n * c, c), c), :]

    sf_ref[...] = sf0_ref[...]
    sb_ref[...] = sb0_ref[...]

    def decayed_keys(i, kt_ref, dec_ref):
        nf = i
        nb = nc - 1 - i
        gf = rows(gf_ref, nf)
        gb = rows(gb_ref, nb)
        g = jnp.concatenate([gf, gb], axis=0)
        sums = sum(_dot(psum_ref[...], part) for part in _split3(g))
        cf = sums[:c]
        cbs = sums[c:]
        cf_ref[nf] = cf * HG_UNIT
        cbs_ref[nb] = cbs * HG_UNIT
        tot_f = cf[c - 1:c]
        tot_b = cbs[0:1]
        kt_ref[0] = (1.0 - jnp.exp(gf)) * jnp.exp(tot_f - cf)
        kt_ref[1] = (1.0 - jnp.exp(gb)) * jnp.exp(tot_b - cbs)
        dec_ref[0:1, :] = jnp.exp(tot_f)
        dec_ref[8:9, :] = jnp.exp(tot_b)

    def apply_step(i, kt_ref, dec_ref):
        nf = i
        nb = nc - 1 - i
        upd_f = _dot_tn(rows(v_ref, nf).astype(BF16), kt_ref[0].astype(BF16)) * bd
        upd_b = _dot_tn(rows(v_ref, nb).astype(BF16), kt_ref[1].astype(BF16)) * bd
        stf = sf_ref[...]
        sfp_ref[nf] = stf.astype(BF16)
        sf_ref[...] = stf * dec_ref[0:1, :] + upd_f
        stb = sb_ref[...]
        sbn_ref[nb] = stb.astype(BF16)
        sb_ref[...] = stb * dec_ref[8:9, :] + upd_b

    pre = ((qa_ref, oa_ref), (qb_ref, ob_ref))
    decayed_keys(0, *pre[0])

    per_trip = min(HG_STEPS_PER_TRIP, nc)

    def pre_body(j, carry):
        for u in range(per_trip):
            i = per_trip * j + u
            apply_step(i, *pre[u % 2])
            decayed_keys(jnp.minimum(i + 1, nc - 1), *pre[1 - u % 2])
        return carry

    lax.fori_loop(0, nc // per_trip, pre_body, 0)
    if states_only:
        return

    def prepare(n, qop_ref, kop_ref):
        q = rows(q_ref, n)
        gf = rows(gf_ref, n) * HG_UNIT
        gb = rows(gb_ref, n) * HG_UNIT
        cf = cf_ref[n]
        cbs = cbs_ref[n]
        kf = 1.0 - exp2(gf)
        kb = 1.0 - exp2(gb)
        for lvl, w in enumerate(HG_WIDTHS):
            lo = lomask_ref[lvl] > 0.5
            if 2 * w >= 8:
                anchor = lambda a, r: jnp.broadcast_to(
                    a.reshape(c // (2 * w), 2 * w, WIDTH_B)[:, r:r + 1, :],
                    (c // (2 * w), 2 * w, WIDTH_B)).reshape(c, WIDTH_B)
                d_f = cf - anchor(cf, w - 1)
                d_b = cbs - anchor(cbs, w)
                eq = exp2(jnp.where(lo, d_b, d_f))
                ek = exp2(-jnp.where(lo, d_f, d_b))
            elif w == 2:
                r4 = lax.broadcasted_iota(jnp.int32, (c, WIDTH_B), 0) & 3
                gf_up, gf_dn = pltpu.roll(gf, c - 1, 0), pltpu.roll(gf, 1, 0)
                gb_up, gb_dn = pltpu.roll(gb, c - 1, 0), pltpu.roll(gb, 1, 0)
                eq = exp2(jnp.where(r4 == 0, gb + gb_up, jnp.where(r4 == 1, gb,
                              jnp.where(r4 == 2, gf, gf + gf_dn))))
                ek = exp2(jnp.where(r4 == 0, gf_up, jnp.where(r4 == 3, gb_dn, 0.0)))
            else:
                eq = exp2(jnp.where(lo, gb, gf))
                ek = None
            klev = jnp.where(lo, kf, kb)
            qop_ref[lvl] = q * eq
            kop_ref[lvl] = klev if ek is None else klev * ek
        qop_ref[HG_LEVELS] = q * exp2(cf)
        qop_ref[HG_LEVELS + 1] = q * exp2(cbs)
        qop_ref[HG_LEVELS + 2] = q * (kf + kb)

    def scores(n, qop_ref, kop_ref, a_ref, part_ref):
        a_all = jnp.zeros((c, N_HEADS_B * c), F32)
        for lvl in range(HG_LEVELS):
            a_all = a_all + _dot_nt(qop_ref[lvl].astype(BF16), stack_heads(kop_ref[lvl])) * amask_ref[lvl]
        a_ref[...] = a_all
        part = _dot(qop_ref[HG_LEVELS + 2].astype(BF16), bd_bf) * rows(v_ref, n)
        part = part + _dot_nt(qop_ref[HG_LEVELS].astype(BF16), sfp_ref[n])
        part_ref[...] = part + _dot_nt(qop_ref[HG_LEVELS + 1].astype(BF16), sbn_ref[n])

    def values(n, a_ref, part_ref, o_ref):
        o_ref[...] = (_dot(a_ref[...].astype(BF16), stack_heads(rows(v_ref, n)))
                      + part_ref[...])

    def finish(n, o_ref):
        o = o_ref[...]
        sq_h, sq_l = _split2(o * o)
        ms = _dot(jnp.concatenate([sq_h, sq_l], axis=0), bd_bf)
        ms = (ms[:c] + ms[c:]) * (1.0 / DV_B)
        og = rows(og_ref, n)
        rec = o * lax.rsqrt(ms + EPS) * hw_ref[...] * (og * jax.nn.sigmoid(og))
        rec_ref[pl.ds(pl.multiple_of(n * c, c), c), :] = rec.astype(BF16)

    ops = ((qa_ref, ka_ref), (qb_ref, kb_ref))
    sc = ((aa_ref, pa_ref), (ab_ref, pb_ref))
    ob = (oa_ref, ob_ref)
    last = nc - 1

    def step(m, par, clamp):
        idx = (lambda i: jnp.minimum(i, last)) if clamp else (lambda i: i)
        if clamp or m >= 0:
            finish(m, ob[par])
        if clamp or m + 1 >= 0:
            values(idx(m + 1), *sc[1 - par], ob[1 - par])
        if clamp or m + 2 >= 0:
            scores(idx(m + 2), *ops[par], *sc[par])
        prepare(idx(m + 3), *ops[1 - par])

    for m in (-3, -2, -1):
        step(m, m % 2, False)

    per_trip = min(HG_STEPS_PER_TRIP, nc)

    def body(j, carry):
        for u in range(per_trip):
            step(per_trip * j + u, u % 2, True)
        return carry

    lax.fori_loop(0, nc // per_trip, body, 0)


def _hgrn(hg, hconst, hw, sf0, sb0, li, states_only=False):
    b, l, _ = hg.shape
    nc = l // HG_CHUNK
    psum, amask, lomask, bd = hconst
    col = lambda j: pl.BlockSpec((None, l, WIDTH_B), lambda bi: (bi, 0, j))
    full = lambda a: _const_spec(a.shape, lambda bi: (0,) * a.ndim)
    st_spec = pl.BlockSpec((None, WIDTH_B, WIDTH_B), lambda bi: (bi, 0, 0))
    state_scratch = pltpu.VMEM((nc, WIDTH_B, WIDTH_B), BF16)
    sums_scratch = pltpu.VMEM((nc, HG_CHUNK, WIDTH_B), F32)
    q_operands = pltpu.VMEM((HG_LEVELS + 3, HG_CHUNK, WIDTH_B), F32)
    k_operands = pltpu.VMEM((HG_LEVELS, HG_CHUNK, WIDTH_B), F32)
    chunk_f32 = pltpu.VMEM((HG_CHUNK, WIDTH_B), F32)
    out_specs = [pl.BlockSpec((None, l, WIDTH_B), lambda bi: (bi, 0, 0)), st_spec, st_spec]
    out_shape = [jax.ShapeDtypeStruct((b, l, WIDTH_B), BF16),
                 jax.ShapeDtypeStruct((b, WIDTH_B, WIDTH_B), F32),
                 jax.ShapeDtypeStruct((b, WIDTH_B, WIDTH_B), F32)]
    if states_only:
        out_specs, out_shape = out_specs[1:], out_shape[1:]
    outs = pl.pallas_call(
        functools.partial(_hgrn_kernel, nc, states_only),
        grid=(b,),
        in_specs=[col(0), col(1), col(2), col(3), col(4), full(psum), full(amask),
                  full(lomask), full(bd), pl.BlockSpec((None, 1, WIDTH_B), lambda bi: (li, 0, 0)),
                  st_spec, st_spec],
        out_specs=tuple(out_specs),
        out_shape=tuple(out_shape),
        scratch_shapes=[state_scratch, state_scratch, sums_scratch, sums_scratch,
                        q_operands, k_operands, q_operands, k_operands,
                        chunk_f32, chunk_f32, chunk_f32, chunk_f32, chunk_f32, chunk_f32],
        compiler_params=_params(("parallel",)),
        name="hgrn_states_l%d" % l if states_only else "hgrn_l%d" % l,
    )(hg, hg, hg, hg, hg, psum, amask, lomask, bd, hw, sf0, sb0)
    return (None,) + tuple(outs) if states_only else tuple(outs)


def _dft_constants(l):
    k = np.arange(FN_GROUP_DIM)
    ang = 2.0 * np.pi * ((k[:, None] * k[None, :]) % FN_GROUP_DIM) / FN_GROUP_DIM
    eye = np.eye(FN_GROUPS)
    cs = np.concatenate([np.kron(eye, np.cos(ang)), np.kron(eye, np.sin(ang))], axis=1)
    n = np.arange(l)
    angl = 2.0 * np.pi * ((n[:, None] * n[None, :]) % l) / l
    return tuple(jnp.asarray(a, F32).astype(BF16) for a in (cs, np.cos(angl), np.sin(angl)))


def _fnet_kernel(scale, u_ref, cs_ref, cl_ref, sl_ref, wf_ref, o_ref):
    t = _dot(u_ref[...], cs_ref[...])
    uc = t[:, :WIDTH_C].astype(BF16)
    us = t[:, WIDTH_C:].astype(BF16)
    y = (_dot(cl_ref[...], uc) - _dot(sl_ref[...], us)) * scale
    o_ref[...] = _dot(y.astype(BF16), wf_ref[...]).astype(BF16)


def _fnet(uf, dft, wf_bf, li):
    b, l, _ = uf.shape
    cs, cl, sl = dft
    scale = 1.0 / math.sqrt(l * FN_GROUP_DIM)
    return pl.pallas_call(
        functools.partial(_fnet_kernel, scale),
        grid=(b,),
        in_specs=[
            pl.BlockSpec((None, l, WIDTH_C), lambda bi: (bi, 0, 0)),
            _const_spec(cs.shape, lambda bi: (0, 0)),
            _const_spec(cl.shape, lambda bi: (0, 0)),
            _const_spec(sl.shape, lambda bi: (0, 0)),
            _const_spec((None, WIDTH_C, WIDTH_C), lambda bi: (li, 0, 0)),
        ],
        out_specs=pl.BlockSpec((None, l, WIDTH_C), lambda bi: (bi, 0, 0)),
        out_shape=jax.ShapeDtypeStruct((b, l, WIDTH_C), BF16),
        compiler_params=_params(("parallel",)),
        name="fnet_l%d" % l,
    )(uf, cs, cl, sl, wf_bf)


def _mix_ffn_kernel(tm, tpb, final, att_ref, attp_ref, attn_ref, rec_ref, recp_ref, recn_ref, four_ref,
                    fourp_ref, fourn_ref, x_ref, xp_ref, xn_ref, mod_ref, nw_ref, wo_ref, wup_ref, cw_ref,
                    cb_ref, wdn_ref, fw_ref, o_ref, a_ref):
    t = pl.program_id(0) % tpb
    ext = tm + 2 * HALO
    m = mod_ref[...]

    def with_halo(prev_ref, main_ref, next_ref):
        return jnp.concatenate([prev_ref[...], main_ref[...], next_ref[...]], axis=0)

    mix = jnp.concatenate([with_halo(attp_ref, att_ref, attn_ref), with_halo(recp_ref, rec_ref, recn_ref),
                           with_halo(fourp_ref, four_ref, fourn_ref)], axis=1)
    x1e = with_halo(xp_ref, x_ref, xn_ref) + m[2:3] * _dot(mix, wo_ref[...])
    h = (x1e * _rms_scale(x1e) * nw_ref[...]) * (1.0 + m[4:5]) + m[3:4]
    row = lax.broadcasted_iota(jnp.int32, (ext, 1), 0)
    outside = ((row < HALO) & (t == 0)) | ((row >= HALO + tm) & (t == tpb - 1))
    hext = jnp.where(outside, 0.0, h).astype(BF16)

    def conv(u, c0):
        w = cw_ref[:, c0:c0 + FFN_COLS]
        up = pltpu.roll(u, 1, 0)
        dn = pltpu.roll(u, ext - 1, 0)
        r = up * w[0:1] + u * w[1:2] + dn * w[2:3] + cb_ref[:, c0:c0 + FFN_COLS]
        return r[HALO:HALO + tm]

    for j in range(D_FF // FFN_COLS):
        c0 = j * FFN_COLS
        gate = conv(_dot(hext, wup_ref[:, c0:c0 + FFN_COLS]), c0)
        val = conv(_dot(hext, wup_ref[:, D_FF + c0:D_FF + c0 + FFN_COLS]), D_FF + c0)
        a_ref[:, c0:c0 + FFN_COLS] = (gate * jax.nn.sigmoid(gate) * val).astype(BF16)
    x2 = x1e[HALO:HALO + tm] + m[5:6] * _dot(a_ref[...], wdn_ref[...])
    if final:
        x2 = x2 * _rms_scale(x2) * fw_ref[...]
    o_ref[...] = x2


def _mix_ffn(att, rec, four, x2d, mod4, nw2, wo_bf, wup_bf, conv_w, conv_b, wdn_bf, final_w, li, seg_len,
             mod_row, tm, final):
    rows = x2d.shape[0]
    tpb = seg_len // tm
    hb = tm // HALO
    nhb = rows // HALO
    mod_map = (lambda i: (li, i // tpb, 0, 0)) if mod_row is None else (lambda i: (li, mod_row, 0, 0))

    def row_specs(w):
        return [pl.BlockSpec((tm, w), lambda i: (i, 0)),
                pl.BlockSpec((HALO, w), lambda i: (jnp.maximum(i * hb - 1, 0), 0)),
                pl.BlockSpec((HALO, w), lambda i: (jnp.minimum((i + 1) * hb, nhb - 1), 0))]

    return pl.pallas_call(
        functools.partial(_mix_ffn_kernel, tm, tpb, final),
        grid=(rows // tm,),
        in_specs=row_specs(WIDTH_A) + row_specs(WIDTH_B) + row_specs(WIDTH_C) + row_specs(D_MODEL) + [
            pl.BlockSpec((None, None, 6, D_MODEL), mod_map),
            pl.BlockSpec((None, 1, D_MODEL), lambda i: (li, 0, 0)),
            _const_spec((None, D_MODEL, D_MODEL), lambda i: (li, 0, 0)),
            _const_spec((None, D_MODEL, 2 * D_FF), lambda i: (li, 0, 0)),
            pl.BlockSpec((None, 3, 2 * D_FF), lambda i: (li, 0, 0)),
            pl.BlockSpec((None, 1, 2 * D_FF), lambda i: (li, 0, 0)),
            _const_spec((None, D_FF, D_MODEL), lambda i: (li, 0, 0)),
            pl.BlockSpec((1, D_MODEL), lambda i: (0, 0)),
        ],
        out_specs=pl.BlockSpec((tm, D_MODEL), lambda i: (i, 0)),
        out_shape=jax.ShapeDtypeStruct((rows, D_MODEL), F32),
        scratch_shapes=[pltpu.VMEM((tm, D_FF), BF16)],
        compiler_params=_params(("parallel",)),
        name="mix_convffn_final" if final else "mix_convffn",
    )(att, att, att, rec, rec, rec, four, four, four, x2d, x2d, x2d, mod4, nw2, wo_bf, wup_bf, conv_w, conv_b,
      wdn_bf, final_w)


def _rope_tables(l):
    rows = l // GRID_W
    pos_r = np.repeat(np.arange(rows), GRID_W).astype(np.float32)
    pos_c = np.tile(np.arange(GRID_W), rows).astype(np.float32)
    half = DH_A // 2
    inv_freq = (ROPE_BASE ** (-np.arange(0, half, 2, dtype=np.float32) / half)).astype(np.float32)
    ang = np.concatenate([pos_r[:, None] * inv_freq] * 2 + [pos_c[:, None] * inv_freq] * 2, axis=-1)
    cos = np.cos(ang.astype(np.float32)).astype(np.float32)
    sin = np.sin(ang.astype(np.float32)).astype(np.float32)
    cos = np.tile(cos, (1, 2))
    sin = np.tile(sin, (1, 2))
    first = (np.arange(DV_A) % 32) < 16
    sin_a = np.where(first, -sin, 0.0).astype(np.float32)
    sin_b = np.where(first, 0.0, sin).astype(np.float32)
    return jnp.asarray(cos), jnp.asarray(sin_a), jnp.asarray(sin_b)


def kernel(x, c, ctx, c_ctx, w_ada, b_ada, norm1_w, norm2_w, w_in, lam_qk, subln_w, lb_param, hgrn_norm_w,
           w_fnet, w_out, w_up, conv_w, conv_b, w_down, final_norm_w):
    b, l, d = x.shape
    lc = ctx.shape[1]
    depth = w_ada.shape[0]
    mod_rows = 16
    assert d == D_MODEL and b < mod_rows
    assert l % INPROJ_ROWS == 0 and l % FFN_ROWS == 0 and l % ATTN_SUB == 0
    for seq in (l, lc):
        nchunks = seq // HG_CHUNK
        assert seq % (2 * HG_CHUNK) == 0 and nchunks % min(HG_STEPS_PER_TRIP, nchunks) == 0
    tm_l = INPROJ_ROWS
    tm_f = FFN_ROWS
    tm_c = lc

    lb_all = jnp.cumsum(jax.nn.softmax(lb_param.astype(F32), axis=0), axis=0)
    lb_all = lb_all - lb_all[0:1]
    lbc = jnp.stack([jnp.log(lb_all[:, 0]), jnp.log1p(-lb_all[:, 0]),
                     jnp.log(lb_all[:, 1]), jnp.log1p(-lb_all[:, 1])], axis=1)
    lq = lam_qk.astype(F32)
    lam_dyn = jnp.exp(jnp.sum(lq[:, 0] * lq[:, 1], axis=-1)) - jnp.exp(jnp.sum(lq[:, 2] * lq[:, 3], axis=-1))
    rope_tabs = _rope_tables(l)
    hconst = _hgrn_constants()
    dft_l = _dft_constants(l)
    dft_c = _dft_constants(lc)
    w_in_bf = w_in.astype(BF16)
    w_out_bf = w_out.astype(BF16)
    w_up_bf = w_up.astype(BF16)
    w_dn_bf = w_down.astype(BF16)
    w_fn_bf = w_fnet.astype(BF16)
    nw1 = norm1_w.reshape(depth, 1, d)
    nw2 = norm2_w.reshape(depth, 1, d)
    sub_w = subln_w.reshape(depth, 1, DV_A)
    hw = jnp.tile(hgrn_norm_w, (1, N_HEADS_B)).reshape(depth, 1, WIDTH_B)
    cb = conv_b.reshape(depth, 1, 2 * D_FF)
    fw = final_norm_w.reshape(1, d)

    cstack = jnp.concatenate([c, c_ctx[None, :], jnp.zeros((mod_rows - b - 1, d), F32)], axis=0)
    mod4 = _modulation(cstack, w_ada, b_ada).reshape(depth, mod_rows, 6, d)
    ctx_row = b

    zeros_state = jnp.zeros((b, WIDTH_B, WIDTH_B), F32)
    xl = x.reshape(b * l, d)
    xc = ctx.reshape(b * lc, d)
    for li in range(depth):
        last = li == depth - 1
        lam_init = 0.8 - 0.6 * math.exp(-0.3 * li)
        lam = (lam_dyn[li] + lam_init).reshape(1, 1)

        qa, ka, va, hg, uf = _inproj(xl, mod4, nw1, w_in_bf, lbc, rope_tabs, li, l, None, tm_l)
        qc, kc, vc, hgc, ufc = _inproj(xc, mod4, nw1, w_in_bf, lbc, None, li, lc, ctx_row,
                                       math.gcd(b * lc, tm_l))
        r3 = lambda a, n: a.reshape(b, n, a.shape[-1])

        att = _attention(lam, r3(qa, l), [r3(kc, lc), r3(ka, l)], [r3(vc, lc), r3(va, l)], sub_w, li,
                         1.0 - lam_init, 1, l, ATTN_SUB)
        rec_c, sf, sb = _hgrn(r3(hgc, lc), hconst, hw, zeros_state, zeros_state, li, states_only=last)
        rec, _, _ = _hgrn(r3(hg, l), hconst, hw, sf, sb, li)
        four = _fnet(r3(uf, l), dft_l, w_fn_bf, li)

        xl = _mix_ffn(att.reshape(b * l, -1), rec.reshape(b * l, -1), four.reshape(b * l, -1), xl, mod4, nw2,
                      w_out_bf, w_up_bf, conv_w, cb, w_dn_bf, fw, li, l, None, tm_f, last)

        if not last:
            att_c = _attention(lam, r3(qc, lc), [r3(kc, lc)], [r3(vc, lc)], sub_w, li, 1.0 - lam_init,
                               N_HEADS_A, lc, min(ATTN_SUB, lc) // 2)
            four_c = _fnet(r3(ufc, lc), dft_c, w_fn_bf, li)
            xc = _mix_ffn(att_c.reshape(b * lc, -1), rec_c.reshape(b * lc, -1), four_c.reshape(b * lc, -1), xc,
                          mod4, nw2, w_out_bf, w_up_bf, conv_w, cb, w_dn_bf, fw, li, lc, ctx_row, tm_c, False)
    return xl.reshape(b, l, d)
```
